```python
import math
import jax, jax.numpy as jnp
from jax import lax
import numpy as np

D_MODEL = 1024
BATCH = 4
SEQ = 4096
DEPTH = 1
DEC_BATCH = 8
DEC_SEQ = 16
PAST_LEN = 4096

CHUNK = 64
Q_BLOCK = 128
ROPE_THETA = 10000.0
EPS = 1e-6
NEG = -1e30
MACARON = 0.5
D_FF = 2816
ATT_HEADS = 4
ATT_DH = 64
ATT_QK_W = ATT_HEADS * 2 * ATT_DH
ATT_V_W = ATT_HEADS * 2 * ATT_DH
RW_HEAD = 64
RW_W = D_MODEL // 2
RW_HEADS = RW_W // RW_HEAD
W_LORA = 64
A_LORA = 64
G_LORA = 128
SHIFT_W = 3 * RW_W + W_LORA + A_LORA + G_LORA
LNX_EPS = 64e-5
GATE_W = 2 * D_MODEL
IN_W = 2 * ATT_QK_W + ATT_V_W + SHIFT_W + GATE_W

kernel_name = 'hybrid_diffattn_rwkv7_macaron_step'


def rms_norm(x, g, eps=EPS):
    xf = x.astype(jnp.float32)
    y = xf * lax.rsqrt(jnp.mean(xf * xf, axis=-1, keepdims=True) + eps)
    return (y * g.astype(jnp.float32)).astype(x.dtype)


def swiglu(x, w_in, w_out):
    gate, up = jnp.split(x @ w_in, 2, axis=-1)
    return (jax.nn.silu(gate) * up) @ w_out


def sandwich_ffn(x, pre_g, w_in, w_out, post_g):
    return x + MACARON * rms_norm(swiglu(rms_norm(x, pre_g), w_in, w_out), post_g)


def rope(x, pos):
    half = ATT_DH // 2
    inv = ROPE_THETA ** (-jnp.arange(half, dtype=jnp.float32) / half)
    ang = pos.astype(jnp.float32)[:, None] * inv[None, :]
    cos = jnp.cos(ang)[None, :, None, None, :]
    sin = jnp.sin(ang)[None, :, None, None, :]
    xf = x.astype(jnp.float32)
    x1, x2 = xf[..., :half], xf[..., half:]
    return jnp.concatenate([x1 * cos - x2 * sin, x1 * sin + x2 * cos], axis=-1).astype(x.dtype)


def diff_core(q, k, v, lam, mask):
    s = jnp.einsum('bqhmd,bkhmd->bhmqk', q, k).astype(jnp.float32) * (ATT_DH ** -0.5)
    if mask is not None:
        s = jnp.where(mask, s, NEG)
    pr = jax.nn.softmax(s, axis=-1)
    a = pr[:, :, 0] - lam * pr[:, :, 1]
    return jnp.einsum('bhqk,bkhe->bqhe', a, v.astype(jnp.float32))


def diff_attention(z_q, z_k, z_v, pos, lam, lam_init, subln_g, cache_k, cache_v):
    B, T, _ = z_q.shape
    q = rope(z_q.reshape(B, T, ATT_HEADS, 2, ATT_DH), pos)
    k = rope(z_k.reshape(B, T, ATT_HEADS, 2, ATT_DH), pos)
    v = z_v.reshape(B, T, ATT_HEADS, 2 * ATT_DH)
    if cache_k is None:
        nb = T // Q_BLOCK
        qb = jnp.moveaxis(q.reshape(B, nb, Q_BLOCK, ATT_HEADS, 2, ATT_DH), 1, 0)
        key_chunk = jnp.arange(T) // CHUNK

        def block(args):
            qi, bi = args
            q_chunk = (bi * Q_BLOCK + jnp.arange(Q_BLOCK)) // CHUNK
            mask = key_chunk[None, :] <= q_chunk[:, None]
            return diff_core(qi, k, v, lam, mask)

        o = lax.map(block, (qb, jnp.arange(nb)))
        o = jnp.moveaxis(o, 0, 1).reshape(B, T, ATT_HEADS, 2 * ATT_DH)
    else:
        keys = jnp.concatenate([cache_k.astype(k.dtype), k], axis=1)
        vals = jnp.concatenate([cache_v.astype(v.dtype), v], axis=1)
        o = diff_core(q, keys, vals, lam, None)
    o = o * lax.rsqrt(jnp.mean(o * o, axis=-1, keepdims=True) + EPS) * subln_g.astype(jnp.float32) * (1.0 - lam_init)
    return o.reshape(B, T, ATT_V_W).astype(z_v.dtype), k, v


def wkv7_scan(r, decay, k, v, kk, a, S0):
    def step(S, inp):
        r_t, w_t, k_t, v_t, kk_t, a_t = inp
        sa = jnp.einsum('bhij,bhj->bhi', S, -kk_t)
        S = S * w_t[:, :, None, :] + sa[..., None] * (kk_t * a_t)[:, :, None, :] + v_t[..., None] * k_t[:, :, None, :]
        y = jnp.einsum('bhij,bhj->bhi', S, r_t)
        return S, y

    xs = tuple(jnp.moveaxis(t, 1, 0) for t in (r, decay, k, v, kk, a))
    S, ys = lax.scan(step, S0, xs)
    return jnp.moveaxis(ys, 0, 1), S


def rwkv7_branch(z_rw, S0, shift0, p):
    B, T, _ = z_rw.shape
    prev = jnp.concatenate([shift0.astype(z_rw.dtype), z_rw[:, :-1]], axis=1)
    zs = (z_rw + (prev - z_rw) * p['rwkv_mu']).astype(jnp.float32)
    r, k, v, zw, za, zg = jnp.split(zs, [RW_W, 2 * RW_W, 3 * RW_W, 3 * RW_W + W_LORA, 3 * RW_W + W_LORA + A_LORA], axis=-1)
    w = -jax.nn.softplus(-(p['rwkv_w0'].astype(jnp.float32) + jnp.tanh(zw) @ p['rwkv_w2'].astype(jnp.float32))) - 0.5
    decay = jnp.exp(-jnp.exp(w))
    a = jax.nn.sigmoid(p['rwkv_a0'].astype(jnp.float32) + za @ p['rwkv_a2'].astype(jnp.float32))
    g = jax.nn.sigmoid(zg) @ p['rwkv_g2'].astype(jnp.float32)
    heads = lambda t: t.reshape(B, T, RW_HEADS, RW_HEAD)
    kk = heads(k * p['rwkv_k_k'].astype(jnp.float32))
    kk = kk / jnp.maximum(jnp.sqrt(jnp.sum(kk * kk, axis=-1, keepdims=True)), 1e-12)
    k = k * (1.0 + (a - 1.0) * p['rwkv_k_a'].astype(jnp.float32))
    rh, kh, vh, ah, dh = heads(r), heads(k), heads(v), heads(a), heads(decay)
    y, S = wkv7_scan(rh, dh, kh, vh, kk, ah, S0.astype(jnp.float32))
    mu = jnp.mean(y, axis=-1, keepdims=True)
    var = jnp.mean(jnp.square(y - mu), axis=-1, keepdims=True)
    y = ((y - mu) * lax.rsqrt(var + LNX_EPS)).reshape(B, T, RW_W)
    y = y * p['rwkv_lnx_g'].astype(jnp.float32) + p['rwkv_lnx_b'].astype(jnp.float32)
    bonus = jnp.sum(rh * kh * p['rwkv_r_k'].astype(jnp.float32), axis=-1, keepdims=True) * vh
    out = (y + bonus.reshape(B, T, RW_W)) * g
    return out.astype(z_rw.dtype), S, z_rw[:, -1:]


def layer(x, pos, l, p, cache_k, cache_v, S0, shift0):
    h = sandwich_ffn(x, p['ffn1_pre_g'], p['ffn1_w_in'], p['ffn1_w_out'], p['ffn1_post_g'])
    u = rms_norm(h, p['mix_pre_g'])
    z = u @ p['w_in']
    o1 = ATT_QK_W
    o2 = 2 * ATT_QK_W
    o3 = o2 + ATT_V_W
    o4 = o3 + SHIFT_W
    z_q, z_k, z_v, z_rw, z_g = jnp.split(z, [o1, o2, o3, o4], axis=-1)
    lam_init = 0.8 - 0.6 * math.exp(-0.3 * l)
    f32 = lambda t: t.astype(jnp.float32)
    lam = (jnp.exp(jnp.sum(f32(p['att_lambda_q1']) * f32(p['att_lambda_k1'])))
           - jnp.exp(jnp.sum(f32(p['att_lambda_q2']) * f32(p['att_lambda_k2']))) + lam_init)
    o_att, k_rows, v_rows = diff_attention(z_q, z_k, z_v, pos, lam, lam_init, p['att_subln_g'], cache_k, cache_v)
    o_rw, S, shift = rwkv7_branch(z_rw, S0, shift0, p)
    gate_att, gate_rw = jnp.split(jax.nn.sigmoid(z_g), 2, axis=-1)
    merged = gate_att * (o_att @ p['w_o_att']) + gate_rw * (o_rw @ p['w_o_rwkv'])
    h = h + rms_norm(merged @ p['w_out'], p['mix_post_g'])
    y = sandwich_ffn(h, p['ffn2_pre_g'], p['ffn2_w_in'], p['ffn2_w_out'], p['ffn2_post_g'])
    return y, k_rows, v_rows, S, shift


def setup_inputs(seed: int = 0) -> dict:
    key = jax.random.key(seed)
    ks = iter(jax.random.split(key, 48))
    L = DEPTH

    def nrm(shape, scale):
        return scale * jax.random.normal(next(ks), shape, jnp.float32)

    def gain(n):
        return 1.0 + nrm((L, n), 0.05)

    return {
        'x_prompt': nrm((BATCH, SEQ, D_MODEL), 1.0),
        'x_sample': nrm((DEC_BATCH, DEC_SEQ, D_MODEL), 1.0),
        'cache_att_k': nrm((L, DEC_BATCH, PAST_LEN, ATT_HEADS, 2, ATT_DH), 1.0),
        'cache_att_v': nrm((L, DEC_BATCH, PAST_LEN, ATT_HEADS, 2 * ATT_DH), 1.0),
        'state_rwkv': nrm((L, DEC_BATCH, RW_HEADS, RW_HEAD, RW_HEAD), 0.3),
        'state_shift': nrm((L, DEC_BATCH, 1, SHIFT_W), 1.0),
        'ffn1_pre_g': gain(D_MODEL),
        'ffn1_w_in': nrm((L, D_MODEL, 2 * D_FF), D_MODEL ** -0.5),
        'ffn1_w_out': nrm((L, D_FF, D_MODEL), D_FF ** -0.5),
        'ffn1_post_g': gain(D_MODEL),
        'mix_pre_g': gain(D_MODEL),
        'w_in': nrm((L, D_MODEL, IN_W), D_MODEL ** -0.5),
        'att_lambda_q1': nrm((L, ATT_DH), 0.1),
        'att_lambda_k1': nrm((L, ATT_DH), 0.1),
        'att_lambda_q2': nrm((L, ATT_DH), 0.1),
        'att_lambda_k2': nrm((L, ATT_DH), 0.1),
        'att_subln_g': gain(2 * ATT_DH),
        'rwkv_mu': jax.random.uniform(next(ks), (L, SHIFT_W), jnp.float32),
        'rwkv_w0': jax.random.uniform(next(ks), (L, RW_W), jnp.float32, minval=-6.0, maxval=0.0),
        'rwkv_w2': nrm((L, W_LORA, RW_W), 0.5 * W_LORA ** -0.5),
        'rwkv_a0': nrm((L, RW_W), 0.1),
        'rwkv_a2': nrm((L, A_LORA, RW_W), 0.5 * A_LORA ** -0.5),
        'rwkv_g2': nrm((L, G_LORA, RW_W), G_LORA ** -0.5),
        'rwkv_k_k': 0.85 + nrm((L, RW_W), 0.05),
        'rwkv_k_a': 1.0 + nrm((L, RW_W), 0.05),
        'rwkv_r_k': nrm((L, RW_HEADS, RW_HEAD), 0.1),
        'rwkv_lnx_g': gain(RW_W),
        'rwkv_lnx_b': nrm((L, RW_W), 0.02),
        'w_o_att': nrm((L, ATT_V_W, D_MODEL), ATT_V_W ** -0.5),
        'w_o_rwkv': nrm((L, RW_W, D_MODEL), RW_W ** -0.5),
        'w_out': nrm((L, D_MODEL, D_MODEL), D_MODEL ** -0.5),
        'mix_post_g': gain(D_MODEL),
        'ffn2_pre_g': gain(D_MODEL),
        'ffn2_w_in': nrm((L, D_MODEL, 2 * D_FF), D_MODEL ** -0.5),
        'ffn2_w_out': nrm((L, D_FF, D_MODEL), D_FF ** -0.5),
        'ffn2_post_g': gain(D_MODEL),
    }


def reference(x_prompt, x_sample, cache_att_k, cache_att_v, state_rwkv, state_shift,
              ffn1_pre_g, ffn1_w_in, ffn1_w_out, ffn1_post_g, mix_pre_g, w_in,
              att_lambda_q1, att_lambda_k1, att_lambda_q2, att_lambda_k2, att_subln_g,
              rwkv_mu, rwkv_w0, rwkv_w2, rwkv_a0, rwkv_a2, rwkv_g2, rwkv_k_k, rwkv_k_a,
              rwkv_r_k, rwkv_lnx_g, rwkv_lnx_b, w_o_att, w_o_rwkv, w_out, mix_post_g,
              ffn2_pre_g, ffn2_w_in, ffn2_w_out, ffn2_post_g):
    Bp, Tp, _ = x_prompt.shape
    Bs, Ts, _ = x_sample.shape
    past = cache_att_k.shape[2]
    pos_p = jnp.arange(Tp)
    pos_s = past + jnp.arange(Ts)
    xp, xs = x_prompt, x_sample
    kp_l, vp_l, sp_l, shp_l = [], [], [], []
    ks_l, vs_l, ss_l, shs_l = [], [], [], []
    for l in range(DEPTH):
        p = dict(
            ffn1_pre_g=ffn1_pre_g[l], ffn1_w_in=ffn1_w_in[l], ffn1_w_out=ffn1_w_out[l], ffn1_post_g=ffn1_post_g[l],
            mix_pre_g=mix_pre_g[l], w_in=w_in[l],
            att_lambda_q1=att_lambda_q1[l], att_lambda_k1=att_lambda_k1[l],
            att_lambda_q2=att_lambda_q2[l], att_lambda_k2=att_lambda_k2[l], att_subln_g=att_subln_g[l],
            rwkv_mu=rwkv_mu[l], rwkv_w0=rwkv_w0[l], rwkv_w2=rwkv_w2[l], rwkv_a0=rwkv_a0[l], rwkv_a2=rwkv_a2[l],
            rwkv_g2=rwkv_g2[l], rwkv_k_k=rwkv_k_k[l], rwkv_k_a=rwkv_k_a[l], rwkv_r_k=rwkv_r_k[l],
            rwkv_lnx_g=rwkv_lnx_g[l], rwkv_lnx_b=rwkv_lnx_b[l],
            w_o_att=w_o_att[l], w_o_rwkv=w_o_rwkv[l], w_out=w_out[l], mix_post_g=mix_post_g[l],
            ffn2_pre_g=ffn2_pre_g[l], ffn2_w_in=ffn2_w_in[l], ffn2_w_out=ffn2_w_out[l], ffn2_post_g=ffn2_post_g[l],
        )
        S0p = jnp.zeros((Bp, RW_HEADS, RW_HEAD, RW_HEAD), jnp.float32)
        sh0p = jnp.zeros((Bp, 1, SHIFT_W), xp.dtype)
        xp, kp, vp, Sp, shp = layer(xp, pos_p, l, p, None, None, S0p, sh0p)
        xs, kn, vn, Sn, shn = layer(xs, pos_s, l, p, cache_att_k[l], cache_att_v[l], state_rwkv[l], state_shift[l])
        kp_l.append(kp); vp_l.append(vp); sp_l.append(Sp); shp_l.append(shp)
        ks_l.append(kn); vs_l.append(vn); ss_l.append(Sn); shs_l.append(shn)
    new_k_prompt = jnp.stack(kp_l, 0)
    new_v_prompt = jnp.stack(vp_l, 0)
    new_rwkv_prompt = jnp.stack(sp_l, 0)
    new_shift_prompt = jnp.stack(shp_l, 0)
    new_k_sample = jnp.stack(ks_l, 0)
    new_v_sample = jnp.stack(vs_l, 0)
    new_rwkv_sample = jnp.stack(ss_l, 0)
    new_shift_sample = jnp.stack(shs_l, 0)
    return (xp, xs, new_k_prompt, new_v_prompt, new_rwkv_prompt, new_shift_prompt,
            new_k_sample, new_v_sample, new_rwkv_sample, new_shift_sample)
```

```python
import functools
import math

import jax
import jax.numpy as jnp
from jax import lax
from jax.experimental import pallas as pl
from jax.experimental.pallas import tpu as pltpu

F32 = jnp.float32
BF16 = jnp.bfloat16

D_MODEL = 1024
D_FF = 2816
CHUNK = 64
ROPE_THETA = 10000.0
EPS = 1e-6
MACARON = 0.5
ATT_HEADS = 4
ATT_DH = 64
ATT_W = ATT_HEADS * 2 * ATT_DH
RW_HEAD = 64
RW_W = D_MODEL // 2
RW_HEADS = RW_W // RW_HEAD
W_LORA = 64
A_LORA = 64
G_LORA = 128
SHIFT_W = 3 * RW_W + W_LORA + A_LORA + G_LORA
LNX_EPS = 64e-5
GATE_W = 2 * D_MODEL
IN_W = 3 * ATT_W + SHIFT_W + GATE_W

LANES = 128
MXU_W = 256
VMEM_LIMIT = 56 * 1024 * 1024

WKV_C = 64
WKV_GROUP = MXU_W // RW_HEAD
WKV_GW = WKV_GROUP * RW_HEAD
WKV_NG = RW_W // WKV_GW


def _params(sem):
    return pltpu.CompilerParams(dimension_semantics=sem, vmem_limit_bytes=VMEM_LIMIT)


def _const_spec(shape):
    nd = len(shape)
    return pl.BlockSpec(shape, lambda *_: (0,) * nd, pipeline_mode=pl.Buffered(1))


def _rms(x, g):
    return x * lax.rsqrt(jnp.mean(x * x, axis=-1, keepdims=True) + EPS) * g


def _dot(a, b):
    return jnp.dot(a, b, preferred_element_type=F32)


def _dot_nt(a, b):
    return lax.dot_general(a, b, (((1,), (1,)), ((), ())), preferred_element_type=F32)


def _dot_tn(a, b):
    return lax.dot_general(a, b, (((0,), (0,)), ((), ())), preferred_element_type=F32)


def _swiglu_ffn(xn_bf, w_in_ref, w_out_ref):
    hh = _dot(xn_bf, w_in_ref[...])
    gate = hh[:, :D_FF]
    up = hh[:, D_FF:]
    act = (gate * jax.nn.sigmoid(gate) * up).astype(BF16)
    return _dot(act, w_out_ref[...])


def _ffn_body(x_ref, pre_ref, win_ref, wout_ref, post_ref, o_ref):
    x = x_ref[...]
    f = _swiglu_ffn(_rms(x, pre_ref[...]).astype(BF16), win_ref, wout_ref)
    o_ref[...] = x + MACARON * _rms(f, post_ref[...])


def _ffn_call(x, pre_g, w_in, w_out, post_g, tm):
    m = x.shape[0]
    row = lambda w: pl.BlockSpec((tm, w), lambda i: (i, 0))
    return pl.pallas_call(
        _ffn_body,
        grid=(m // tm,),
        in_specs=[row(D_MODEL), _const_spec((1, D_MODEL)), _const_spec((D_MODEL, 2 * D_FF)),
                  _const_spec((D_FF, D_MODEL)), _const_spec((1, D_MODEL))],
        out_specs=row(D_MODEL),
        out_shape=jax.ShapeDtypeStruct((m, D_MODEL), F32),
        compiler_params=_params(("parallel",)),
        name="ffn",
    )(x, pre_g, w_in, w_out, post_g)


def _rope_tables(pos):
    half = ATT_DH // 2
    inv = ROPE_THETA ** (-jnp.arange(half, dtype=F32) / half)
    ang = pos.astype(F32)[:, None] * inv[None, :]
    cos, sin = jnp.cos(ang), jnp.sin(ang)
    cos_t = jnp.tile(jnp.concatenate([cos, cos], axis=-1), (1, ATT_W // ATT_DH))
    sin_t = jnp.tile(jnp.concatenate([-sin, sin], axis=-1), (1, ATT_W // ATT_DH))
    return cos_t, sin_t


def _rope(x, cos_t, sin_t):
    n = x.shape[-1]
    half = ATT_DH // 2
    lane = lax.broadcasted_iota(jnp.int32, x.shape, 1)
    swapped = jnp.where(lane % ATT_DH < half, pltpu.roll(x, n - half, 1), pltpu.roll(x, half, 1))
    return x * cos_t + swapped * sin_t


def _mixproj_body(h_ref, g_ref, w_ref, cos_ref, sin_ref, q_ref, k_ref, kb_ref, v_ref, vb_ref, rw_ref, gate_ref):
    u = _rms(h_ref[...], g_ref[...]).astype(BF16)
    z = _dot(u, w_ref[...])
    cos_t, sin_t = cos_ref[...], sin_ref[...]
    q_ref[...] = (_rope(z[:, :ATT_W], cos_t, sin_t) * (ATT_DH ** -0.5)).astype(BF16)
    k = _rope(z[:, ATT_W:2 * ATT_W], cos_t, sin_t)
    k_ref[...] = k
    kb_ref[...] = k.astype(BF16)
    v = z[:, 2 * ATT_W:3 * ATT_W]
    v_ref[...] = v
    vb_ref[...] = v.astype(BF16)
    rw_ref[...] = z[:, 3 * ATT_W:3 * ATT_W + SHIFT_W]
    gate_ref[...] = jax.nn.sigmoid(z[:, 3 * ATT_W + SHIFT_W:]).astype(BF16)


def _mixproj_call(h, mix_pre_g, w_in, cos_t, sin_t, tm):
    m = h.shape[0]
    ntab = cos_t.shape[0] // tm
    row = lambda w: pl.BlockSpec((tm, w), lambda i: (i, 0))
    tab = pl.BlockSpec((tm, ATT_W), lambda i: (i % ntab, 0))
    sds = lambda w, dt: jax.ShapeDtypeStruct((m, w), dt)
    return pl.pallas_call(
        _mixproj_body,
        grid=(m // tm,),
        in_specs=[row(D_MODEL), _const_spec((1, D_MODEL)), _const_spec((D_MODEL, IN_W)), tab, tab],
        out_specs=[row(ATT_W)] * 5 + [row(SHIFT_W), row(GATE_W)],
        out_shape=[sds(ATT_W, BF16), sds(ATT_W, F32), sds(ATT_W, BF16), sds(ATT_W, F32), sds(ATT_W, BF16),
                   sds(SHIFT_W, F32), sds(GATE_W, BF16)],
        compiler_params=_params(("parallel",)),
        name="mixproj",
    )(h, mix_pre_g, w_in, cos_t, sin_t)


def _lambda(lq1_ref, lk1_ref, lq2_ref, lk2_ref, lam_init):
    s1 = jnp.sum(lq1_ref[...] * lk1_ref[...], axis=-1, keepdims=True)
    s2 = jnp.sum(lq2_ref[...] * lk2_ref[...], axis=-1, keepdims=True)
    return jnp.exp(s1) - jnp.exp(s2) + lam_init


def _subln(o, g, lam_init):
    return o * lax.rsqrt(jnp.mean(o * o, axis=-1, keepdims=True) + EPS) * g * (1.0 - lam_init)


def _softmax_step(s, v_bf, m_ref, l_ref, acc_ref):
    m_prev = m_ref[...]
    m_new = jnp.maximum(m_prev, jnp.max(s, axis=-1, keepdims=True))
    alpha = jnp.exp(m_prev - m_new)
    p = jnp.exp(s - m_new)
    l_ref[...] = alpha * l_ref[...] + jnp.sum(p, axis=-1, keepdims=True)
    acc_ref[...] = alpha * acc_ref[...] + _dot(p.astype(BF16), v_bf)
    m_ref[...] = m_new


def _attn_prompt_body(lam_init, blk, q_ref, k_ref, v_ref, lq1_ref, lk1_ref, lq2_ref, lk2_ref, g_ref, o_ref,
                      qq_ref, m_ref, l_ref, acc_ref):
    i = pl.program_id(2)
    q = q_ref[0]
    lane = lax.broadcasted_iota(jnp.int32, q.shape, 1)
    zero = jnp.zeros_like(q)
    qq_ref[:blk, :] = jnp.where(lane < ATT_DH, q, zero)
    qq_ref[blk:, :] = jnp.where(lane >= ATT_DH, q, zero)
    m_ref[...] = jnp.full(m_ref.shape, -jnp.inf, F32)
    l_ref[...] = jnp.zeros(l_ref.shape, F32)
    acc_ref[...] = jnp.zeros(acc_ref.shape, F32)

    def full_block(j, carry):
        start = pl.multiple_of(j * blk, blk)
        s = _dot_nt(qq_ref[...], k_ref[0, pl.ds(start, blk), :])
        _softmax_step(s, v_ref[0, pl.ds(start, blk), :], m_ref, l_ref, acc_ref)
        return carry

    lax.fori_loop(0, i, full_block, 0)

    start = pl.multiple_of(i * blk, blk)
    s = _dot_nt(qq_ref[...], k_ref[0, pl.ds(start, blk), :])
    row = lax.broadcasted_iota(jnp.int32, s.shape, 0) % blk
    col = lax.broadcasted_iota(jnp.int32, s.shape, 1)
    s = jnp.where(col // CHUNK <= row // CHUNK, s, -jnp.inf)
    _softmax_step(s, v_ref[0, pl.ds(start, blk), :], m_ref, l_ref, acc_ref)

    lam = _lambda(lq1_ref, lk1_ref, lq2_ref, lk2_ref, lam_init)
    o = acc_ref[...] / l_ref[...]
    o = o[:blk] - lam * o[blk:]
    o_ref[0] = _subln(o, g_ref[...], lam_init).astype(o_ref.dtype)


def _attn_prompt_call(q, k, v, lq1, lk1, lq2, lk2, subln_g, lam_init, blk):
    b, t, _ = q.shape
    hw = 2 * ATT_DH
    qspec = pl.BlockSpec((1, blk, hw), lambda bi, hi, i: (bi, i, hi))
    kvspec = pl.BlockSpec((1, t, hw), lambda bi, hi, i: (bi, 0, hi))
    return pl.pallas_call(
        functools.partial(_attn_prompt_body, lam_init, blk),
        grid=(b, ATT_HEADS, t // blk),
        in_specs=[qspec, kvspec, kvspec] + [_const_spec((1, ATT_DH))] * 4 + [_const_spec((1, hw))],
        out_specs=qspec,
        out_shape=jax.ShapeDtypeStruct((b, t, ATT_W), BF16),
        scratch_shapes=[pltpu.VMEM((2 * blk, hw), BF16), pltpu.VMEM((2 * blk, 1), F32),
                        pltpu.VMEM((2 * blk, 1), F32), pltpu.VMEM((2 * blk, hw), F32)],
        compiler_params=_params(("parallel", "parallel", "arbitrary")),
        name="attn_prompt",
    )(q, k, v, lq1, lk1, lq2, lk2, subln_g)


def _attn_sample_body(lam_init, ts, q_ref, ck_ref, cv_ref, kn_ref, vn_ref, lq1_ref, lk1_ref, lq2_ref, lk2_ref,
                      g_ref, o_ref, qq_ref, m_ref, l_ref, acc_ref):
    j = pl.program_id(1)
    nmaps = 2 * ATT_HEADS
    hw = 2 * ATT_DH

    @pl.when(j == 0)
    def _():
        q = jnp.concatenate([q_ref[0]] * nmaps, axis=0)
        row = lax.broadcasted_iota(jnp.int32, q.shape, 0)
        lane = lax.broadcasted_iota(jnp.int32, q.shape, 1)
        qq_ref[...] = jnp.where(row // ts == lane // ATT_DH, q, jnp.zeros_like(q))
        m_ref[...] = jnp.full(m_ref.shape, -jnp.inf, F32)
        l_ref[...] = jnp.zeros(l_ref.shape, F32)
        acc_ref[...] = jnp.zeros(acc_ref.shape, F32)

    def step(k_bf, v_bf):
        _softmax_step(_dot_nt(qq_ref[...], k_bf), v_bf, m_ref, l_ref, acc_ref)

    step(ck_ref[0].astype(BF16), cv_ref[0].astype(BF16))

    @pl.when(j == pl.num_programs(1) - 1)
    def _():
        step(kn_ref[0], vn_ref[0])
        lam = _lambda(lq1_ref, lk1_ref, lq2_ref, lk2_ref, lam_init)
        o = acc_ref[...] / l_ref[...]
        for h in range(ATT_HEADS):
            r0 = 2 * h * ts
            oh = o[r0:r0 + ts, h * hw:(h + 1) * hw] - lam * o[r0 + ts:r0 + 2 * ts, h * hw:(h + 1) * hw]
            o_ref[0, :, h * hw:(h + 1) * hw] = _subln(oh, g_ref[...], lam_init).astype(o_ref.dtype)


def _attn_sample_call(q, cache_k, cache_v, k_new, v_new, lq1, lk1, lq2, lk2, subln_g, lam_init, tk):
    b, ts, _ = q.shape
    past = cache_k.shape[1]
    rows = 2 * ATT_HEADS * ts
    new = pl.BlockSpec((1, ts, ATT_W), lambda bi, j: (bi, 0, 0))
    cache = pl.BlockSpec((1, tk, ATT_W), lambda bi, j: (bi, j, 0))
    return pl.pallas_call(
        functools.partial(_attn_sample_body, lam_init, ts),
        grid=(b, past // tk),
        in_specs=[new, cache, cache, new, new] + [_const_spec((1, ATT_DH))] * 4 + [_const_spec((1, 2 * ATT_DH))],
        out_specs=new,
        out_shape=jax.ShapeDtypeStruct((b, ts, ATT_W), BF16),
        scratch_shapes=[pltpu.VMEM((rows, ATT_W), BF16), pltpu.VMEM((rows, 1), F32), pltpu.VMEM((rows, 1), F32),
                        pltpu.VMEM((rows, ATT_W), F32)],
        compiler_params=_params(("parallel", "arbitrary")),
        name="attn_sample",
    )(q, cache_k, cache_v, k_new, v_new, lq1, lk1, lq2, lk2, subln_g)


def _split2(x):
    hi = x.astype(BF16)
    return hi, (x - hi.astype(F32)).astype(BF16)


def _split3(x):
    hi = x.astype(BF16)
    r1 = x - hi.astype(F32)
    mid = r1.astype(BF16)
    return hi, mid, (r1 - mid.astype(F32)).astype(BF16)


def _head_sum(x, ones_bd):
    hi, lo = _split2(x)
    return _dot(hi, ones_bd) + _dot(lo, ones_bd)


def _bd_expand(x):
    c = x.shape[0]
    xt = jnp.concatenate([x] * WKV_GROUP, axis=0)
    row = lax.broadcasted_iota(jnp.int32, xt.shape, 0)
    lane = lax.broadcasted_iota(jnp.int32, xt.shape, 1)
    return jnp.where(row // c == lane // RW_HEAD, xt, jnp.zeros_like(xt))


def _wkv_chunk_terms(at, rt, bt, kt, bh, kh, v):
    c = at.shape[0]
    slab = (c, WKV_GROUP * c)
    t_idx = lax.broadcasted_iota(jnp.int32, slab, 0)
    s_idx = lax.broadcasted_iota(jnp.int32, slab, 1) % c
    strict = s_idx < t_idx
    incl = s_idx <= t_idx

    ar = jnp.concatenate([at, rt], axis=0).astype(BF16)
    sb = _dot_nt(ar, _bd_expand(bt).astype(BF16))
    sk = _dot_nt(ar, _bd_expand(kt).astype(BF16))
    l_ab = jnp.where(strict, sb[:c], 0.0)
    l_ak = jnp.where(strict, sk[:c], 0.0)
    m_rb = jnp.where(incl, sb[c:], 0.0)
    m_rk = jnp.where(incl, sk[c:], 0.0)

    tinv = jnp.where(s_idx == t_idx, 1.0, 0.0) + l_ab
    lp = l_ab
    power = 1
    while 2 * power < c:
        lp = _dot(lp.astype(BF16), _bd_expand(lp).astype(BF16))
        tinv = tinv + _dot(tinv.astype(BF16), _bd_expand(lp).astype(BF16))
        power *= 2

    v_exp = _bd_expand(v).astype(BF16)
    tinv_bf = tinv.astype(BF16)
    ta = _dot(tinv_bf, _bd_expand(at).astype(BF16))
    uv = _dot(tinv_bf, _bd_expand(_dot(l_ak.astype(BF16), v_exp)).astype(BF16))

    gw = at.shape[1]
    bd = (lax.broadcasted_iota(jnp.int32, (gw, gw), 0) // RW_HEAD
          == lax.broadcasted_iota(jnp.int32, (gw, gw), 1) // RW_HEAD)
    p_lr = jnp.where(bd, _dot_tn(bh.astype(BF16), ta.astype(BF16)), 0.0)
    q_t = jnp.where(bd, _dot_tn(jnp.concatenate([uv, v], axis=0).astype(BF16),
                                jnp.concatenate([bh, kh], axis=0).astype(BF16)), 0.0)
    m_rb_bf = m_rb.astype(BF16)
    ya = rt + _dot(m_rb_bf, _bd_expand(ta).astype(BF16))
    yb = _dot(m_rb_bf, _bd_expand(uv).astype(BF16)) + _dot(m_rk.astype(BF16), v_exp)
    return p_lr, q_t, ya, yb


def _rwkv_body(tb, t_valid, z_ref, s0_ref, sh0_ref, mu_ref, w0_ref, wa_ref, a0_ref, g2_ref, kk_ref, ka_ref,
               rk_ref, lng_ref, lnb_ref, o_ref, s_ref, h_ref, carry_ref):
    ib = pl.program_id(1)
    c = WKV_C
    nc = tb // c

    @pl.when(ib == 0)
    def _():
        carry_ref[...] = sh0_ref[0]
        for g in range(WKV_NG):
            blocks = [s0_ref[0, g * WKV_GROUP + h] for h in range(WKV_GROUP)]
            h_ref[g] = _bd_expand(jnp.concatenate(blocks, axis=1))

    z = z_ref[0]
    row = lax.broadcasted_iota(jnp.int32, z.shape, 0)
    prev = jnp.where(row == 0, carry_ref[...], pltpu.roll(z, 1, 0))
    carry_ref[...] = z[tb - 1:tb, :]
    zs = z + (prev - z) * mu_ref[...]

    r = zs[:, :RW_W]
    k = zs[:, RW_W:2 * RW_W]
    v = zs[:, 2 * RW_W:3 * RW_W]
    zwa = zs[:, 3 * RW_W:3 * RW_W + W_LORA + A_LORA]
    zg = zs[:, 3 * RW_W + W_LORA + A_LORA:]
    lane = lax.broadcasted_iota(jnp.int32, zwa.shape, 1)
    lora = _dot(jnp.where(lane < W_LORA, jnp.tanh(zwa), zwa).astype(BF16), wa_ref[...])
    ew = math.exp(-0.5) * jax.nn.sigmoid(w0_ref[...] + lora[:, :RW_W])
    a = jax.nn.sigmoid(a0_ref[...] + lora[:, RW_W:])
    gate = _dot(jax.nn.sigmoid(zg).astype(BF16), g2_ref[...])

    ones_bd = (lax.broadcasted_iota(jnp.int32, (RW_W, RW_W), 0) // RW_HEAD
               == lax.broadcasted_iota(jnp.int32, (RW_W, RW_W), 1) // RW_HEAD).astype(BF16)
    kk = k * kk_ref[...]
    kk = kk / jnp.maximum(jnp.sqrt(_head_sum(kk * kk, ones_bd)), 1e-12)
    k = k * (1.0 + (a - 1.0) * ka_ref[...])
    bonus = _head_sum(r * k * rk_ref[...], ones_bd) * v

    if t_valid is not None:
        live = lax.broadcasted_iota(jnp.int32, ew.shape, 0) + ib * tb < t_valid
        ew = jnp.where(live, ew, 0.0)
        k = jnp.where(live, k, 0.0)
        kk = jnp.where(live, kk, 0.0)
        v = jnp.where(live, v, 0.0)

    ti = lax.broadcasted_iota(jnp.int32, (tb, tb), 0)
    si = lax.broadcasted_iota(jnp.int32, (tb, tb), 1)
    tri = jnp.logical_and(si <= ti, si // c == ti // c).astype(BF16)
    e1, e2, e3 = _split3(ew)
    cum = -(_dot(tri, e1) + _dot(tri, e2) + _dot(tri, e3))

    ys = []
    for ci in range(nc):
        sl = slice(ci * c, (ci + 1) * c)
        cum_c = cum[sl]
        cum_end = cum_c[c - 1:c, :]
        grow = jnp.exp(-cum_c)
        to_end = jnp.exp(cum_end - cum_c)
        b = kk[sl] * a[sl]
        at = -kk[sl] * jnp.exp(cum_c + ew[sl])
        rt = r[sl] * jnp.exp(cum_c)
        gamma_end = jnp.exp(cum_end)
        y_groups = []
        for g in range(WKV_NG):
            gl = slice(g * WKV_GW, (g + 1) * WKV_GW)
            p_lr, q_t, ya, yb = _wkv_chunk_terms(at[:, gl], rt[:, gl], (b * grow)[:, gl], (k[sl] * grow)[:, gl],
                                               (b * to_end)[:, gl], (k[sl] * to_end)[:, gl], v[sl][:, gl])
            s = h_ref[g]
            s_bf = s.astype(BF16)
            y_groups.append(_dot_nt(ya.astype(BF16), s_bf) + yb)
            h_ref[g] = s * gamma_end[:, gl] + _dot_nt(s_bf, p_lr.astype(BF16)) + q_t
        ys.append(jnp.concatenate(y_groups, axis=1))
    y = jnp.concatenate(ys, axis=0)

    mean = _head_sum(y, ones_bd) * (1.0 / RW_HEAD)
    d = y - mean
    var = _head_sum(d * d, ones_bd) * (1.0 / RW_HEAD)
    yn = d * lax.rsqrt(var + LNX_EPS) * lng_ref[...] + lnb_ref[...]
    o_ref[0] = ((yn + bonus) * gate).astype(o_ref.dtype)

    @pl.when(ib == pl.num_programs(1) - 1)
    def _():
        for g in range(WKV_NG):
            h = h_ref[g]
            for hh in range(WKV_GROUP):
                blk = h[hh * RW_HEAD:(hh + 1) * RW_HEAD, hh * RW_HEAD:(hh + 1) * RW_HEAD]
                s_ref[0, g * WKV_GROUP + hh] = blk


def _rwkv_call(z_rw, s0, shift0, p, tb, t_valid):
    b, t, _ = z_rw.shape
    vec = lambda w: _const_spec((1, w))
    return pl.pallas_call(
        functools.partial(_rwkv_body, tb, t_valid),
        grid=(b, t // tb),
        in_specs=[pl.BlockSpec((1, tb, SHIFT_W), lambda bi, i: (bi, i, 0)),
                  pl.BlockSpec((1, RW_HEADS, RW_HEAD, RW_HEAD), lambda bi, i: (bi, 0, 0, 0)),
                  pl.BlockSpec((1, 1, SHIFT_W), lambda bi, i: (bi, 0, 0)),
                  vec(SHIFT_W), vec(RW_W), _const_spec((W_LORA + A_LORA, 2 * RW_W)), vec(RW_W),
                  _const_spec((G_LORA, RW_W)), vec(RW_W), vec(RW_W), vec(RW_W), vec(RW_W), vec(RW_W)],
        out_specs=[pl.BlockSpec((1, tb, RW_W), lambda bi, i: (bi, i, 0)),
                   pl.BlockSpec((1, RW_HEADS, RW_HEAD, RW_HEAD), lambda bi, i: (bi, 0, 0, 0))],
        out_shape=[jax.ShapeDtypeStruct((b, t, RW_W), BF16),
                   jax.ShapeDtypeStruct((b, RW_HEADS, RW_HEAD, RW_HEAD), F32)],
        scratch_shapes=[pltpu.VMEM((WKV_NG, WKV_GW, WKV_GW), F32), pltpu.VMEM((1, SHIFT_W), F32)],
        compiler_params=_params(("parallel", "arbitrary")),
        name="rwkv",
    )(z_rw, s0, shift0, p["mu"], p["w0"], p["wa"], p["a0"], p["g2"], p["k_k"], p["k_a"], p["r_k"],
      p["lnx_g"], p["lnx_b"])


def _merge_body(h_ref, oa_ref, orw_ref, gate_ref, woa_ref, worw_ref, wout_ref, postg_ref, pre_ref, win_ref,
                wo2_ref, post2_ref, y_ref):
    gates = gate_ref[...]
    merged = (gates[:, :D_MODEL] * _dot(oa_ref[...], woa_ref[...])
              + gates[:, D_MODEL:] * _dot(orw_ref[...], worw_ref[...]))
    h2 = h_ref[...] + _rms(_dot(merged.astype(BF16), wout_ref[...]), postg_ref[...])
    f = _swiglu_ffn(_rms(h2, pre_ref[...]).astype(BF16), win_ref, wo2_ref)
    y_ref[...] = h2 + MACARON * _rms(f, post2_ref[...])


def _merge_call(h, o_att, o_rw, gates, w_o_att, w_o_rwkv, w_out, mix_post_g, pre_g, w_in, w_out2, post_g, tm):
    m = h.shape[0]
    row = lambda w: pl.BlockSpec((tm, w), lambda i: (i, 0))
    vec = _const_spec((1, D_MODEL))
    return pl.pallas_call(
        _merge_body,
        grid=(m // tm,),
        in_specs=[row(D_MODEL), row(ATT_W), row(RW_W), row(GATE_W),
                  _const_spec((ATT_W, D_MODEL)), _const_spec((RW_W, D_MODEL)), _const_spec((D_MODEL, D_MODEL)), vec,
                  vec, _const_spec((D_MODEL, 2 * D_FF)), _const_spec((D_FF, D_MODEL)), vec],
        out_specs=row(D_MODEL),
        out_shape=jax.ShapeDtypeStruct((m, D_MODEL), F32),
        compiler_params=_params(("parallel",)),
        name="merge_ffn",
    )(h, o_att, o_rw, gates, w_o_att, w_o_rwkv, w_out, mix_post_g, pre_g, w_in, w_out2, post_g)


def _pick_tile(m, want):
    t = min(m, want)
    assert m % t == 0
    return t


def _layer(x, pos, l, p, cache_k, cache_v, s0, shift0):
    b, t, _ = x.shape
    m = b * t
    tm = _pick_tile(m, 256)
    lam_init = 0.8 - 0.6 * math.exp(-0.3 * l)

    h = _ffn_call(x.reshape(m, D_MODEL), p["ffn1_pre_g"], p["ffn1_w_in"], p["ffn1_w_out"], p["ffn1_post_g"], tm)

    cos_t, sin_t = _rope_tables(pos)
    if t < tm:
        cos_t, sin_t = jnp.tile(cos_t, (tm // t, 1)), jnp.tile(sin_t, (tm // t, 1))
    q, k, k_bf, v, v_bf, z_rw, gates = _mixproj_call(h, p["mix_pre_g"], p["w_in"], cos_t, sin_t, tm)

    lam_args = (p["att_lambda_q1"], p["att_lambda_k1"], p["att_lambda_q2"], p["att_lambda_k2"], p["att_subln_g"])
    r3 = lambda a: a.reshape(b, t, a.shape[-1])
    if cache_k is None:
        o_att = _attn_prompt_call(r3(q), r3(k_bf), r3(v_bf), *lam_args, lam_init, _pick_tile(t, 256))
    else:
        past = cache_k.shape[1]
        o_att = _attn_sample_call(r3(q), cache_k.reshape(b, past, ATT_W), cache_v.reshape(b, past, ATT_W),
                                  r3(k_bf), r3(v_bf), *lam_args, lam_init, _pick_tile(past, 1024))

    z_rw3 = r3(z_rw)
    t_pad = -(-t // WKV_C) * WKV_C
    z_in = z_rw3 if t_pad == t else jnp.pad(z_rw3, ((0, 0), (0, t_pad - t), (0, 0)))
    o_rw, s_new = _rwkv_call(z_in, s0, shift0, p["rwkv"], _pick_tile(t_pad, 256), None if t_pad == t else t)
    o_rw = o_rw[:, :t]

    y = _merge_call(h, o_att.reshape(m, ATT_W), o_rw.reshape(m, RW_W), gates, p["w_o_att"], p["w_o_rwkv"],
                    p["w_out"], p["mix_post_g"], p["ffn2_pre_g"], p["ffn2_w_in"], p["ffn2_w_out"],
                    p["ffn2_post_g"], tm)
    return (y.reshape(b, t, D_MODEL), k.reshape(b, t, ATT_HEADS, 2, ATT_DH), v.reshape(b, t, ATT_HEADS, 2 * ATT_DH),
            s_new, z_rw3[:, -1:])


def kernel(x_prompt, x_sample, cache_att_k, cache_att_v, state_rwkv, state_shift, ffn1_pre_g, ffn1_w_in, ffn1_w_out, ffn1_post_g, mix_pre_g, w_in, att_lambda_q1, att_lambda_k1, att_lambda_q2, att_lambda_k2, att_subln_g, rwkv_mu, rwkv_w0, rwkv_w2, rwkv_a0, rwkv_a2, rwkv_g2, rwkv_k_k, rwkv_k_a, rwkv_r_k, rwkv_lnx_g, rwkv_lnx_b, w_o_att, w_o_rwkv, w_out, mix_post_g, ffn2_pre_g, ffn2_w_in, ffn2_w_out, ffn2_post_g):
    depth = w_in.shape[0]
    bp, tp, _ = x_prompt.shape
    bs, ts, _ = x_sample.shape
    past = cache_att_k.shape[2]
    pos_p = jnp.arange(tp)
    pos_s = past + jnp.arange(ts)
    xp, xs = x_prompt, x_sample
    outs_p, outs_s = [], []
    vec = lambda a: a.reshape(1, -1)
    for l in range(depth):
        zeros = jnp.zeros((W_LORA, RW_W), F32)
        wa = jnp.concatenate([jnp.concatenate([rwkv_w2[l], zeros], axis=1),
                              jnp.concatenate([zeros, rwkv_a2[l]], axis=1)], axis=0)
        p = dict(
            ffn1_pre_g=vec(ffn1_pre_g[l]), ffn1_w_in=ffn1_w_in[l].astype(BF16), ffn1_w_out=ffn1_w_out[l].astype(BF16),
            ffn1_post_g=vec(ffn1_post_g[l]), mix_pre_g=vec(mix_pre_g[l]), w_in=w_in[l].astype(BF16),
            att_lambda_q1=vec(att_lambda_q1[l]), att_lambda_k1=vec(att_lambda_k1[l]),
            att_lambda_q2=vec(att_lambda_q2[l]), att_lambda_k2=vec(att_lambda_k2[l]),
            att_subln_g=vec(att_subln_g[l]),
            rwkv=dict(mu=vec(rwkv_mu[l]), w0=vec(rwkv_w0[l]), wa=wa.astype(BF16), a0=vec(rwkv_a0[l]),
                      g2=rwkv_g2[l].astype(BF16), k_k=vec(rwkv_k_k[l]), k_a=vec(rwkv_k_a[l]), r_k=vec(rwkv_r_k[l]),
                      lnx_g=vec(rwkv_lnx_g[l]), lnx_b=vec(rwkv_lnx_b[l])),
            w_o_att=w_o_att[l].astype(BF16), w_o_rwkv=w_o_rwkv[l].astype(BF16), w_out=w_out[l].astype(BF16),
            mix_post_g=vec(mix_post_g[l]), ffn2_pre_g=vec(ffn2_pre_g[l]), ffn2_w_in=ffn2_w_in[l].astype(BF16),
            ffn2_w_out=ffn2_w_out[l].astype(BF16), ffn2_post_g=vec(ffn2_post_g[l]),
        )
        s0p = jnp.zeros((bp, RW_HEADS, RW_HEAD, RW_HEAD), F32)
        sh0p = jnp.zeros((bp, 1, SHIFT_W), F32)
        xp, *rest_p = _layer(xp, pos_p, l, p, None, None, s0p, sh0p)
        xs, *rest_s = _layer(xs, pos_s, l, p, cache_att_k[l], cache_att_v[l], state_rwkv[l], state_shift[l])
        outs_p.append(rest_p)
        outs_s.append(rest_s)
    stack = lambda outs, i: jnp.stack([o[i] for o in outs], 0)
    return (xp, xs, stack(outs_p, 0), stack(outs_p, 1), stack(outs_p, 2), stack(outs_p, 3),
            stack(outs_s, 0), stack(outs_s, 1), stack(outs_s, 2), stack(outs_s, 3))
```

```python
import functools
import math

import jax
import jax.numpy as jnp
from jax import lax
from jax.experimental import pallas as pl
from jax.experimental.pallas import tpu as pltpu

F32 = jnp.float32
BF16 = jnp.bfloat16

D_MODEL = 1024
D_FF = 2816
CHUNK = 64
ROPE_THETA = 10000.0
EPS = 1e-6
MACARON = 0.5
ATT_HEADS = 4
ATT_DH = 64
ATT_W = ATT_HEADS * 2 * ATT_DH
RW_HEAD = 64
RW_W = D_MODEL // 2
RW_HEADS = RW_W // RW_HEAD
W_LORA = 64
A_LORA = 64
G_LORA = 128
SHIFT_W = 3 * RW_W + W_LORA + A_LORA + G_LORA
LNX_EPS = 64e-5
GATE_W = 2 * D_MODEL
IN_W = 3 * ATT_W + SHIFT_W + GATE_W

LANES = 128
MXU_W = 256
VMEM_LIMIT = 56 * 1024 * 1024

WKV_C = 64
WKV_GROUP = MXU_W // RW_HEAD
WKV_GW = WKV_GROUP * RW_HEAD
WKV_NG = RW_W // WKV_GW


def _params(sem):
    return pltpu.CompilerParams(dimension_semantics=sem, vmem_limit_bytes=VMEM_LIMIT)


def _const_spec(shape):
    nd = len(shape)
    return pl.BlockSpec(shape, lambda *_: (0,) * nd, pipeline_mode=pl.Buffered(1))


def _rms(x, g):
    return x * lax.rsqrt(jnp.mean(x * x, axis=-1, keepdims=True) + EPS) * g


def _dot(a, b):
    return jnp.dot(a, b, preferred_element_type=F32)


def _dot_nt(a, b):
    return lax.dot_general(a, b, (((1,), (1,)), ((), ())), preferred_element_type=F32)


def _dot_tn(a, b):
    return lax.dot_general(a, b, (((0,), (0,)), ((), ())), preferred_element_type=F32)


def _swiglu_ffn(xn_bf, w_in_ref, w_out_ref):
    hh = _dot(xn_bf, w_in_ref[...])
    gate = hh[:, :D_FF]
    up = hh[:, D_FF:]
    act = (gate * jax.nn.sigmoid(gate) * up).astype(BF16)
    return _dot(act, w_out_ref[...])


def _ffn_body(x_ref, pre_ref, win_ref, wout_ref, post_ref, o_ref):
    x = x_ref[...]
    f = _swiglu_ffn(_rms(x, pre_ref[...]).astype(BF16), win_ref, wout_ref)
    o_ref[...] = x + MACARON * _rms(f, post_ref[...])


def _ffn_call(x, pre_g, w_in, w_out, post_g, tm):
    m = x.shape[0]
    row = lambda w: pl.BlockSpec((tm, w), lambda i: (i, 0))
    return pl.pallas_call(
        _ffn_body,
        grid=(m // tm,),
        in_specs=[row(D_MODEL), _const_spec((1, D_MODEL)), _const_spec((D_MODEL, 2 * D_FF)),
                  _const_spec((D_FF, D_MODEL)), _const_spec((1, D_MODEL))],
        out_specs=row(D_MODEL),
        out_shape=jax.ShapeDtypeStruct((m, D_MODEL), F32),
        compiler_params=_params(("parallel",)),
        name="ffn",
    )(x, pre_g, w_in, w_out, post_g)


def _rope_tables(pos):
    half = ATT_DH // 2
    inv = ROPE_THETA ** (-jnp.arange(half, dtype=F32) / half)
    ang = pos.astype(F32)[:, None] * inv[None, :]
    cos, sin = jnp.cos(ang), jnp.sin(ang)
    cos_t = jnp.tile(jnp.concatenate([cos, cos], axis=-1), (1, ATT_W // ATT_DH))
    sin_t = jnp.tile(jnp.concatenate([-sin, sin], axis=-1), (1, ATT_W // ATT_DH))
    return cos_t, sin_t


def _rope(x, cos_t, sin_t):
    n = x.shape[-1]
    half = ATT_DH // 2
    lane = lax.broadcasted_iota(jnp.int32, x.shape, 1)
    swapped = jnp.where(lane % ATT_DH < half, pltpu.roll(x, n - half, 1), pltpu.roll(x, half, 1))
    return x * cos_t + swapped * sin_t


def _mixproj_body(v_cols, h_ref, g_ref, w_ref, cos_ref, sin_ref, q_ref, k_ref, kb_ref, v_ref, vb_ref, rw_ref,
                  gate_ref):
    u = _rms(h_ref[...], g_ref[...]).astype(BF16)
    z = _dot(u, w_ref[...])
    cos_t, sin_t = cos_ref[...], sin_ref[...]
    q_ref[...] = (_rope(z[:, :ATT_W], cos_t, sin_t) * (ATT_DH ** -0.5)).astype(BF16)
    k = _rope(z[:, ATT_W:2 * ATT_W], cos_t, sin_t)
    k_ref[...] = k
    kb_ref[...] = k.astype(BF16)
    v = z[:, 2 * ATT_W:3 * ATT_W]
    v_ref[...] = v
    if v_cols:
        vb_ref[0] = v.T.astype(BF16)
    else:
        vb_ref[...] = v.astype(BF16)
    rw_ref[...] = z[:, 3 * ATT_W:3 * ATT_W + SHIFT_W]
    gate_ref[...] = jax.nn.sigmoid(z[:, 3 * ATT_W + SHIFT_W:]).astype(BF16)


def _mixproj_call(h, mix_pre_g, w_in, cos_t, sin_t, tm, v_cols_t):
    m = h.shape[0]
    ntab = cos_t.shape[0] // tm
    row = lambda w: pl.BlockSpec((tm, w), lambda i: (i, 0))
    tab = pl.BlockSpec((tm, ATT_W), lambda i: (i % ntab, 0))
    sds = lambda w, dt: jax.ShapeDtypeStruct((m, w), dt)
    if v_cols_t is None:
        vb_spec, vb_shape = row(ATT_W), sds(ATT_W, BF16)
    else:
        nt = v_cols_t // tm
        vb_spec = pl.BlockSpec((1, ATT_W, tm), lambda i: (i // nt, 0, i % nt))
        vb_shape = jax.ShapeDtypeStruct((m // v_cols_t, ATT_W, v_cols_t), BF16)
    return pl.pallas_call(
        functools.partial(_mixproj_body, v_cols_t is not None),
        grid=(m // tm,),
        in_specs=[row(D_MODEL), _const_spec((1, D_MODEL)), _const_spec((D_MODEL, IN_W)), tab, tab],
        out_specs=[row(ATT_W)] * 4 + [vb_spec, row(SHIFT_W), row(GATE_W)],
        out_shape=[sds(ATT_W, BF16), sds(ATT_W, F32), sds(ATT_W, BF16), sds(ATT_W, F32), vb_shape,
                   sds(SHIFT_W, F32), sds(GATE_W, BF16)],
        compiler_params=_params(("parallel",)),
        name="mixproj",
    )(h, mix_pre_g, w_in, cos_t, sin_t)


def _lambda(lq1_ref, lk1_ref, lq2_ref, lk2_ref, lam_init):
    s1 = jnp.sum(lq1_ref[...] * lk1_ref[...], axis=-1, keepdims=True)
    s2 = jnp.sum(lq2_ref[...] * lk2_ref[...], axis=-1, keepdims=True)
    return jnp.exp(s1) - jnp.exp(s2) + lam_init


def _subln(o, g, lam_init):
    return o * lax.rsqrt(jnp.mean(o * o, axis=-1, keepdims=True) + EPS) * g * (1.0 - lam_init)


def _softmax_step(s, v_bf, m_ref, l_ref, acc_ref):
    m_prev = m_ref[...]
    m_new = jnp.maximum(m_prev, jnp.max(s, axis=-1, keepdims=True))
    alpha = jnp.exp(m_prev - m_new)
    p = jnp.exp(s - m_new)
    l_ref[...] = alpha * l_ref[...] + jnp.sum(p, axis=-1, keepdims=True)
    acc_ref[...] = alpha * acc_ref[...] + _dot(p.astype(BF16), v_bf)
    m_ref[...] = m_new


def _attn_prompt_body(lam_init, blk, q_ref, k_ref, vt_ref, lq1_ref, lk1_ref, lq2_ref, lk2_ref, g_ref, o_ref,
                      qq_ref, m_ref, l_ref, acc_ref):
    i = pl.program_id(2)
    q = q_ref[0]
    lane = lax.broadcasted_iota(jnp.int32, q.shape, 1)
    zero = jnp.zeros_like(q)
    qq_ref[:blk, :] = jnp.where(lane < ATT_DH, q, zero)
    qq_ref[blk:, :] = jnp.where(lane >= ATT_DH, q, zero)
    m_ref[...] = jnp.full(m_ref.shape, -jnp.inf, F32)
    l_ref[...] = jnp.zeros(l_ref.shape, F32)
    acc_ref[...] = jnp.zeros(acc_ref.shape, F32)

    def step(start, diagonal):
        st = _dot_nt(k_ref[0, pl.ds(start, blk), :], qq_ref[...])
        if diagonal:
            key = lax.broadcasted_iota(jnp.int32, st.shape, 0)
            qry = lax.broadcasted_iota(jnp.int32, st.shape, 1) % blk
            st = jnp.where(key // CHUNK <= qry // CHUNK, st, -jnp.inf)
        m_prev = m_ref[...]
        m_new = jnp.maximum(m_prev, jnp.max(st, axis=0, keepdims=True))
        alpha = jnp.exp(m_prev - m_new)
        p = jnp.exp(st - m_new)
        l_ref[...] = alpha * l_ref[...] + jnp.sum(p, axis=0, keepdims=True)
        acc_ref[...] = alpha * acc_ref[...] + _dot(vt_ref[0, :, pl.ds(start, blk)], p.astype(BF16))
        m_ref[...] = m_new

    def full_block(j, carry):
        step(pl.multiple_of(j * blk, blk), False)
        return carry

    lax.fori_loop(0, i, full_block, 0)
    step(pl.multiple_of(i * blk, blk), True)

    lam = _lambda(lq1_ref, lk1_ref, lq2_ref, lk2_ref, lam_init)
    o = acc_ref[...] / l_ref[...]
    o = o[:, :blk] - lam * o[:, blk:]
    o = o * lax.rsqrt(jnp.mean(o * o, axis=0, keepdims=True) + EPS) * g_ref[...] * (1.0 - lam_init)
    o_ref[0] = o.T.astype(o_ref.dtype)


def _attn_prompt_call(q, k, vt, lq1, lk1, lq2, lk2, subln_g, lam_init, blk):
    b, t, _ = q.shape
    hw = 2 * ATT_DH
    qspec = pl.BlockSpec((1, blk, hw), lambda bi, hi, i: (bi, i, hi))
    kspec = pl.BlockSpec((1, t, hw), lambda bi, hi, i: (bi, 0, hi))
    vtspec = pl.BlockSpec((1, hw, t), lambda bi, hi, i: (bi, hi, 0))
    return pl.pallas_call(
        functools.partial(_attn_prompt_body, lam_init, blk),
        grid=(b, ATT_HEADS, t // blk),
        in_specs=[qspec, kspec, vtspec] + [_const_spec((1, ATT_DH))] * 4 + [_const_spec((hw, 1))],
        out_specs=qspec,
        out_shape=jax.ShapeDtypeStruct((b, t, ATT_W), BF16),
        scratch_shapes=[pltpu.VMEM((2 * blk, hw), BF16), pltpu.VMEM((1, 2 * blk), F32),
                        pltpu.VMEM((1, 2 * blk), F32), pltpu.VMEM((hw, 2 * blk), F32)],
        compiler_params=_params(("parallel", "parallel", "arbitrary")),
        name="attn_prompt",
    )(q, k, vt, lq1, lk1, lq2, lk2, subln_g.reshape(hw, 1))


def _attn_sample_body(lam_init, ts, q_ref, ck_ref, cv_ref, kn_ref, vn_ref, lq1_ref, lk1_ref, lq2_ref, lk2_ref,
                      g_ref, o_ref, qq_ref, m_ref, l_ref, acc_ref):
    j = pl.program_id(1)
    nmaps = 2 * ATT_HEADS
    hw = 2 * ATT_DH

    @pl.when(j == 0)
    def _():
        q = jnp.concatenate([q_ref[0]] * nmaps, axis=0)
        row = lax.broadcasted_iota(jnp.int32, q.shape, 0)
        lane = lax.broadcasted_iota(jnp.int32, q.shape, 1)
        qq_ref[...] = jnp.where(row // ts == lane // ATT_DH, q, jnp.zeros_like(q))
        m_ref[...] = jnp.full(m_ref.shape, -jnp.inf, F32)
        l_ref[...] = jnp.zeros(l_ref.shape, F32)
        acc_ref[...] = jnp.zeros(acc_ref.shape, F32)

    def step(k_bf, v_bf):
        _softmax_step(_dot_nt(qq_ref[...], k_bf), v_bf, m_ref, l_ref, acc_ref)

    step(ck_ref[0].astype(BF16), cv_ref[0].astype(BF16))

    @pl.when(j == pl.num_programs(1) - 1)
    def _():
        step(kn_ref[0], vn_ref[0])
        lam = _lambda(lq1_ref, lk1_ref, lq2_ref, lk2_ref, lam_init)
        o = acc_ref[...] / l_ref[...]
        for h in range(ATT_HEADS):
            r0 = 2 * h * ts
            oh = o[r0:r0 + ts, h * hw:(h + 1) * hw] - lam * o[r0 + ts:r0 + 2 * ts, h * hw:(h + 1) * hw]
            o_ref[0, :, h * hw:(h + 1) * hw] = _subln(oh, g_ref[...], lam_init).astype(o_ref.dtype)


def _attn_sample_call(q, cache_k, cache_v, k_new, v_new, lq1, lk1, lq2, lk2, subln_g, lam_init, tk):
    b, ts, _ = q.shape
    past = cache_k.shape[1]
    rows = 2 * ATT_HEADS * ts
    new = pl.BlockSpec((1, ts, ATT_W), lambda bi, j: (bi, 0, 0))
    cache = pl.BlockSpec((1, tk, ATT_W), lambda bi, j: (bi, j, 0))
    return pl.pallas_call(
        functools.partial(_attn_sample_body, lam_init, ts),
        grid=(b, past // tk),
        in_specs=[new, cache, cache, new, new] + [_const_spec((1, ATT_DH))] * 4 + [_const_spec((1, 2 * ATT_DH))],
        out_specs=new,
        out_shape=jax.ShapeDtypeStruct((b, ts, ATT_W), BF16),
        scratch_shapes=[pltpu.VMEM((rows, ATT_W), BF16), pltpu.VMEM((rows, 1), F32), pltpu.VMEM((rows, 1), F32),
                        pltpu.VMEM((rows, ATT_W), F32)],
        compiler_params=_params(("parallel", "arbitrary")),
        name="attn_sample",
    )(q, cache_k, cache_v, k_new, v_new, lq1, lk1, lq2, lk2, subln_g)


def _split2(x):
    hi = x.astype(BF16)
    return hi, (x - hi.astype(F32)).astype(BF16)


def _split3(x):
    hi = x.astype(BF16)
    r1 = x - hi.astype(F32)
    mid = r1.astype(BF16)
    return hi, mid, (r1 - mid.astype(F32)).astype(BF16)


def _head_sum(x, ones_bd):
    hi, lo = _split2(x)
    return _dot(hi, ones_bd) + _dot(lo, ones_bd)


def _bd_expand(x):
    c = x.shape[0]
    xt = jnp.concatenate([x] * WKV_GROUP, axis=0)
    row = lax.broadcasted_iota(jnp.int32, xt.shape, 0)
    lane = lax.broadcasted_iota(jnp.int32, xt.shape, 1)
    return jnp.where(row // c == lane // RW_HEAD, xt, jnp.zeros_like(xt))


def _wkv_chunk_terms(at, rt, bt, kt, bh, kh, v):
    c = at.shape[0]
    slab = (c, WKV_GROUP * c)
    t_idx = lax.broadcasted_iota(jnp.int32, slab, 0)
    s_idx = lax.broadcasted_iota(jnp.int32, slab, 1) % c
    strict = s_idx < t_idx
    incl = s_idx <= t_idx

    ar = jnp.concatenate([at, rt], axis=0).astype(BF16)
    sb = _dot_nt(ar, _bd_expand(bt).astype(BF16))
    sk = _dot_nt(ar, _bd_expand(kt).astype(BF16))
    l_ab = jnp.where(strict, sb[:c], 0.0)
    l_ak = jnp.where(strict, sk[:c], 0.0)
    m_rb = jnp.where(incl, sb[c:], 0.0)
    m_rk = jnp.where(incl, sk[c:], 0.0)

    tinv = jnp.where(s_idx == t_idx, 1.0, 0.0) + l_ab
    lp = l_ab
    power = 1
    while 2 * power < c:
        lp = _dot(lp.astype(BF16), _bd_expand(lp).astype(BF16))
        tinv = tinv + _dot(tinv.astype(BF16), _bd_expand(lp).astype(BF16))
        power *= 2

    v_exp = _bd_expand(v).astype(BF16)
    tinv_bf = tinv.astype(BF16)
    ta = _dot(tinv_bf, _bd_expand(at).astype(BF16))
    uv = _dot(tinv_bf, _bd_expand(_dot(l_ak.astype(BF16), v_exp)).astype(BF16))

    gw = at.shape[1]
    bd = (lax.broadcasted_iota(jnp.int32, (gw, gw), 0) // RW_HEAD
          == lax.broadcasted_iota(jnp.int32, (gw, gw), 1) // RW_HEAD)
    p_lr = jnp.where(bd, _dot_tn(bh.astype(BF16), ta.astype(BF16)), 0.0)
    q_t = jnp.where(bd, _dot_tn(jnp.concatenate([uv, v], axis=0).astype(BF16),
                                jnp.concatenate([bh, kh], axis=0).astype(BF16)), 0.0)
    m_rb_bf = m_rb.astype(BF16)
    ya = rt + _dot(m_rb_bf, _bd_expand(ta).astype(BF16))
    yb = _dot(m_rb_bf, _bd_expand(uv).astype(BF16)) + _dot(m_rk.astype(BF16), v_exp)
    return p_lr, q_t, ya, yb


def _rwkv_body(tb, t_valid, z_ref, s0_ref, sh0_ref, mu_ref, w0_ref, wa_ref, a0_ref, g2_ref, kk_ref, ka_ref,
               rk_ref, lng_ref, lnb_ref, o_ref, s_ref, h_ref, carry_ref):
    ib = pl.program_id(1)
    c = WKV_C
    nc = tb // c

    @pl.when(ib == 0)
    def _():
        carry_ref[...] = sh0_ref[0]
        for g in range(WKV_NG):
            blocks = [s0_ref[0, g * WKV_GROUP + h] for h in range(WKV_GROUP)]
            h_ref[g] = _bd_expand(jnp.concatenate(blocks, axis=1))

    z = z_ref[0]
    row = lax.broadcasted_iota(jnp.int32, z.shape, 0)
    prev = jnp.where(row == 0, carry_ref[...], pltpu.roll(z, 1, 0))
    carry_ref[...] = z[tb - 1:tb, :]
    zs = z + (prev - z) * mu_ref[...]

    r = zs[:, :RW_W]
    k = zs[:, RW_W:2 * RW_W]
    v = zs[:, 2 * RW_W:3 * RW_W]
    zwa = zs[:, 3 * RW_W:3 * RW_W + W_LORA + A_LORA]
    zg = zs[:, 3 * RW_W + W_LORA + A_LORA:]
    lane = lax.broadcasted_iota(jnp.int32, zwa.shape, 1)
    lora = _dot(jnp.where(lane < W_LORA, jnp.tanh(zwa), zwa).astype(BF16), wa_ref[...])
    ew = math.exp(-0.5) * jax.nn.sigmoid(w0_ref[...] + lora[:, :RW_W])
    a = jax.nn.sigmoid(a0_ref[...] + lora[:, RW_W:])
    gate = _dot(jax.nn.sigmoid(zg).astype(BF16), g2_ref[...])

    ones_bd = (lax.broadcasted_iota(jnp.int32, (RW_W, RW_W), 0) // RW_HEAD
               == lax.broadcasted_iota(jnp.int32, (RW_W, RW_W), 1) // RW_HEAD).astype(BF16)
    kk = k * kk_ref[...]
    kk = kk / jnp.maximum(jnp.sqrt(_head_sum(kk * kk, ones_bd)), 1e-12)
    k = k * (1.0 + (a - 1.0) * ka_ref[...])
    bonus = _head_sum(r * k * rk_ref[...], ones_bd) * v

    if t_valid is not None:
        live = lax.broadcasted_iota(jnp.int32, ew.shape, 0) + ib * tb < t_valid
        ew = jnp.where(live, ew, 0.0)
        k = jnp.where(live, k, 0.0)
        kk = jnp.where(live, kk, 0.0)
        v = jnp.where(live, v, 0.0)

    ti = lax.broadcasted_iota(jnp.int32, (tb, tb), 0)
    si = lax.broadcasted_iota(jnp.int32, (tb, tb), 1)
    tri = jnp.logical_and(si <= ti, si // c == ti // c).astype(BF16)
    e1, e2, e3 = _split3(ew)
    cum = -(_dot(tri, e1) + _dot(tri, e2) + _dot(tri, e3))

    ys = []
    for ci in range(nc):
        sl = slice(ci * c, (ci + 1) * c)
        cum_c = cum[sl]
        cum_end = cum_c[c - 1:c, :]
        grow = jnp.exp(-cum_c)
        to_end = jnp.exp(cum_end - cum_c)
        b = kk[sl] * a[sl]
        at = -kk[sl] * jnp.exp(cum_c + ew[sl])
        rt = r[sl] * jnp.exp(cum_c)
        gamma_end = jnp.exp(cum_end)
        y_groups = []
        for g in range(WKV_NG):
            gl = slice(g * WKV_GW, (g + 1) * WKV_GW)
            p_lr, q_t, ya, yb = _wkv_chunk_terms(at[:, gl], rt[:, gl], (b * grow)[:, gl], (k[sl] * grow)[:, gl],
                                               (b * to_end)[:, gl], (k[sl] * to_end)[:, gl], v[sl][:, gl])
            s = h_ref[g]
            s_bf = s.astype(BF16)
            y_groups.append(_dot_nt(ya.astype(BF16), s_bf) + yb)
            h_ref[g] = s * gamma_end[:, gl] + _dot_nt(s_bf, p_lr.astype(BF16)) + q_t
        ys.append(jnp.concatenate(y_groups, axis=1))
    y = jnp.concatenate(ys, axis=0)

    mean = _head_sum(y, ones_bd) * (1.0 / RW_HEAD)
    d = y - mean
    var = _head_sum(d * d, ones_bd) * (1.0 / RW_HEAD)
    yn = d * lax.rsqrt(var + LNX_EPS) * lng_ref[...] + lnb_ref[...]
    o_ref[0] = ((yn + bonus) * gate).astype(o_ref.dtype)

    @pl.when(ib == pl.num_programs(1) - 1)
    def _():
        for g in range(WKV_NG):
            h = h_ref[g]
            for hh in range(WKV_GROUP):
                blk = h[hh * RW_HEAD:(hh + 1) * RW_HEAD, hh * RW_HEAD:(hh + 1) * RW_HEAD]
                s_ref[0, g * WKV_GROUP + hh] = blk


def _rwkv_call(z_rw, s0, shift0, p, tb, t_valid):
    b, t, _ = z_rw.shape
    vec = lambda w: _const_spec((1, w))
    return pl.pallas_call(
        functools.partial(_rwkv_body, tb, t_valid),
        grid=(b, t // tb),
        in_specs=[pl.BlockSpec((1, tb, SHIFT_W), lambda bi, i: (bi, i, 0)),
                  pl.BlockSpec((1, RW_HEADS, RW_HEAD, RW_HEAD), lambda bi, i: (bi, 0, 0, 0)),
                  pl.BlockSpec((1, 1, SHIFT_W), lambda bi, i: (bi, 0, 0)),
                  vec(SHIFT_W), vec(RW_W), _const_spec((W_LORA + A_LORA, 2 * RW_W)), vec(RW_W),
                  _const_spec((G_LORA, RW_W)), vec(RW_W), vec(RW_W), vec(RW_W), vec(RW_W), vec(RW_W)],
        out_specs=[pl.BlockSpec((1, tb, RW_W), lambda bi, i: (bi, i, 0)),
                   pl.BlockSpec((1, RW_HEADS, RW_HEAD, RW_HEAD), lambda bi, i: (bi, 0, 0, 0))],
        out_shape=[jax.ShapeDtypeStruct((b, t, RW_W), BF16),
                   jax.ShapeDtypeStruct((b, RW_HEADS, RW_HEAD, RW_HEAD), F32)],
        scratch_shapes=[pltpu.VMEM((WKV_NG, WKV_GW, WKV_GW), F32), pltpu.VMEM((1, SHIFT_W), F32)],
        compiler_params=_params(("parallel", "arbitrary")),
        name="rwkv",
    )(z_rw, s0, shift0, p["mu"], p["w0"], p["wa"], p["a0"], p["g2"], p["k_k"], p["k_a"], p["r_k"],
      p["lnx_g"], p["lnx_b"])


def _merge_body(h_ref, oa_ref, orw_ref, gate_ref, woa_ref, worw_ref, wout_ref, postg_ref, pre_ref, win_ref,
                wo2_ref, post2_ref, y_ref):
    gates = gate_ref[...]
    merged = (gates[:, :D_MODEL] * _dot(oa_ref[...], woa_ref[...])
              + gates[:, D_MODEL:] * _dot(orw_ref[...], worw_ref[...]))
    h2 = h_ref[...] + _rms(_dot(merged.astype(BF16), wout_ref[...]), postg_ref[...])
    f = _swiglu_ffn(_rms(h2, pre_ref[...]).astype(BF16), win_ref, wo2_ref)
    y_ref[...] = h2 + MACARON * _rms(f, post2_ref[...])


def _merge_call(h, o_att, o_rw, gates, w_o_att, w_o_rwkv, w_out, mix_post_g, pre_g, w_in, w_out2, post_g, tm):
    m = h.shape[0]
    row = lambda w: pl.BlockSpec((tm, w), lambda i: (i, 0))
    vec = _const_spec((1, D_MODEL))
    return pl.pallas_call(
        _merge_body,
        grid=(m // tm,),
        in_specs=[row(D_MODEL), row(ATT_W), row(RW_W), row(GATE_W),
                  _const_spec((ATT_W, D_MODEL)), _const_spec((RW_W, D_MODEL)), _const_spec((D_MODEL, D_MODEL)), vec,
                  vec, _const_spec((D_MODEL, 2 * D_FF)), _const_spec((D_FF, D_MODEL)), vec],
        out_specs=row(D_MODEL),
        out_shape=jax.ShapeDtypeStruct((m, D_MODEL), F32),
        compiler_params=_params(("parallel",)),
        name="merge_ffn",
    )(h, o_att, o_rw, gates, w_o_att, w_o_rwkv, w_out, mix_post_g, pre_g, w_in, w_out2, post_g)


def _pick_tile(m, want):
    t = min(m, want)
    assert m % t == 0
    return t


def _layer(x, pos, l, p, cache_k, cache_v, s0, shift0):
    b, t, _ = x.shape
    m = b * t
    tm = _pick_tile(m, 256)
    lam_init = 0.8 - 0.6 * math.exp(-0.3 * l)

    h = _ffn_call(x.reshape(m, D_MODEL), p["ffn1_pre_g"], p["ffn1_w_in"], p["ffn1_w_out"], p["ffn1_post_g"], tm)

    cos_t, sin_t = _rope_tables(pos)
    if t < tm:
        cos_t, sin_t = jnp.tile(cos_t, (tm // t, 1)), jnp.tile(sin_t, (tm // t, 1))
    q, k, k_bf, v, v_bf, z_rw, gates = _mixproj_call(h, p["mix_pre_g"], p["w_in"], cos_t, sin_t, tm,
                                                     t if cache_k is None else None)

    lam_args = (p["att_lambda_q1"], p["att_lambda_k1"], p["att_lambda_q2"], p["att_lambda_k2"], p["att_subln_g"])
    r3 = lambda a: a.reshape(b, t, a.shape[-1])
    if cache_k is None:
        o_att = _attn_prompt_call(r3(q), r3(k_bf), v_bf, *lam_args, lam_init, _pick_tile(t, 512))
    else:
        past = cache_k.shape[1]
        o_att = _attn_sample_call(r3(q), cache_k.reshape(b, past, ATT_W), cache_v.reshape(b, past, ATT_W),
                                  r3(k_bf), r3(v_bf), *lam_args, lam_init, _pick_tile(past, 1024))

    z_rw3 = r3(z_rw)
    t_pad = -(-t // WKV_C) * WKV_C
    z_in = z_rw3 if t_pad == t else jnp.pad(z_rw3, ((0, 0), (0, t_pad - t), (0, 0)))
    o_rw, s_new = _rwkv_call(z_in, s0, shift0, p["rwkv"], _pick_tile(t_pad, 256), None if t_pad == t else t)
    o_rw = o_rw[:, :t]

    y = _merge_call(h, o_att.reshape(m, ATT_W), o_rw.reshape(m, RW_W), gates, p["w_o_att"], p["w_o_rwkv"],
                    p["w_out"], p["mix_post_g"], p["ffn2_pre_g"], p["ffn2_w_in"], p["ffn2_w_out"],
                    p["ffn2_post_g"], tm)
    return (y.reshape(b, t, D_MODEL), k.reshape(b, t, ATT_HEADS, 2, ATT_DH), v.reshape(b, t, ATT_HEADS, 2 * ATT_DH),
            s_new, z_rw3[:, -1:])


def kernel(x_prompt, x_sample, cache_att_k, cache_att_v, state_rwkv, state_shift, ffn1_pre_g, ffn1_w_in, ffn1_w_out, ffn1_post_g, mix_pre_g, w_in, att_lambda_q1, att_lambda_k1, att_lambda_q2, att_lambda_k2, att_subln_g, rwkv_mu, rwkv_w0, rwkv_w2, rwkv_a0, rwkv_a2, rwkv_g2, rwkv_k_k, rwkv_k_a, rwkv_r_k, rwkv_lnx_g, rwkv_lnx_b, w_o_att, w_o_rwkv, w_out, mix_post_g, ffn2_pre_g, ffn2_w_in, ffn2_w_out, ffn2_post_g):
    depth = w_in.shape[0]
    bp, tp, _ = x_prompt.shape
    bs, ts, _ = x_sample.shape
    past = cache_att_k.shape[2]
    pos_p = jnp.arange(tp)
    pos_s = past + jnp.arange(ts)
    xp, xs = x_prompt, x_sample
    outs_p, outs_s = [], []
    vec = lambda a: a.reshape(1, -1)
    for l in range(depth):
        zeros = jnp.zeros((W_LORA, RW_W), F32)
        wa = jnp.concatenate([jnp.concatenate([rwkv_w2[l], zeros], axis=1),
                              jnp.concatenate([zeros, rwkv_a2[l]], axis=1)], axis=0)
        p = dict(
            ffn1_pre_g=vec(ffn1_pre_g[l]), ffn1_w_in=ffn1_w_in[l].astype(BF16), ffn1_w_out=ffn1_w_out[l].astype(BF16),
            ffn1_post_g=vec(ffn1_post_g[l]), mix_pre_g=vec(mix_pre_g[l]), w_in=w_in[l].astype(BF16),
            att_lambda_q1=vec(att_lambda_q1[l]), att_lambda_k1=vec(att_lambda_k1[l]),
            att_lambda_q2=vec(att_lambda_q2[l]), att_lambda_k2=vec(att_lambda_k2[l]),
            att_subln_g=vec(att_subln_g[l]),
            rwkv=dict(mu=vec(rwkv_mu[l]), w0=vec(rwkv_w0[l]), wa=wa.astype(BF16), a0=vec(rwkv_a0[l]),
                      g2=rwkv_g2[l].astype(BF16), k_k=vec(rwkv_k_k[l]), k_a=vec(rwkv_k_a[l]), r_k=vec(rwkv_r_k[l]),
                      lnx_g=vec(rwkv_lnx_g[l]), lnx_b=vec(rwkv_lnx_b[l])),
            w_o_att=w_o_att[l].astype(BF16), w_o_rwkv=w_o_rwkv[l].astype(BF16), w_out=w_out[l].astype(BF16),
            mix_post_g=vec(mix_post_g[l]), ffn2_pre_g=vec(ffn2_pre_g[l]), ffn2_w_in=ffn2_w_in[l].astype(BF16),
            ffn2_w_out=ffn2_w_out[l].astype(BF16), ffn2_post_g=vec(ffn2_post_g[l]),
        )
        s0p = jnp.zeros((bp, RW_HEADS, RW_HEAD, RW_HEAD), F32)
        sh0p = jnp.zeros((bp, 1, SHIFT_W), F32)
        xp, *rest_p = _layer(xp, pos_p, l, p, None, None, s0p, sh0p)
        xs, *rest_s = _layer(xs, pos_s, l, p, cache_att_k[l], cache_att_v[l], state_rwkv[l], state_shift[l])
        outs_p.append(rest_p)
        outs_s.append(rest_s)
    stack = lambda outs, i: jnp.stack([o[i] for o in outs], 0)
    return (xp, xs, stack(outs_p, 0), stack(outs_p, 1), stack(outs_p, 2), stack(outs_p, 3),
            stack(outs_s, 0), stack(outs_s, 1), stack(outs_s, 2), stack(outs_s, 3))
```

```python
import functools
import math

import jax
import jax.numpy as jnp
from jax import lax
from jax.experimental import pallas as pl
from jax.experimental.pallas import tpu as pltpu

F32 = jnp.float32
BF16 = jnp.bfloat16

D_MODEL = 1024
D_FF = 2816
CHUNK = 64
ROPE_THETA = 10000.0
EPS = 1e-6
MACARON = 0.5
ATT_HEADS = 4
ATT_DH = 64
ATT_W = ATT_HEADS * 2 * ATT_DH
ATT_HPS = 2
Q_SCALE = ATT_DH ** -0.5 * math.log2(math.e)
RW_HEAD = 64
RW_W = D_MODEL // 2
RW_HEADS = RW_W // RW_HEAD
W_LORA = 64
A_LORA = 64
G_LORA = 128
SHIFT_W = 3 * RW_W + W_LORA + A_LORA + G_LORA
LNX_EPS = 64e-5
GATE_W = 2 * D_MODEL
IN_W = 3 * ATT_W + SHIFT_W + GATE_W

LANES = 128
BF16_ROWS = 16
MXU_W = 256
VMEM_LIMIT = 56 * 1024 * 1024

WKV_C = 64
WKV_GROUP = MXU_W // RW_HEAD
WKV_GW = WKV_GROUP * RW_HEAD
WKV_NG = RW_W // WKV_GW
WKV_SEQS = 4
WKV_TB = 128


def _params(sem):
    return pltpu.CompilerParams(dimension_semantics=sem, vmem_limit_bytes=VMEM_LIMIT)


def _const_spec(shape):
    nd = len(shape)
    return pl.BlockSpec(shape, lambda *_: (0,) * nd, pipeline_mode=pl.Buffered(1))


def _rms(x, g):
    return x * lax.rsqrt(jnp.mean(x * x, axis=-1, keepdims=True) + EPS) * g


def _dot(a, b):
    return jnp.dot(a, b, preferred_element_type=F32)


def _dot_nt(a, b):
    return lax.dot_general(a, b, (((1,), (1,)), ((), ())), preferred_element_type=F32)


def _dot_tn(a, b):
    return lax.dot_general(a, b, (((0,), (0,)), ((), ())), preferred_element_type=F32)


def _swiglu_ffn(xn_bf, w_in_ref, w_out_ref):
    hh = _dot(xn_bf, w_in_ref[...])
    gate = hh[:, :D_FF]
    up = hh[:, D_FF:]
    act = (gate * jax.nn.sigmoid(gate) * up).astype(BF16)
    return _dot(act, w_out_ref[...])


def _ffn_body(x_ref, pre_ref, win_ref, wout_ref, post_ref, o_ref):
    x = x_ref[...]
    f = _swiglu_ffn(_rms(x, pre_ref[...]).astype(BF16), win_ref, wout_ref)
    o_ref[...] = x + MACARON * _rms(f, post_ref[...])


def _ffn_call(x, pre_g, w_in, w_out, post_g, tm):
    m = x.shape[0]
    row = lambda w: pl.BlockSpec((tm, w), lambda i: (i, 0))
    return pl.pallas_call(
        _ffn_body,
        grid=(m // tm,),
        in_specs=[row(D_MODEL), _const_spec((1, D_MODEL)), _const_spec((D_MODEL, 2 * D_FF)),
                  _const_spec((D_FF, D_MODEL)), _const_spec((1, D_MODEL))],
        out_specs=row(D_MODEL),
        out_shape=jax.ShapeDtypeStruct((m, D_MODEL), F32),
        compiler_params=_params(("parallel",)),
        name="ffn",
    )(x, pre_g, w_in, w_out, post_g)


def _rope_tables(pos):
    half = ATT_DH // 2
    inv = ROPE_THETA ** (-jnp.arange(half, dtype=F32) / half)
    ang = pos.astype(F32)[:, None] * inv[None, :]
    cos, sin = jnp.cos(ang), jnp.sin(ang)
    cos_t = jnp.tile(jnp.concatenate([cos, cos], axis=-1), (1, ATT_W // ATT_DH))
    sin_t = jnp.tile(jnp.concatenate([-sin, sin], axis=-1), (1, ATT_W // ATT_DH))
    return cos_t, sin_t


def _rope(x, cos_t, sin_t):
    n = x.shape[-1]
    half = ATT_DH // 2
    lane = lax.broadcasted_iota(jnp.int32, x.shape, 1)
    swapped = jnp.where(lane % ATT_DH < half, pltpu.roll(x, n - half, 1), pltpu.roll(x, half, 1))
    return x * cos_t + swapped * sin_t


def _mixproj_body(v_cols, h_ref, g_ref, w_ref, cos_ref, sin_ref, q_ref, k_ref, kb_ref, v_ref, vb_ref, rw_ref,
                  gate_ref):
    u = _rms(h_ref[...], g_ref[...]).astype(BF16)
    z = _dot(u, w_ref[...])
    cos_t, sin_t = cos_ref[...], sin_ref[...]
    q_ref[...] = (_rope(z[:, :ATT_W], cos_t, sin_t) * Q_SCALE).astype(BF16)
    k = _rope(z[:, ATT_W:2 * ATT_W], cos_t, sin_t)
    k_ref[...] = k
    kb_ref[...] = k.astype(BF16)
    v = z[:, 2 * ATT_W:3 * ATT_W]
    v_ref[...] = v
    if v_cols:
        vb_ref[0] = v.T.astype(BF16)
    else:
        vb_ref[...] = v.astype(BF16)
    rw_ref[...] = z[:, 3 * ATT_W:3 * ATT_W + SHIFT_W]
    gate_ref[...] = jax.nn.sigmoid(z[:, 3 * ATT_W + SHIFT_W:]).astype(BF16)


def _mixproj_call(h, mix_pre_g, w_in, cos_t, sin_t, tm, v_cols_t):
    m = h.shape[0]
    ntab = cos_t.shape[0] // tm
    row = lambda w: pl.BlockSpec((tm, w), lambda i: (i, 0))
    tab = pl.BlockSpec((tm, ATT_W), lambda i: (i % ntab, 0))
    sds = lambda w, dt: jax.ShapeDtypeStruct((m, w), dt)
    if v_cols_t is None:
        vb_spec, vb_shape = row(ATT_W), sds(ATT_W, BF16)
    else:
        nt = v_cols_t // tm
        vb_spec = pl.BlockSpec((1, ATT_W, tm), lambda i: (i // nt, 0, i % nt))
        vb_shape = jax.ShapeDtypeStruct((m // v_cols_t, ATT_W, v_cols_t), BF16)
    return pl.pallas_call(
        functools.partial(_mixproj_body, v_cols_t is not None),
        grid=(m // tm,),
        in_specs=[row(D_MODEL), _const_spec((1, D_MODEL)), _const_spec((D_MODEL, IN_W)), tab, tab],
        out_specs=[row(ATT_W)] * 4 + [vb_spec, row(SHIFT_W), row(GATE_W)],
        out_shape=[sds(ATT_W, BF16), sds(ATT_W, F32), sds(ATT_W, BF16), sds(ATT_W, F32), vb_shape,
                   sds(SHIFT_W, F32), sds(GATE_W, BF16)],
        compiler_params=_params(("parallel",)),
        name="mixproj",
    )(h, mix_pre_g, w_in, cos_t, sin_t)


def _lambda(lq1_ref, lk1_ref, lq2_ref, lk2_ref, lam_init):
    s1 = jnp.sum(lq1_ref[...] * lk1_ref[...], axis=-1, keepdims=True)
    s2 = jnp.sum(lq2_ref[...] * lk2_ref[...], axis=-1, keepdims=True)
    return jnp.exp(s1) - jnp.exp(s2) + lam_init


def _subln(o, g, lam_init):
    return o * lax.rsqrt(jnp.mean(o * o, axis=-1, keepdims=True) + EPS) * g * (1.0 - lam_init)


def _softmax_step(s, v_bf, m_ref, l_ref, acc_ref):
    m_prev = m_ref[...]
    m_new = jnp.maximum(m_prev, jnp.max(s, axis=-1, keepdims=True))
    alpha = jnp.exp2(m_prev - m_new)
    p = jnp.exp2(s - m_new)
    l_ref[...] = alpha * l_ref[...] + jnp.sum(p, axis=-1, keepdims=True)
    acc_ref[...] = alpha * acc_ref[...] + _dot(p.astype(BF16), v_bf)
    m_ref[...] = m_new


def _attn_prompt_body(lam_init, blk, q_ref, k_ref, vt_ref, lq1_ref, lk1_ref, lq2_ref, lk2_ref, g_ref, o_ref,
                      qq_ref, m_ref, l_ref, acc_ref, st_ref):
    i = pl.program_id(2)
    hw = 2 * ATT_DH
    heads = range(ATT_HPS)
    for h in heads:
        q = q_ref[0, :, h * hw:(h + 1) * hw]
        lane = lax.broadcasted_iota(jnp.int32, q.shape, 1)
        zero = jnp.zeros_like(q)
        qq_ref[h, :blk, :] = jnp.where(lane < ATT_DH, q, zero)
        qq_ref[h, blk:, :] = jnp.where(lane >= ATT_DH, q, zero)
    m_ref[...] = jnp.full(m_ref.shape, -jnp.inf, F32)
    l_ref[...] = jnp.zeros(l_ref.shape, F32)
    acc_ref[...] = jnp.zeros(acc_ref.shape, F32)

    def scores(j, slot):
        start = pl.multiple_of(j * blk, blk)
        for h in heads:
            st_ref[slot * ATT_HPS + h] = _dot_nt(k_ref[0, pl.ds(start, blk), h * hw:(h + 1) * hw], qq_ref[h])

    ones = jnp.ones((BF16_ROWS, blk), BF16)

    def consume(j, slot, diagonal):
        start = pl.multiple_of(j * blk, blk)
        for h in heads:
            st = st_ref[slot * ATT_HPS + h]
            if diagonal:
                key = lax.broadcasted_iota(jnp.int32, st.shape, 0)
                qry = lax.broadcasted_iota(jnp.int32, st.shape, 1) % blk
                st = jnp.where(key // CHUNK <= qry // CHUNK, st, -jnp.inf)
            m_prev = m_ref[h]
            m_new = jnp.maximum(m_prev, jnp.max(st, axis=0, keepdims=True))
            alpha = jnp.exp2(m_prev - m_new)
            p = jnp.exp2(st - m_new).astype(BF16)
            lhs = jnp.concatenate([vt_ref[0, h * hw:(h + 1) * hw, pl.ds(start, blk)], ones], axis=0)
            pv = _dot(lhs, p)
            l_ref[h] = alpha * l_ref[h] + pv[hw:hw + 1]
            acc_ref[h] = alpha * acc_ref[h] + pv[:hw]
            m_ref[h] = m_new

    def block_pair(jj, carry):
        j = 2 * jj
        scores(j + 1, 1)
        consume(j, 0, False)
        scores(j + 2, 0)
        consume(j + 1, 1, False)
        return carry

    scores(0, 0)
    lax.fori_loop(0, i // 2, block_pair, 0)

    @pl.when(i % 2 == 0)
    def _():
        consume(i, 0, True)

    @pl.when(i % 2 == 1)
    def _():
        scores(i, 1)
        consume(i - 1, 0, False)
        consume(i, 1, True)

    lam = _lambda(lq1_ref, lk1_ref, lq2_ref, lk2_ref, lam_init)
    for h in heads:
        o = acc_ref[h] / l_ref[h]
        o = o[:, :blk] - lam * o[:, blk:]
        o = o * lax.rsqrt(jnp.mean(o * o, axis=0, keepdims=True) + EPS) * g_ref[...] * (1.0 - lam_init)
        o_ref[0, :, h * hw:(h + 1) * hw] = o.T.astype(o_ref.dtype)


def _attn_prompt_call(q, k, vt, lq1, lk1, lq2, lk2, subln_g, lam_init, blk):
    b, t, _ = q.shape
    hw = 2 * ATT_DH
    gw = ATT_HPS * hw
    qspec = pl.BlockSpec((1, blk, gw), lambda bi, hi, i: (bi, i, hi))
    kspec = pl.BlockSpec((1, t, gw), lambda bi, hi, i: (bi, 0, hi))
    vtspec = pl.BlockSpec((1, gw, t), lambda bi, hi, i: (bi, hi, 0))
    return pl.pallas_call(
        functools.partial(_attn_prompt_body, lam_init, blk),
        grid=(b, ATT_HEADS // ATT_HPS, t // blk),
        in_specs=[qspec, kspec, vtspec] + [_const_spec((1, ATT_DH))] * 4 + [_const_spec((hw, 1))],
        out_specs=qspec,
        out_shape=jax.ShapeDtypeStruct((b, t, ATT_W), BF16),
        scratch_shapes=[pltpu.VMEM((ATT_HPS, 2 * blk, hw), BF16), pltpu.VMEM((ATT_HPS, 1, 2 * blk), F32),
                        pltpu.VMEM((ATT_HPS, 1, 2 * blk), F32), pltpu.VMEM((ATT_HPS, hw, 2 * blk), F32),
                        pltpu.VMEM((2 * ATT_HPS, blk, 2 * blk), F32)],
        compiler_params=_params(("parallel", "parallel", "arbitrary")),
        name="attn_prompt",
    )(q, k, vt, lq1, lk1, lq2, lk2, subln_g.reshape(hw, 1))


def _attn_sample_body(lam_init, ts, q_ref, ck_ref, cv_ref, kn_ref, vn_ref, lq1_ref, lk1_ref, lq2_ref, lk2_ref,
                      g_ref, o_ref, qq_ref, m_ref, l_ref, acc_ref):
    j = pl.program_id(1)
    nmaps = 2 * ATT_HEADS
    hw = 2 * ATT_DH

    @pl.when(j == 0)
    def _():
        q = jnp.concatenate([q_ref[0]] * nmaps, axis=0)
        row = lax.broadcasted_iota(jnp.int32, q.shape, 0)
        lane = lax.broadcasted_iota(jnp.int32, q.shape, 1)
        qq_ref[...] = jnp.where(row // ts == lane // ATT_DH, q, jnp.zeros_like(q))
        m_ref[...] = jnp.full(m_ref.shape, -jnp.inf, F32)
        l_ref[...] = jnp.zeros(l_ref.shape, F32)
        acc_ref[...] = jnp.zeros(acc_ref.shape, F32)

    def step(k_bf, v_bf):
        _softmax_step(_dot_nt(qq_ref[...], k_bf), v_bf, m_ref, l_ref, acc_ref)

    step(ck_ref[0].astype(BF16), cv_ref[0].astype(BF16))

    @pl.when(j == pl.num_programs(1) - 1)
    def _():
        step(kn_ref[0], vn_ref[0])
        lam = _lambda(lq1_ref, lk1_ref, lq2_ref, lk2_ref, lam_init)
        o = acc_ref[...] / l_ref[...]
        for h in range(ATT_HEADS):
            r0 = 2 * h * ts
            oh = o[r0:r0 + ts, h * hw:(h + 1) * hw] - lam * o[r0 + ts:r0 + 2 * ts, h * hw:(h + 1) * hw]
            o_ref[0, :, h * hw:(h + 1) * hw] = _subln(oh, g_ref[...], lam_init).astype(o_ref.dtype)


def _attn_sample_call(q, cache_k, cache_v, k_new, v_new, lq1, lk1, lq2, lk2, subln_g, lam_init, tk):
    b, ts, _ = q.shape
    past = cache_k.shape[1]
    rows = 2 * ATT_HEADS * ts
    new = pl.BlockSpec((1, ts, ATT_W), lambda bi, j: (bi, 0, 0))
    cache = pl.BlockSpec((1, tk, ATT_W), lambda bi, j: (bi, j, 0))
    return pl.pallas_call(
        functools.partial(_attn_sample_body, lam_init, ts),
        grid=(b, past // tk),
        in_specs=[new, cache, cache, new, new] + [_const_spec((1, ATT_DH))] * 4 + [_const_spec((1, 2 * ATT_DH))],
        out_specs=new,
        out_shape=jax.ShapeDtypeStruct((b, ts, ATT_W), BF16),
        scratch_shapes=[pltpu.VMEM((rows, ATT_W), BF16), pltpu.VMEM((rows, 1), F32), pltpu.VMEM((rows, 1), F32),
                        pltpu.VMEM((rows, ATT_W), F32)],
        compiler_params=_params(("parallel", "arbitrary")),
        name="attn_sample",
    )(q, cache_k, cache_v, k_new, v_new, lq1, lk1, lq2, lk2, subln_g)


def _split2(x):
    hi = x.astype(BF16)
    return hi, (x - hi.astype(F32)).astype(BF16)


def _head_sum(x, ones_bd, passes=2):
    parts = []
    for g in range(x.shape[1] // MXU_W):
        xg = x[:, g * MXU_W:(g + 1) * MXU_W]
        if passes == 1:
            parts.append(_dot(xg.astype(BF16), ones_bd))
        else:
            hi, lo = _split2(xg)
            parts.append(_dot(hi, ones_bd) + _dot(lo, ones_bd))
    return jnp.concatenate(parts, axis=1)


def _bd_expand(x):
    c = x.shape[0]
    xt = jnp.concatenate([x] * WKV_GROUP, axis=0)
    row = lax.broadcasted_iota(jnp.int32, xt.shape, 0)
    lane = lax.broadcasted_iota(jnp.int32, xt.shape, 1)
    return jnp.where(row // c == lane // RW_HEAD, xt, jnp.zeros_like(xt))


def _wkv_chunk_terms(units):
    c = units[0][0].shape[0]
    slab = (c, WKV_GROUP * c)
    t_idx = lax.broadcasted_iota(jnp.int32, slab, 0)
    s_idx = lax.broadcasted_iota(jnp.int32, slab, 1) % c
    strict = s_idx < t_idx
    incl = s_idx <= t_idx
    eye = jnp.where(s_idx == t_idx, 1.0, 0.0)
    expand = lambda x: _bd_expand(x).astype(BF16)

    ar = [jnp.concatenate([u[0], u[1]], axis=0).astype(BF16) for u in units]
    sb = [_dot_nt(a, expand(u[2])) for a, u in zip(ar, units)]
    sk = [_dot_nt(a, expand(u[3])) for a, u in zip(ar, units)]
    l_ab = [jnp.where(strict, x[:c], 0.0) for x in sb]
    m_rb = [jnp.where(incl, x[c:], 0.0).astype(BF16) for x in sb]
    l_ak = [jnp.where(strict, x[:c], 0.0).astype(BF16) for x in sk]
    m_rk = [jnp.where(incl, x[c:], 0.0).astype(BF16) for x in sk]

    lp = l_ab
    tinv = [eye + x for x in l_ab]
    lp = [_dot(x.astype(BF16), expand(x)) for x in lp]
    power = 2
    while power < c:
        last = 2 * power >= c
        nxt = []
        for i, (x, t) in enumerate(zip(lp, tinv)):
            lhs = t if last else jnp.concatenate([x, t], axis=0)
            prod = _dot(lhs.astype(BF16), expand(x))
            if last:
                tinv[i] = t + prod
            else:
                nxt.append(prod[:c])
                tinv[i] = t + prod[c:]
        lp = nxt
        power *= 2

    v_exp = [expand(u[4]) for u in units]
    lakv = [_dot(x, ve) for x, ve in zip(l_ak, v_exp)]
    tinv_bf = [t.astype(BF16) for t in tinv]
    ta = [_dot(t, expand(u[0])) for t, u in zip(tinv_bf, units)]
    uv = [_dot(t, expand(x)) for t, x in zip(tinv_bf, lakv)]

    tr = [jnp.concatenate([x, u[1]], axis=0).astype(BF16) for x, u in zip(ta, units)]
    m_rbk = [jnp.concatenate([m, mk], axis=1) for m, mk in zip(m_rb, m_rk)]
    return list(zip(tr, uv, m_rbk, v_exp))


def _wkv_chunk_apply(states, terms, vs, bkhs, decays):
    c = vs[0].shape[0]
    gw = states[0].shape[0]
    bd = (lax.broadcasted_iota(jnp.int32, (gw, gw), 0) // RW_HEAD
          == lax.broadcasted_iota(jnp.int32, (gw, gw), 1) // RW_HEAD)
    ur = [_dot_nt(t[0], s.astype(BF16)) for t, s in zip(terms, states)]
    u = [x[:c] + t[1] for x, t in zip(ur, terms)]
    upd = [_dot_tn(jnp.concatenate([x, v], axis=0).astype(BF16), bkh) for x, v, bkh in zip(u, vs, bkhs)]
    ys = [x[c:] + _dot(t[2], jnp.concatenate([_bd_expand(uu).astype(BF16), t[3]], axis=0))
          for x, uu, t in zip(ur, u, terms)]
    return ys, [s * d + jnp.where(bd, x, 0.0) for s, d, x in zip(states, decays, upd)]


def _rwkv_body(bb, tb, t_valid, z_ref, s0_ref, sh0_ref, mu_ref, w0_ref, wa_ref, a0_ref, g2_ref, kk_ref, ka_ref,
               rk_ref, lng_ref, lnb_ref, o_ref, s_ref, h_ref, carry_ref):
    ib = pl.program_id(1)
    c = WKV_C
    nc = tb // c

    @pl.when(ib == 0)
    def _():
        for bi in range(bb):
            carry_ref[bi] = sh0_ref[bi]
            for g in range(WKV_NG):
                blocks = [s0_ref[bi, g * WKV_GROUP + h] for h in range(WKV_GROUP)]
                h_ref[bi, g] = _bd_expand(jnp.concatenate(blocks, axis=1))

    z = z_ref[...].reshape(bb * tb, SHIFT_W)
    row = lax.broadcasted_iota(jnp.int32, z.shape, 0)
    carry = jnp.concatenate([jnp.broadcast_to(carry_ref[bi], (tb, SHIFT_W)) for bi in range(bb)], axis=0)
    prev = jnp.where(row % tb == 0, carry, pltpu.roll(z, 1, 0))
    for bi in range(bb):
        carry_ref[bi] = z[(bi + 1) * tb - 1:(bi + 1) * tb, :]
    zs = z + (prev - z) * mu_ref[...]

    r = zs[:, :RW_W]
    k = zs[:, RW_W:2 * RW_W]
    v = zs[:, 2 * RW_W:3 * RW_W]
    zwa = zs[:, 3 * RW_W:3 * RW_W + W_LORA + A_LORA]
    zg = zs[:, 3 * RW_W + W_LORA + A_LORA:]
    lane = lax.broadcasted_iota(jnp.int32, zwa.shape, 1)
    lora = _dot(jnp.where(lane < W_LORA, jnp.tanh(zwa), zwa).astype(BF16), wa_ref[...])
    ew = math.exp(-0.5) * jax.nn.sigmoid(w0_ref[...] + lora[:, :RW_W])
    a = jax.nn.sigmoid(a0_ref[...] + lora[:, RW_W:])
    gate = _dot(jax.nn.sigmoid(zg).astype(BF16), g2_ref[...])

    ones_bd = (lax.broadcasted_iota(jnp.int32, (MXU_W, MXU_W), 0) // RW_HEAD
               == lax.broadcasted_iota(jnp.int32, (MXU_W, MXU_W), 1) // RW_HEAD).astype(BF16)
    kk = k * kk_ref[...]
    kk = kk / jnp.maximum(jnp.sqrt(_head_sum(kk * kk, ones_bd)), 1e-12)
    k = k * (1.0 + (a - 1.0) * ka_ref[...])
    bonus = _head_sum(r * k * rk_ref[...], ones_bd) * v

    if t_valid is not None:
        live = lax.broadcasted_iota(jnp.int32, ew.shape, 0) % tb + ib * tb < t_valid
        ew = jnp.where(live, ew, 0.0)
        k = jnp.where(live, k, 0.0)
        kk = jnp.where(live, kk, 0.0)
        v = jnp.where(live, v, 0.0)

    n = bb * tb
    rows = min(n, MXU_W)
    ti = lax.broadcasted_iota(jnp.int32, (rows, rows), 0)
    si = lax.broadcasted_iota(jnp.int32, (rows, rows), 1)
    tri = jnp.logical_and(si <= ti, si // c == ti // c).astype(BF16)
    e_hi, e_lo = _split2(ew)
    cum = -jnp.concatenate([_dot(tri, e_hi[i0:i0 + rows]) + _dot(tri, e_lo[i0:i0 + rows])
                            for i0 in range(0, n, rows)], axis=0)

    cum_end = jnp.concatenate([jnp.broadcast_to(cum[i0 + c - 1:i0 + c, :], (c, RW_W)) for i0 in range(0, n, c)],
                              axis=0)
    grow = jnp.exp(-cum)
    to_end = jnp.exp(cum_end - cum)
    b = kk * a
    at = -kk * jnp.exp(cum + ew)
    rt = r * jnp.exp(cum)
    bt, kt, bh, kh = b * grow, k * grow, b * to_end, k * to_end
    gamma_end = jnp.exp(cum_end)
    order = [(ci, bi, g) for ci in range(nc) for bi in range(bb) for g in range(WKV_NG)]
    rows_of = lambda ci, bi: slice(bi * tb + ci * c, bi * tb + (ci + 1) * c)
    lanes_of = lambda g: slice(g * WKV_GW, (g + 1) * WKV_GW)
    terms = _wkv_chunk_terms([tuple(x[rows_of(ci, bi), lanes_of(g)] for x in (at, rt, bt, kt, v))
                              for ci, bi, g in order])

    y_parts = {}
    chains = [(bi, g) for bi in range(bb) for g in range(WKV_NG)]
    states = [h_ref[bi, g] for bi, g in chains]
    for ci in range(nc):
        views = [(rows_of(ci, bi), lanes_of(g)) for bi, g in chains]
        ys, states = _wkv_chunk_apply(
            states, terms[ci * len(chains):(ci + 1) * len(chains)], [v[sl, gl] for sl, gl in views],
            [jnp.concatenate([bh[sl, gl], kh[sl, gl]], axis=0).astype(BF16) for sl, gl in views],
            [gamma_end[sl, gl][:1] for sl, gl in views])
        for (bi, g), y_unit in zip(chains, ys):
            y_parts[ci, bi, g] = y_unit
    for (bi, g), s_out in zip(chains, states):
        h_ref[bi, g] = s_out
    y = jnp.concatenate([jnp.concatenate([y_parts[ci, bi, g] for g in range(WKV_NG)], axis=1)
                         for bi in range(bb) for ci in range(nc)], axis=0)

    mean = _head_sum(y, ones_bd) * (1.0 / RW_HEAD)
    d = y - mean
    var = _head_sum(d * d, ones_bd) * (1.0 / RW_HEAD)
    yn = d * lax.rsqrt(var + LNX_EPS) * lng_ref[...] + lnb_ref[...]
    o_ref[...] = ((yn + bonus) * gate).reshape(bb, tb, RW_W).astype(o_ref.dtype)

    @pl.when(ib == pl.num_programs(1) - 1)
    def _():
        for bi in range(bb):
            for g in range(WKV_NG):
                h = h_ref[bi, g]
                for hh in range(WKV_GROUP):
                    blk = h[hh * RW_HEAD:(hh + 1) * RW_HEAD, hh * RW_HEAD:(hh + 1) * RW_HEAD]
                    s_ref[bi, g * WKV_GROUP + hh] = blk


def _rwkv_call(z_rw, s0, shift0, p, bb, tb, t_valid):
    b, t, _ = z_rw.shape
    vec = lambda w: _const_spec((1, w))
    return pl.pallas_call(
        functools.partial(_rwkv_body, bb, tb, t_valid),
        grid=(b // bb, t // tb),
        in_specs=[pl.BlockSpec((bb, tb, SHIFT_W), lambda bi, i: (bi, i, 0)),
                  pl.BlockSpec((bb, RW_HEADS, RW_HEAD, RW_HEAD), lambda bi, i: (bi, 0, 0, 0)),
                  pl.BlockSpec((bb, 1, SHIFT_W), lambda bi, i: (bi, 0, 0)),
                  vec(SHIFT_W), vec(RW_W), _const_spec((W_LORA + A_LORA, 2 * RW_W)), vec(RW_W),
                  _const_spec((G_LORA, RW_W)), vec(RW_W), vec(RW_W), vec(RW_W), vec(RW_W), vec(RW_W)],
        out_specs=[pl.BlockSpec((bb, tb, RW_W), lambda bi, i: (bi, i, 0)),
                   pl.BlockSpec((bb, RW_HEADS, RW_HEAD, RW_HEAD), lambda bi, i: (bi, 0, 0, 0))],
        out_shape=[jax.ShapeDtypeStruct((b, t, RW_W), BF16),
                   jax.ShapeDtypeStruct((b, RW_HEADS, RW_HEAD, RW_HEAD), F32)],
        scratch_shapes=[pltpu.VMEM((bb, WKV_NG, WKV_GW, WKV_GW), F32), pltpu.VMEM((bb, 1, SHIFT_W), F32)],
        compiler_params=_params(("parallel", "arbitrary")),
        name="rwkv",
    )(z_rw, s0, shift0, p["mu"], p["w0"], p["wa"], p["a0"], p["g2"], p["k_k"], p["k_a"], p["r_k"],
      p["lnx_g"], p["lnx_b"])


def _merge_body(h_ref, oa_ref, orw_ref, gate_ref, woa_ref, worw_ref, wout_ref, postg_ref, pre_ref, win_ref,
                wo2_ref, post2_ref, y_ref):
    gates = gate_ref[...]
    merged = (gates[:, :D_MODEL] * _dot(oa_ref[...], woa_ref[...])
              + gates[:, D_MODEL:] * _dot(orw_ref[...], worw_ref[...]))
    h2 = h_ref[...] + _rms(_dot(merged.astype(BF16), wout_ref[...]), postg_ref[...])
    f = _swiglu_ffn(_rms(h2, pre_ref[...]).astype(BF16), win_ref, wo2_ref)
    y_ref[...] = h2 + MACARON * _rms(f, post2_ref[...])


def _merge_call(h, o_att, o_rw, gates, w_o_att, w_o_rwkv, w_out, mix_post_g, pre_g, w_in, w_out2, post_g, tm):
    m = h.shape[0]
    row = lambda w: pl.BlockSpec((tm, w), lambda i: (i, 0))
    vec = _const_spec((1, D_MODEL))
    return pl.pallas_call(
        _merge_body,
        grid=(m // tm,),
        in_specs=[row(D_MODEL), row(ATT_W), row(RW_W), row(GATE_W),
                  _const_spec((ATT_W, D_MODEL)), _const_spec((RW_W, D_MODEL)), _const_spec((D_MODEL, D_MODEL)), vec,
                  vec, _const_spec((D_MODEL, 2 * D_FF)), _const_spec((D_FF, D_MODEL)), vec],
        out_specs=row(D_MODEL),
        out_shape=jax.ShapeDtypeStruct((m, D_MODEL), F32),
        compiler_params=_params(("parallel",)),
        name="merge_ffn",
    )(h, o_att, o_rw, gates, w_o_att, w_o_rwkv, w_out, mix_post_g, pre_g, w_in, w_out2, post_g)


def _pick_tile(m, want):
    t = min(m, want)
    assert m % t == 0
    return t


def _layer(x, pos, l, p, cache_k, cache_v, s0, shift0):
    b, t, _ = x.shape
    m = b * t
    tm = _pick_tile(m, 256)
    lam_init = 0.8 - 0.6 * math.exp(-0.3 * l)

    h = _ffn_call(x.reshape(m, D_MODEL), p["ffn1_pre_g"], p["ffn1_w_in"], p["ffn1_w_out"], p["ffn1_post_g"],
                  _pick_tile(m, 512))

    cos_t, sin_t = _rope_tables(pos)
    if t < tm:
        cos_t, sin_t = jnp.tile(cos_t, (tm // t, 1)), jnp.tile(sin_t, (tm // t, 1))
    q, k, k_bf, v, v_bf, z_rw, gates = _mixproj_call(h, p["mix_pre_g"], p["w_in"], cos_t, sin_t, tm,
                                                     t if cache_k is None else None)

    lam_args = (p["att_lambda_q1"], p["att_lambda_k1"], p["att_lambda_q2"], p["att_lambda_k2"], p["att_subln_g"])
    r3 = lambda a: a.reshape(b, t, a.shape[-1])
    if cache_k is None:
        o_att = _attn_prompt_call(r3(q), r3(k_bf), v_bf, *lam_args, lam_init, _pick_tile(t, 512))
    else:
        past = cache_k.shape[1]
        o_att = _attn_sample_call(r3(q), cache_k.reshape(b, past, ATT_W), cache_v.reshape(b, past, ATT_W),
                                  r3(k_bf), r3(v_bf), *lam_args, lam_init, _pick_tile(past, 1024))

    z_rw3 = r3(z_rw)
    t_pad = -(-t // WKV_C) * WKV_C
    z_in = z_rw3 if t_pad == t else jnp.pad(z_rw3, ((0, 0), (0, t_pad - t), (0, 0)))
    o_rw, s_new = _rwkv_call(z_in, s0, shift0, p["rwkv"], _pick_tile(b, WKV_SEQS), _pick_tile(t_pad, WKV_TB),
                             None if t_pad == t else t)
    o_rw = o_rw[:, :t]

    y = _merge_call(h, o_att.reshape(m, ATT_W), o_rw.reshape(m, RW_W), gates, p["w_o_att"], p["w_o_rwkv"],
                    p["w_out"], p["mix_post_g"], p["ffn2_pre_g"], p["ffn2_w_in"], p["ffn2_w_out"],
                    p["ffn2_post_g"], tm)
    return (y.reshape(b, t, D_MODEL), k.reshape(b, t, ATT_HEADS, 2, ATT_DH), v.reshape(b, t, ATT_HEADS, 2 * ATT_DH),
            s_new, z_rw3[:, -1:])


def kernel(x_prompt, x_sample, cache_att_k, cache_att_v, state_rwkv, state_shift, ffn1_pre_g, ffn1_w_in, ffn1_w_out, ffn1_post_g, mix_pre_g, w_in, att_lambda_q1, att_lambda_k1, att_lambda_q2, att_lambda_k2, att_subln_g, rwkv_mu, rwkv_w0, rwkv_w2, rwkv_a0, rwkv_a2, rwkv_g2, rwkv_k_k, rwkv_k_a, rwkv_r_k, rwkv_lnx_g, rwkv_lnx_b, w_o_att, w_o_rwkv, w_out, mix_post_g, ffn2_pre_g, ffn2_w_in, ffn2_w_out, ffn2_post_g):
    depth = w_in.shape[0]
    bp, tp, _ = x_prompt.shape
    bs, ts, _ = x_sample.shape
    past = cache_att_k.shape[2]
    pos_p = jnp.arange(tp)
    pos_s = past + jnp.arange(ts)
    xp, xs = x_prompt, x_sample
    outs_p, outs_s = [], []
    vec = lambda a: a.reshape(1, -1)
    for l in range(depth):
        zeros = jnp.zeros((W_LORA, RW_W), F32)
        wa = jnp.concatenate([jnp.concatenate([rwkv_w2[l], zeros], axis=1),
                              jnp.concatenate([zeros, rwkv_a2[l]], axis=1)], axis=0)
        p = dict(
            ffn1_pre_g=vec(ffn1_pre_g[l]), ffn1_w_in=ffn1_w_in[l].astype(BF16), ffn1_w_out=ffn1_w_out[l].astype(BF16),
            ffn1_post_g=vec(ffn1_post_g[l]), mix_pre_g=vec(mix_pre_g[l]), w_in=w_in[l].astype(BF16),
            att_lambda_q1=vec(att_lambda_q1[l]), att_lambda_k1=vec(att_lambda_k1[l]),
            att_lambda_q2=vec(att_lambda_q2[l]), att_lambda_k2=vec(att_lambda_k2[l]),
            att_subln_g=vec(att_subln_g[l]),
            rwkv=dict(mu=vec(rwkv_mu[l]), w0=vec(rwkv_w0[l]), wa=wa.astype(BF16), a0=vec(rwkv_a0[l]),
                      g2=rwkv_g2[l].astype(BF16), k_k=vec(rwkv_k_k[l]), k_a=vec(rwkv_k_a[l]), r_k=vec(rwkv_r_k[l]),
                      lnx_g=vec(rwkv_lnx_g[l]), lnx_b=vec(rwkv_lnx_b[l])),
            w_o_att=w_o_att[l].astype(BF16), w_o_rwkv=w_o_rwkv[l].astype(BF16), w_out=w_out[l].astype(BF16),
            mix_post_g=vec(mix_post_g[l]), ffn2_pre_g=vec(ffn2_pre_g[l]), ffn2_w_in=ffn2_w_in[l].astype(BF16),
            ffn2_w_out=ffn2_w_out[l].astype(BF16), ffn2_post_g=vec(ffn2_post_g[l]),
        )
        s0p = jnp.zeros((bp, RW_HEADS, RW_HEAD, RW_HEAD), F32)
        sh0p = jnp.zeros((bp, 1, SHIFT_W), F32)
        xp, *rest_p = _layer(xp, pos_p, l, p, None, None, s0p, sh0p)
        xs, *rest_s = _layer(xs, pos_s, l, p, cache_att_k[l], cache_att_v[l], state_rwkv[l], state_shift[l])
        outs_p.append(rest_p)
        outs_s.append(rest_s)
    stack = lambda outs, i: jnp.stack([o[i] for o in outs], 0)
    return (xp, xs, stack(outs_p, 0), stack(outs_p, 1), stack(outs_p, 2), stack(outs_p, 3),
            stack(outs_s, 0), stack(outs_s, 1), stack(outs_s, 2), stack(outs_s, 3))
```

```python
import functools
import math

import jax
import jax.numpy as jnp
from jax import lax
from jax.experimental import pallas as pl
from jax.experimental.pallas import tpu as pltpu

F32 = jnp.float32
BF16 = jnp.bfloat16

D_MODEL = 1024
D_FF = 2816
CHUNK = 64
ROPE_THETA = 10000.0
EPS = 1e-6
MACARON = 0.5
ATT_HEADS = 4
ATT_DH = 64
ATT_W = ATT_HEADS * 2 * ATT_DH
ATT_HPS = 2
Q_SCALE = ATT_DH ** -0.5 * math.log2(math.e)
RW_HEAD = 64
RW_W = D_MODEL // 2
RW_HEADS = RW_W // RW_HEAD
W_LORA = 64
A_LORA = 64
G_LORA = 128
SHIFT_W = 3 * RW_W + W_LORA + A_LORA + G_LORA
LNX_EPS = 64e-5
GATE_W = 2 * D_MODEL
IN_W = 3 * ATT_W + SHIFT_W + GATE_W

LANES = 128
BF16_ROWS = 16
MXU_W = 256
VMEM_LIMIT = 56 * 1024 * 1024

WKV_C = 64
WKV_GROUP = MXU_W // RW_HEAD
WKV_GW = WKV_GROUP * RW_HEAD
WKV_NG = RW_W // WKV_GW
WKV_SEQS = 4
WKV_TB = 128


def _params(sem):
    return pltpu.CompilerParams(dimension_semantics=sem, vmem_limit_bytes=VMEM_LIMIT)


def _const_spec(shape):
    nd = len(shape)
    return pl.BlockSpec(shape, lambda *_: (0,) * nd, pipeline_mode=pl.Buffered(1))


def _rms(x, g):
    return x * lax.rsqrt(jnp.mean(x * x, axis=-1, keepdims=True) + EPS) * g


def _dot(a, b):
    return jnp.dot(a, b, preferred_element_type=F32)


def _dot_nt(a, b):
    return lax.dot_general(a, b, (((1,), (1,)), ((), ())), preferred_element_type=F32)


def _dot_tn(a, b):
    return lax.dot_general(a, b, (((0,), (0,)), ((), ())), preferred_element_type=F32)


def _swiglu_ffn(xn_bf, w_in_ref, w_out_ref):
    hh = _dot(xn_bf, w_in_ref[...])
    gate = hh[:, :D_FF]
    up = hh[:, D_FF:]
    act = (gate * jax.nn.sigmoid(gate) * up).astype(BF16)
    return _dot(act, w_out_ref[...])


def _ffn_body(x_ref, pre_ref, win_ref, wout_ref, post_ref, o_ref):
    x = x_ref[...]
    f = _swiglu_ffn(_rms(x, pre_ref[...]).astype(BF16), win_ref, wout_ref)
    o_ref[...] = x + MACARON * _rms(f, post_ref[...])


def _ffn_call(x, pre_g, w_in, w_out, post_g, tm):
    m = x.shape[0]
    row = lambda w: pl.BlockSpec((tm, w), lambda i: (i, 0))
    return pl.pallas_call(
        _ffn_body,
        grid=(m // tm,),
        in_specs=[row(D_MODEL), _const_spec((1, D_MODEL)), _const_spec((D_MODEL, 2 * D_FF)),
                  _const_spec((D_FF, D_MODEL)), _const_spec((1, D_MODEL))],
        out_specs=row(D_MODEL),
        out_shape=jax.ShapeDtypeStruct((m, D_MODEL), F32),
        compiler_params=_params(("parallel",)),
        name="ffn",
    )(x, pre_g, w_in, w_out, post_g)


def _rope_tables(pos):
    half = ATT_DH // 2
    inv = ROPE_THETA ** (-jnp.arange(half, dtype=F32) / half)
    ang = pos.astype(F32)[:, None] * inv[None, :]
    cos, sin = jnp.cos(ang), jnp.sin(ang)
    cos_t = jnp.tile(jnp.concatenate([cos, cos], axis=-1), (1, ATT_W // ATT_DH))
    sin_t = jnp.tile(jnp.concatenate([-sin, sin], axis=-1), (1, ATT_W // ATT_DH))
    return cos_t, sin_t


def _rope(x, cos_t, sin_t):
    n = x.shape[-1]
    half = ATT_DH // 2
    lane = lax.broadcasted_iota(jnp.int32, x.shape, 1)
    swapped = jnp.where(lane % ATT_DH < half, pltpu.roll(x, n - half, 1), pltpu.roll(x, half, 1))
    return x * cos_t + swapped * sin_t


def _mixproj_body(v_cols, h_ref, g_ref, w_ref, cos_ref, sin_ref, q_ref, k_ref, kb_ref, v_ref, vb_ref, rw_ref,
                  gate_ref):
    u = _rms(h_ref[...], g_ref[...]).astype(BF16)
    z = _dot(u, w_ref[...])
    cos_t, sin_t = cos_ref[...], sin_ref[...]
    q_ref[...] = (_rope(z[:, :ATT_W], cos_t, sin_t) * Q_SCALE).astype(BF16)
    k = _rope(z[:, ATT_W:2 * ATT_W], cos_t, sin_t)
    kb_ref[...] = k.astype(BF16)
    v = z[:, 2 * ATT_W:3 * ATT_W]
    if v_cols:
        k_ref[0] = k.T
        hw = 2 * ATT_DH
        for h in range(ATT_HEADS):
            v_ref[pl.ds(h, v.shape[0], stride=ATT_HEADS), :] = v[:, h * hw:(h + 1) * hw]
        vb_ref[0] = v.T.astype(BF16)
    else:
        k_ref[...] = k
        v_ref[...] = v
        vb_ref[...] = v.astype(BF16)
    rw_ref[...] = z[:, 3 * ATT_W:3 * ATT_W + SHIFT_W]
    gate_ref[...] = jax.nn.sigmoid(z[:, 3 * ATT_W + SHIFT_W:]).astype(BF16)


def _mixproj_call(h, mix_pre_g, w_in, cos_t, sin_t, tm, v_cols_t):
    m = h.shape[0]
    ntab = cos_t.shape[0] // tm
    row = lambda w: pl.BlockSpec((tm, w), lambda i: (i, 0))
    tab = pl.BlockSpec((tm, ATT_W), lambda i: (i % ntab, 0))
    sds = lambda w, dt: jax.ShapeDtypeStruct((m, w), dt)
    if v_cols_t is None:
        k_spec, k_shape = row(ATT_W), sds(ATT_W, F32)
        v_spec, v_shape = row(ATT_W), sds(ATT_W, F32)
        vb_spec, vb_shape = row(ATT_W), sds(ATT_W, BF16)
    else:
        nt = v_cols_t // tm
        k_spec = vb_spec = pl.BlockSpec((1, ATT_W, tm), lambda i: (i // nt, 0, i % nt))
        k_shape = jax.ShapeDtypeStruct((m // v_cols_t, ATT_W, v_cols_t), F32)
        vb_shape = jax.ShapeDtypeStruct((m // v_cols_t, ATT_W, v_cols_t), BF16)
        v_spec = pl.BlockSpec((tm * ATT_HEADS, 2 * ATT_DH), lambda i: (i, 0))
        v_shape = jax.ShapeDtypeStruct((m * ATT_HEADS, 2 * ATT_DH), F32)
    return pl.pallas_call(
        functools.partial(_mixproj_body, v_cols_t is not None),
        grid=(m // tm,),
        in_specs=[row(D_MODEL), _const_spec((1, D_MODEL)), _const_spec((D_MODEL, IN_W)), tab, tab],
        out_specs=[row(ATT_W), k_spec, row(ATT_W), v_spec, vb_spec, row(SHIFT_W), row(GATE_W)],
        out_shape=[sds(ATT_W, BF16), k_shape, sds(ATT_W, BF16), v_shape, vb_shape,
                   sds(SHIFT_W, F32), sds(GATE_W, BF16)],
        compiler_params=_params(("parallel",)),
        name="mixproj",
    )(h, mix_pre_g, w_in, cos_t, sin_t)


def _lambda(lq1_ref, lk1_ref, lq2_ref, lk2_ref, lam_init):
    s1 = jnp.sum(lq1_ref[...] * lk1_ref[...], axis=-1, keepdims=True)
    s2 = jnp.sum(lq2_ref[...] * lk2_ref[...], axis=-1, keepdims=True)
    return jnp.exp(s1) - jnp.exp(s2) + lam_init


def _subln(o, g, lam_init):
    return o * lax.rsqrt(jnp.mean(o * o, axis=-1, keepdims=True) + EPS) * g * (1.0 - lam_init)


def _attn_prompt_body(lam_init, blk, q_ref, k_ref, vt_ref, lq1_ref, lk1_ref, lq2_ref, lk2_ref, g_ref, o_ref,
                      qq_ref, m_ref, l_ref, acc_ref, st_ref):
    i = pl.program_id(2)
    hw = 2 * ATT_DH
    heads = range(ATT_HPS)
    for h in heads:
        q = q_ref[0, :, h * hw:(h + 1) * hw]
        lane = lax.broadcasted_iota(jnp.int32, q.shape, 1)
        zero = jnp.zeros_like(q)
        qq_ref[h, :blk, :] = jnp.where(lane < ATT_DH, q, zero)
        qq_ref[h, blk:, :] = jnp.where(lane >= ATT_DH, q, zero)
    m_ref[...] = jnp.full(m_ref.shape, -jnp.inf, F32)
    l_ref[...] = jnp.zeros(l_ref.shape, F32)
    acc_ref[...] = jnp.zeros(acc_ref.shape, F32)

    def scores(j, slot):
        start = pl.multiple_of(j * blk, blk)
        for h in heads:
            st_ref[slot * ATT_HPS + h] = _dot_nt(k_ref[0, pl.ds(start, blk), h * hw:(h + 1) * hw], qq_ref[h])

    ones = jnp.ones((BF16_ROWS, blk), BF16)

    def consume(j, slot, diagonal):
        start = pl.multiple_of(j * blk, blk)
        for h in heads:
            st = st_ref[slot * ATT_HPS + h]
            if diagonal:
                key = lax.broadcasted_iota(jnp.int32, st.shape, 0)
                qry = lax.broadcasted_iota(jnp.int32, st.shape, 1) % blk
                st = jnp.where(key // CHUNK <= qry // CHUNK, st, -jnp.inf)
            m_prev = m_ref[h]
            m_new = jnp.maximum(m_prev, jnp.max(st, axis=0, keepdims=True))
            alpha = jnp.exp2(m_prev - m_new)
            p = jnp.exp2(st - m_new).astype(BF16)
            lhs = jnp.concatenate([vt_ref[0, h * hw:(h + 1) * hw, pl.ds(start, blk)], ones], axis=0)
            pv = _dot(lhs, p)
            l_ref[h] = alpha * l_ref[h] + pv[hw:hw + 1]
            acc_ref[h] = alpha * acc_ref[h] + pv[:hw]
            m_ref[h] = m_new

    def block_pair(jj, carry):
        j = 2 * jj
        scores(j + 1, 1)
        consume(j, 0, False)
        scores(j + 2, 0)
        consume(j + 1, 1, False)
        return carry

    scores(0, 0)
    lax.fori_loop(0, i // 2, block_pair, 0)

    @pl.when(i % 2 == 0)
    def _():
        consume(i, 0, True)

    @pl.when(i % 2 == 1)
    def _():
        scores(i, 1)
        consume(i - 1, 0, False)
        consume(i, 1, True)

    lam = _lambda(lq1_ref, lk1_ref, lq2_ref, lk2_ref, lam_init)
    for h in heads:
        o = acc_ref[h] / l_ref[h]
        o = o[:, :blk] - lam * o[:, blk:]
        o = o * lax.rsqrt(jnp.mean(o * o, axis=0, keepdims=True) + EPS) * g_ref[...] * (1.0 - lam_init)
        o_ref[0, :, h * hw:(h + 1) * hw] = o.T.astype(o_ref.dtype)


def _attn_prompt_call(q, k, vt, lq1, lk1, lq2, lk2, subln_g, lam_init, blk):
    b, t, _ = q.shape
    hw = 2 * ATT_DH
    gw = ATT_HPS * hw
    qspec = pl.BlockSpec((1, blk, gw), lambda bi, hi, i: (bi, i, hi))
    kspec = pl.BlockSpec((1, t, gw), lambda bi, hi, i: (bi, 0, hi))
    vtspec = pl.BlockSpec((1, gw, t), lambda bi, hi, i: (bi, hi, 0))
    return pl.pallas_call(
        functools.partial(_attn_prompt_body, lam_init, blk),
        grid=(b, ATT_HEADS // ATT_HPS, t // blk),
        in_specs=[qspec, kspec, vtspec] + [_const_spec((1, ATT_DH))] * 4 + [_const_spec((hw, 1))],
        out_specs=qspec,
        out_shape=jax.ShapeDtypeStruct((b, t, ATT_W), BF16),
        scratch_shapes=[pltpu.VMEM((ATT_HPS, 2 * blk, hw), BF16), pltpu.VMEM((ATT_HPS, 1, 2 * blk), F32),
                        pltpu.VMEM((ATT_HPS, 1, 2 * blk), F32), pltpu.VMEM((ATT_HPS, hw, 2 * blk), F32),
                        pltpu.VMEM((2 * ATT_HPS, blk, 2 * blk), F32)],
        compiler_params=_params(("parallel", "parallel", "arbitrary")),
        name="attn_prompt",
    )(q, k, vt, lq1, lk1, lq2, lk2, subln_g.reshape(hw, 1))


def _attn_sample_body(lam_init, ts, q_ref, ckt_ref, cv_ref, kn_ref, vn_ref, lq1_ref, lk1_ref, lq2_ref, lk2_ref,
                      g_ref, o_ref, qq_ref, m_ref, l_ref, acc_ref):
    j = pl.program_id(1)
    hw = 2 * ATT_DH
    heads = range(ATT_HEADS)

    @pl.when(j == 0)
    def _():
        for h in heads:
            q = q_ref[0, :, h * hw:(h + 1) * hw]
            lane = lax.broadcasted_iota(jnp.int32, q.shape, 1)
            zero = jnp.zeros_like(q)
            qq_ref[h, :ts, :] = jnp.where(lane < ATT_DH, q, zero)
            qq_ref[h, ts:, :] = jnp.where(lane >= ATT_DH, q, zero)
        m_ref[...] = jnp.full(m_ref.shape, -jnp.inf, F32)
        l_ref[...] = jnp.zeros(l_ref.shape, F32)
        acc_ref[...] = jnp.zeros(acc_ref.shape, F32)

    def softmax_step(h, s, v_bf):
        m_prev = m_ref[h]
        m_new = jnp.maximum(m_prev, jnp.max(s, axis=-1, keepdims=True))
        alpha = jnp.exp2(m_prev - m_new)
        p = jnp.exp2(s - m_new)
        l_ref[h] = alpha * l_ref[h] + jnp.sum(p, axis=-1, keepdims=True)
        acc_ref[h] = alpha * acc_ref[h] + _dot(p.astype(BF16), v_bf)
        m_ref[h] = m_new

    for h in heads:
        s = _dot(qq_ref[h], ckt_ref[0, h * hw:(h + 1) * hw, :].astype(BF16))
        softmax_step(h, s, cv_ref[0, pl.ds(h, s.shape[1], stride=ATT_HEADS), :].astype(BF16))

    @pl.when(j == pl.num_programs(1) - 1)
    def _():
        lam = _lambda(lq1_ref, lk1_ref, lq2_ref, lk2_ref, lam_init)
        for h in heads:
            cols = slice(h * hw, (h + 1) * hw)
            softmax_step(h, _dot_nt(qq_ref[h], kn_ref[0, :, cols]), vn_ref[0, :, cols])
            o = acc_ref[h] / l_ref[h]
            o_ref[0, :, cols] = _subln(o[:ts] - lam * o[ts:], g_ref[...], lam_init).astype(o_ref.dtype)


def _attn_sample_call(q, cache_kt, cache_v, k_new, v_new, lq1, lk1, lq2, lk2, subln_g, lam_init, tk):
    b, ts, _ = q.shape
    past = cache_kt.shape[2]
    hw = 2 * ATT_DH
    new = pl.BlockSpec((1, ts, ATT_W), lambda bi, j: (bi, 0, 0))
    return pl.pallas_call(
        functools.partial(_attn_sample_body, lam_init, ts),
        grid=(b, past // tk),
        in_specs=[new, pl.BlockSpec((1, ATT_W, tk), lambda bi, j: (bi, 0, j)),
                  pl.BlockSpec((1, tk * ATT_HEADS, hw), lambda bi, j: (bi, j, 0)), new, new]
        + [_const_spec((1, ATT_DH))] * 4 + [_const_spec((1, hw))],
        out_specs=new,
        out_shape=jax.ShapeDtypeStruct((b, ts, ATT_W), BF16),
        scratch_shapes=[pltpu.VMEM((ATT_HEADS, 2 * ts, hw), BF16), pltpu.VMEM((ATT_HEADS, 2 * ts, 1), F32),
                        pltpu.VMEM((ATT_HEADS, 2 * ts, 1), F32), pltpu.VMEM((ATT_HEADS, 2 * ts, hw), F32)],
        compiler_params=_params(("parallel", "arbitrary")),
        name="attn_sample",
    )(q, cache_kt, cache_v, k_new, v_new, lq1, lk1, lq2, lk2, subln_g)


def _split2(x):
    hi = x.astype(BF16)
    return hi, (x - hi.astype(F32)).astype(BF16)


def _head_sum(x, ones_bd, passes=2):
    parts = []
    for g in range(x.shape[1] // MXU_W):
        xg = x[:, g * MXU_W:(g + 1) * MXU_W]
        if passes == 1:
            parts.append(_dot(xg.astype(BF16), ones_bd))
        else:
            hi, lo = _split2(xg)
            parts.append(_dot(hi, ones_bd) + _dot(lo, ones_bd))
    return jnp.concatenate(parts, axis=1)


def _bd_expand(x):
    c = x.shape[0]
    xt = jnp.concatenate([x] * WKV_GROUP, axis=0)
    row = lax.broadcasted_iota(jnp.int32, xt.shape, 0)
    lane = lax.broadcasted_iota(jnp.int32, xt.shape, 1)
    return jnp.where(row // c == lane // RW_HEAD, xt, jnp.zeros_like(xt))


def _wkv_chunk_terms(units):
    c = units[0][0].shape[0]
    slab = (c, WKV_GROUP * c)
    t_idx = lax.broadcasted_iota(jnp.int32, slab, 0)
    s_idx = lax.broadcasted_iota(jnp.int32, slab, 1) % c
    strict = s_idx < t_idx
    incl = s_idx <= t_idx
    eye = jnp.where(s_idx == t_idx, 1.0, 0.0)
    expand = lambda x: _bd_expand(x).astype(BF16)

    ar = [jnp.concatenate([u[0], u[1]], axis=0).astype(BF16) for u in units]
    sb = [_dot_nt(a, expand(u[2])) for a, u in zip(ar, units)]
    sk = [_dot_nt(a, expand(u[3])) for a, u in zip(ar, units)]
    l_ab = [jnp.where(strict, x[:c], 0.0) for x in sb]
    m_rb = [jnp.where(incl, x[c:], 0.0).astype(BF16) for x in sb]
    l_ak = [jnp.where(strict, x[:c], 0.0).astype(BF16) for x in sk]
    m_rk = [jnp.where(incl, x[c:], 0.0).astype(BF16) for x in sk]

    lp = l_ab
    tinv = [eye + x for x in l_ab]
    lp = [_dot(x.astype(BF16), expand(x)) for x in lp]
    power = 2
    while power < c:
        last = 2 * power >= c
        nxt = []
        for i, (x, t) in enumerate(zip(lp, tinv)):
            lhs = t if last else jnp.concatenate([x, t], axis=0)
            prod = _dot(lhs.astype(BF16), expand(x))
            if last:
                tinv[i] = t + prod
            else:
                nxt.append(prod[:c])
                tinv[i] = t + prod[c:]
        lp = nxt
        power *= 2

    v_exp = [expand(u[4]) for u in units]
    lakv = [_dot(x, ve) for x, ve in zip(l_ak, v_exp)]
    tinv_bf = [t.astype(BF16) for t in tinv]
    ta = [_dot(t, expand(u[0])) for t, u in zip(tinv_bf, units)]
    uv = [_dot(t, expand(x)) for t, x in zip(tinv_bf, lakv)]

    tr = [jnp.concatenate([x, u[1]], axis=0).astype(BF16) for x, u in zip(ta, units)]
    m_rbk = [jnp.concatenate([m, mk], axis=1) for m, mk in zip(m_rb, m_rk)]
    return list(zip(tr, uv, m_rbk, v_exp))


def _wkv_chunk_apply(states, terms, vs, bkhs, decays):
    c = vs[0].shape[0]
    gw = states[0].shape[0]
    bd = (lax.broadcasted_iota(jnp.int32, (gw, gw), 0) // RW_HEAD
          == lax.broadcasted_iota(jnp.int32, (gw, gw), 1) // RW_HEAD)
    ur = [_dot_nt(t[0], s.astype(BF16)) for t, s in zip(terms, states)]
    u = [x[:c] + t[1] for x, t in zip(ur, terms)]
    upd = [_dot_tn(jnp.concatenate([x, v], axis=0).astype(BF16), bkh) for x, v, bkh in zip(u, vs, bkhs)]
    ys = [x[c:] + _dot(t[2], jnp.concatenate([_bd_expand(uu).astype(BF16), t[3]], axis=0))
          for x, uu, t in zip(ur, u, terms)]
    return ys, [s * d + jnp.where(bd, x, 0.0) for s, d, x in zip(states, decays, upd)]


def _rwkv_body(bb, tb, t_valid, z_ref, s0_ref, sh0_ref, mu_ref, w0_ref, wa_ref, a0_ref, g2_ref, kk_ref, ka_ref,
               rk_ref, lng_ref, lnb_ref, o_ref, s_ref, h_ref, carry_ref):
    ib = pl.program_id(1)
    c = WKV_C
    nc = tb // c

    @pl.when(ib == 0)
    def _():
        for bi in range(bb):
            carry_ref[bi] = sh0_ref[bi]
            for g in range(WKV_NG):
                blocks = [s0_ref[bi, g * WKV_GROUP + h] for h in range(WKV_GROUP)]
                h_ref[bi, g] = _bd_expand(jnp.concatenate(blocks, axis=1))

    z = z_ref[...].reshape(bb * tb, SHIFT_W)
    row = lax.broadcasted_iota(jnp.int32, z.shape, 0)
    carry = jnp.concatenate([jnp.broadcast_to(carry_ref[bi], (tb, SHIFT_W)) for bi in range(bb)], axis=0)
    prev = jnp.where(row % tb == 0, carry, pltpu.roll(z, 1, 0))
    for bi in range(bb):
        carry_ref[bi] = z[(bi + 1) * tb - 1:(bi + 1) * tb, :]
    zs = z + (prev - z) * mu_ref[...]

    r = zs[:, :RW_W]
    k = zs[:, RW_W:2 * RW_W]
    v = zs[:, 2 * RW_W:3 * RW_W]
    zwa = zs[:, 3 * RW_W:3 * RW_W + W_LORA + A_LORA]
    zg = zs[:, 3 * RW_W + W_LORA + A_LORA:]
    lane = lax.broadcasted_iota(jnp.int32, zwa.shape, 1)
    lora = _dot(jnp.where(lane < W_LORA, jnp.tanh(zwa), zwa).astype(BF16), wa_ref[...])
    ew = math.exp(-0.5) * jax.nn.sigmoid(w0_ref[...] + lora[:, :RW_W])
    a = jax.nn.sigmoid(a0_ref[...] + lora[:, RW_W:])
    gate = _dot(jax.nn.sigmoid(zg).astype(BF16), g2_ref[...])

    ones_bd = (lax.broadcasted_iota(jnp.int32, (MXU_W, MXU_W), 0) // RW_HEAD
               == lax.broadcasted_iota(jnp.int32, (MXU_W, MXU_W), 1) // RW_HEAD).astype(BF16)
    kk = k * kk_ref[...]
    kk = kk / jnp.maximum(jnp.sqrt(_head_sum(kk * kk, ones_bd)), 1e-12)
    k = k * (1.0 + (a - 1.0) * ka_ref[...])
    bonus = _head_sum(r * k * rk_ref[...], ones_bd) * v

    if t_valid is not None:
        live = lax.broadcasted_iota(jnp.int32, ew.shape, 0) % tb + ib * tb < t_valid
        ew = jnp.where(live, ew, 0.0)
        k = jnp.where(live, k, 0.0)
        kk = jnp.where(live, kk, 0.0)
        v = jnp.where(live, v, 0.0)

    n = bb * tb
    rows = min(n, MXU_W)
    ti = lax.broadcasted_iota(jnp.int32, (rows, rows), 0)
    si = lax.broadcasted_iota(jnp.int32, (rows, rows), 1)
    tri = jnp.logical_and(si <= ti, si // c == ti // c).astype(BF16)
    e_hi, e_lo = _split2(ew)
    cum = -jnp.concatenate([_dot(tri, e_hi[i0:i0 + rows]) + _dot(tri, e_lo[i0:i0 + rows])
                            for i0 in range(0, n, rows)], axis=0)

    cum_end = jnp.concatenate([jnp.broadcast_to(cum[i0 + c - 1:i0 + c, :], (c, RW_W)) for i0 in range(0, n, c)],
                              axis=0)
    grow = jnp.exp(-cum)
    to_end = jnp.exp(cum_end - cum)
    b = kk * a
    at = -kk * jnp.exp(cum + ew)
    rt = r * jnp.exp(cum)
    bt, kt, bh, kh = b * grow, k * grow, b * to_end, k * to_end
    gamma_end = jnp.exp(cum_end)
    order = [(ci, bi, g) for ci in range(nc) for bi in range(bb) for g in range(WKV_NG)]
    rows_of = lambda ci, bi: slice(bi * tb + ci * c, bi * tb + (ci + 1) * c)
    lanes_of = lambda g: slice(g * WKV_GW, (g + 1) * WKV_GW)
    terms = _wkv_chunk_terms([tuple(x[rows_of(ci, bi), lanes_of(g)] for x in (at, rt, bt, kt, v))
                              for ci, bi, g in order])

    y_parts = {}
    chains = [(bi, g) for bi in range(bb) for g in range(WKV_NG)]
    states = [h_ref[bi, g] for bi, g in chains]
    for ci in range(nc):
        views = [(rows_of(ci, bi), lanes_of(g)) for bi, g in chains]
        ys, states = _wkv_chunk_apply(
            states, terms[ci * len(chains):(ci + 1) * len(chains)], [v[sl, gl] for sl, gl in views],
            [jnp.concatenate([bh[sl, gl], kh[sl, gl]], axis=0).astype(BF16) for sl, gl in views],
            [gamma_end[sl, gl][:1] for sl, gl in views])
        for (bi, g), y_unit in zip(chains, ys):
            y_parts[ci, bi, g] = y_unit
    for (bi, g), s_out in zip(chains, states):
        h_ref[bi, g] = s_out
    y = jnp.concatenate([jnp.concatenate([y_parts[ci, bi, g] for g in range(WKV_NG)], axis=1)
                         for bi in range(bb) for ci in range(nc)], axis=0)

    mean = _head_sum(y, ones_bd) * (1.0 / RW_HEAD)
    d = y - mean
    var = _head_sum(d * d, ones_bd) * (1.0 / RW_HEAD)
    yn = d * lax.rsqrt(var + LNX_EPS) * lng_ref[...] + lnb_ref[...]
    o_ref[...] = ((yn + bonus) * gate).reshape(bb, tb, RW_W).astype(o_ref.dtype)

    @pl.when(ib == pl.num_programs(1) - 1)
    def _():
        for bi in range(bb):
            for g in range(WKV_NG):
                h = h_ref[bi, g]
                for hh in range(WKV_GROUP):
                    blk = h[hh * RW_HEAD:(hh + 1) * RW_HEAD, hh * RW_HEAD:(hh + 1) * RW_HEAD]
                    s_ref[bi, g * WKV_GROUP + hh] = blk


def _rwkv_call(z_rw, s0, shift0, p, bb, tb, t_valid):
    b, t, _ = z_rw.shape
    vec = lambda w: _const_spec((1, w))
    return pl.pallas_call(
        functools.partial(_rwkv_body, bb, tb, t_valid),
        grid=(b // bb, t // tb),
        in_specs=[pl.BlockSpec((bb, tb, SHIFT_W), lambda bi, i: (bi, i, 0)),
                  pl.BlockSpec((bb, RW_HEADS, RW_HEAD, RW_HEAD), lambda bi, i: (bi, 0, 0, 0)),
                  pl.BlockSpec((bb, 1, SHIFT_W), lambda bi, i: (bi, 0, 0)),
                  vec(SHIFT_W), vec(RW_W), _const_spec((W_LORA + A_LORA, 2 * RW_W)), vec(RW_W),
                  _const_spec((G_LORA, RW_W)), vec(RW_W), vec(RW_W), vec(RW_W), vec(RW_W), vec(RW_W)],
        out_specs=[pl.BlockSpec((bb, tb, RW_W), lambda bi, i: (bi, i, 0)),
                   pl.BlockSpec((bb, RW_HEADS, RW_HEAD, RW_HEAD), lambda bi, i: (bi, 0, 0, 0))],
        out_shape=[jax.ShapeDtypeStruct((b, t, RW_W), BF16),
                   jax.ShapeDtypeStruct((b, RW_HEADS, RW_HEAD, RW_HEAD), F32)],
        scratch_shapes=[pltpu.VMEM((bb, WKV_NG, WKV_GW, WKV_GW), F32), pltpu.VMEM((bb, 1, SHIFT_W), F32)],
        compiler_params=_params(("parallel", "arbitrary")),
        name="rwkv",
    )(z_rw, s0, shift0, p["mu"], p["w0"], p["wa"], p["a0"], p["g2"], p["k_k"], p["k_a"], p["r_k"],
      p["lnx_g"], p["lnx_b"])


def _merge_body(h_ref, oa_ref, orw_ref, gate_ref, woa_ref, worw_ref, wout_ref, postg_ref, pre_ref, win_ref,
                wo2_ref, post2_ref, y_ref):
    gates = gate_ref[...]
    merged = (gates[:, :D_MODEL] * _dot(oa_ref[...], woa_ref[...])
              + gates[:, D_MODEL:] * _dot(orw_ref[...], worw_ref[...]))
    h2 = h_ref[...] + _rms(_dot(merged.astype(BF16), wout_ref[...]), postg_ref[...])
    f = _swiglu_ffn(_rms(h2, pre_ref[...]).astype(BF16), win_ref, wo2_ref)
    y_ref[...] = h2 + MACARON * _rms(f, post2_ref[...])


def _merge_call(h, o_att, o_rw, gates, w_o_att, w_o_rwkv, w_out, mix_post_g, pre_g, w_in, w_out2, post_g, tm):
    m = h.shape[0]
    row = lambda w: pl.BlockSpec((tm, w), lambda i: (i, 0))
    vec = _const_spec((1, D_MODEL))
    return pl.pallas_call(
        _merge_body,
        grid=(m // tm,),
        in_specs=[row(D_MODEL), row(ATT_W), row(RW_W), row(GATE_W),
                  _const_spec((ATT_W, D_MODEL)), _const_spec((RW_W, D_MODEL)), _const_spec((D_MODEL, D_MODEL)), vec,
                  vec, _const_spec((D_MODEL, 2 * D_FF)), _const_spec((D_FF, D_MODEL)), vec],
        out_specs=row(D_MODEL),
        out_shape=jax.ShapeDtypeStruct((m, D_MODEL), F32),
        compiler_params=_params(("parallel",)),
        name="merge_ffn",
    )(h, o_att, o_rw, gates, w_o_att, w_o_rwkv, w_out, mix_post_g, pre_g, w_in, w_out2, post_g)


def _pick_tile(m, want):
    t = min(m, want)
    assert m % t == 0
    return t


def _layer(x, pos, l, p, cache_k, cache_v, s0, shift0):
    b, t, _ = x.shape
    m = b * t
    tm = _pick_tile(m, 256)
    lam_init = 0.8 - 0.6 * math.exp(-0.3 * l)

    h = _ffn_call(x.reshape(m, D_MODEL), p["ffn1_pre_g"], p["ffn1_w_in"], p["ffn1_w_out"], p["ffn1_post_g"],
                  _pick_tile(m, 512))

    cos_t, sin_t = _rope_tables(pos)
    if t < tm:
        cos_t, sin_t = jnp.tile(cos_t, (tm // t, 1)), jnp.tile(sin_t, (tm // t, 1))
    q, k, k_bf, v, v_bf, z_rw, gates = _mixproj_call(h, p["mix_pre_g"], p["w_in"], cos_t, sin_t, tm,
                                                     t if cache_k is None else None)

    lam_args = (p["att_lambda_q1"], p["att_lambda_k1"], p["att_lambda_q2"], p["att_lambda_k2"], p["att_subln_g"])
    r3 = lambda a: a.reshape(b, t, a.shape[-1])
    if cache_k is None:
        o_att = _attn_prompt_call(r3(q), r3(k_bf), v_bf, *lam_args, lam_init, _pick_tile(t, 512))
    else:
        past = cache_k.shape[1]
        cache_kt = cache_k.reshape(b, past, ATT_W).transpose(0, 2, 1)
        cache_vr = cache_v.reshape(b, past * ATT_HEADS, 2 * ATT_DH)
        o_att = _attn_sample_call(r3(q), cache_kt, cache_vr, r3(k_bf), r3(v_bf), *lam_args, lam_init,
                                  _pick_tile(past, 1024))

    z_rw3 = r3(z_rw)
    t_pad = -(-t // WKV_C) * WKV_C
    z_in = z_rw3 if t_pad == t else jnp.pad(z_rw3, ((0, 0), (0, t_pad - t), (0, 0)))
    o_rw, s_new = _rwkv_call(z_in, s0, shift0, p["rwkv"], _pick_tile(b, WKV_SEQS), _pick_tile(t_pad, WKV_TB),
                             None if t_pad == t else t)
    o_rw = o_rw[:, :t]

    y = _merge_call(h, o_att.reshape(m, ATT_W), o_rw.reshape(m, RW_W), gates, p["w_o_att"], p["w_o_rwkv"],
                    p["w_out"], p["mix_post_g"], p["ffn2_pre_g"], p["ffn2_w_in"], p["ffn2_w_out"],
                    p["ffn2_post_g"], tm)
    if cache_k is None:
        k_rows = k.reshape(b, ATT_HEADS, 2, ATT_DH, t).transpose(0, 4, 1, 2, 3)
    else:
        k_rows = k.reshape(b, t, ATT_HEADS, 2, ATT_DH)
    return y.reshape(b, t, D_MODEL), k_rows, v.reshape(b, t, ATT_HEADS, 2 * ATT_DH), s_new, z_rw3[:, -1:]


def kernel(x_prompt, x_sample, cache_att_k, cache_att_v, state_rwkv, state_shift, ffn1_pre_g, ffn1_w_in, ffn1_w_out, ffn1_post_g, mix_pre_g, w_in, att_lambda_q1, att_lambda_k1, att_lambda_q2, att_lambda_k2, att_subln_g, rwkv_mu, rwkv_w0, rwkv_w2, rwkv_a0, rwkv_a2, rwkv_g2, rwkv_k_k, rwkv_k_a, rwkv_r_k, rwkv_lnx_g, rwkv_lnx_b, w_o_att, w_o_rwkv, w_out, mix_post_g, ffn2_pre_g, ffn2_w_in, ffn2_w_out, ffn2_post_g):
    depth = w_in.shape[0]
    bp, tp, _ = x_prompt.shape
    bs, ts, _ = x_sample.shape
    past = cache_att_k.shape[2]
    pos_p = jnp.arange(tp)
    pos_s = past + jnp.arange(ts)
    xp, xs = x_prompt, x_sample
    outs_p, outs_s = [], []
    vec = lambda a: a.reshape(1, -1)
    for l in range(depth):
        zeros = jnp.zeros((W_LORA, RW_W), F32)
        wa = jnp.concatenate([jnp.concatenate([rwkv_w2[l], zeros], axis=1),
                              jnp.concatenate([zeros, rwkv_a2[l]], axis=1)], axis=0)
        p = dict(
            ffn1_pre_g=vec(ffn1_pre_g[l]), ffn1_w_in=ffn1_w_in[l].astype(BF16), ffn1_w_out=ffn1_w_out[l].astype(BF16),
            ffn1_post_g=vec(ffn1_post_g[l]), mix_pre_g=vec(mix_pre_g[l]), w_in=w_in[l].astype(BF16),
            att_lambda_q1=vec(att_lambda_q1[l]), att_lambda_k1=vec(att_lambda_k1[l]),
            att_lambda_q2=vec(att_lambda_q2[l]), att_lambda_k2=vec(att_lambda_k2[l]),
            att_subln_g=vec(att_subln_g[l]),
            rwkv=dict(mu=vec(rwkv_mu[l]), w0=vec(rwkv_w0[l]), wa=wa.astype(BF16), a0=vec(rwkv_a0[l]),
                      g2=rwkv_g2[l].astype(BF16), k_k=vec(rwkv_k_k[l]), k_a=vec(rwkv_k_a[l]), r_k=vec(rwkv_r_k[l]),
                      lnx_g=vec(rwkv_lnx_g[l]), lnx_b=vec(rwkv_lnx_b[l])),
            w_o_att=w_o_att[l].astype(BF16), w_o_rwkv=w_o_rwkv[l].astype(BF16), w_out=w_out[l].astype(BF16),
            mix_post_g=vec(mix_post_g[l]), ffn2_pre_g=vec(ffn2_pre_g[l]), ffn2_w_in=ffn2_w_in[l].astype(BF16),
            ffn2_w_out=ffn2_w_out[l].astype(BF16), ffn2_post_g=vec(ffn2_post_g[l]),
        )
        s0p = jnp.zeros((bp, RW_HEADS, RW_HEAD, RW_HEAD), F32)
        sh0p = jnp.zeros((bp, 1, SHIFT_W), F32)
        xp, *rest_p = _layer(xp, pos_p, l, p, None, None, s0p, sh0p)
        xs, *rest_s = _layer(xs, pos_s, l, p, cache_att_k[l], cache_att_v[l], state_rwkv[l], state_shift[l])
        outs_p.append(rest_p)
        outs_s.append(rest_s)
    stack = lambda outs, i: jnp.stack([o[i] for o in outs], 0)
    return (xp, xs, stack(outs_p, 0), stack(outs_p, 1), stack(outs_p, 2), stack(outs_p, 3),
            stack(outs_s, 0), stack(outs_s, 1), stack(outs_s, 2), stack(outs_s, 3))
```

```python
import functools
import math

import jax
import jax.numpy as jnp
from jax import lax
from jax.experimental import pallas as pl
from jax.experimental.pallas import tpu as pltpu

F32 = jnp.float32
BF16 = jnp.bfloat16

D_MODEL = 1024
D_FF = 2816
CHUNK = 64
ROPE_THETA = 10000.0
EPS = 1e-6
MACARON = 0.5
ATT_HEADS = 4
ATT_DH = 64
ATT_W = ATT_HEADS * 2 * ATT_DH
ATT_HPS = 2
Q_SCALE = ATT_DH ** -0.5 * math.log2(math.e)
RW_HEAD = 64
RW_W = D_MODEL // 2
RW_HEADS = RW_W // RW_HEAD
W_LORA = 64
A_LORA = 64
G_LORA = 128
SHIFT_W = 3 * RW_W + W_LORA + A_LORA + G_LORA
LNX_EPS = 64e-5
GATE_W = 2 * D_MODEL
IN_W = 3 * ATT_W + SHIFT_W + GATE_W

LANES = 128
F32_ROWS = 8
BF16_ROWS = 16
MXU_W = 256
VMEM_LIMIT = 56 * 1024 * 1024

WKV_C = 64
WKV_GROUP = MXU_W // RW_HEAD
WKV_GW = WKV_GROUP * RW_HEAD
WKV_NG = RW_W // WKV_GW
WKV_SEQS = 4
WKV_TB = 256


def _params(sem):
    return pltpu.CompilerParams(dimension_semantics=sem, vmem_limit_bytes=VMEM_LIMIT)


def _const_spec(shape):
    nd = len(shape)
    return pl.BlockSpec(shape, lambda *_: (0,) * nd, pipeline_mode=pl.Buffered(1))


def _rms(x, g):
    return x * lax.rsqrt(jnp.mean(x * x, axis=-1, keepdims=True) + EPS) * g


def _dot(a, b):
    return jnp.dot(a, b, preferred_element_type=F32)


def _dot_nt(a, b):
    return lax.dot_general(a, b, (((1,), (1,)), ((), ())), preferred_element_type=F32)


def _dot_tn(a, b):
    return lax.dot_general(a, b, (((0,), (0,)), ((), ())), preferred_element_type=F32)


def _swiglu_ffn(xn_bf, w_in_ref, w_out_ref):
    hh = _dot(xn_bf, w_in_ref[...])
    gate = hh[:, :D_FF]
    up = hh[:, D_FF:]
    act = (gate * jax.nn.sigmoid(gate) * up).astype(BF16)
    return _dot(act, w_out_ref[...])


def _ffn_body(x_ref, pre_ref, win_ref, wout_ref, post_ref, o_ref):
    x = x_ref[...]
    f = _swiglu_ffn(_rms(x, pre_ref[...]).astype(BF16), win_ref, wout_ref)
    o_ref[...] = x + MACARON * _rms(f, post_ref[...])


def _ffn_call(x, pre_g, w_in, w_out, post_g, tm):
    m = x.shape[0]
    row = lambda w: pl.BlockSpec((tm, w), lambda i: (i, 0))
    return pl.pallas_call(
        _ffn_body,
        grid=(m // tm,),
        in_specs=[row(D_MODEL), _const_spec((1, D_MODEL)), _const_spec((D_MODEL, 2 * D_FF)),
                  _const_spec((D_FF, D_MODEL)), _const_spec((1, D_MODEL))],
        out_specs=row(D_MODEL),
        out_shape=jax.ShapeDtypeStruct((m, D_MODEL), F32),
        compiler_params=_params(("parallel",)),
        name="ffn",
    )(x, pre_g, w_in, w_out, post_g)


def _rope_tables(pos):
    half = ATT_DH // 2
    inv = ROPE_THETA ** (-jnp.arange(half, dtype=F32) / half)
    ang = pos.astype(F32)[:, None] * inv[None, :]
    cos, sin = jnp.cos(ang), jnp.sin(ang)
    cos_t = jnp.tile(jnp.concatenate([cos, cos], axis=-1), (1, LANES // ATT_DH))
    sin_t = jnp.tile(jnp.concatenate([-sin, sin], axis=-1), (1, LANES // ATT_DH))
    return cos_t, sin_t


def _rope(x, cos_t, sin_t):
    n = x.shape[-1]
    half = ATT_DH // 2
    lane = lax.broadcasted_iota(jnp.int32, x.shape, 1)
    swapped = jnp.where(lane % ATT_DH < half, pltpu.roll(x, n - half, 1), pltpu.roll(x, half, 1))
    return x * cos_t + swapped * sin_t


def _mixproj_body(v_cols, h_ref, g_ref, w_ref, cos_ref, sin_ref, q_ref, k_ref, kb_ref, v_ref, vb_ref, rw_ref,
                  gate_ref):
    u = _rms(h_ref[...], g_ref[...]).astype(BF16)
    z = _dot(u, w_ref[...])
    cos_t = jnp.tile(cos_ref[...], (1, ATT_W // LANES))
    sin_t = jnp.tile(sin_ref[...], (1, ATT_W // LANES))
    q_ref[...] = (_rope(z[:, :ATT_W], cos_t, sin_t) * Q_SCALE).astype(BF16)
    k = _rope(z[:, ATT_W:2 * ATT_W], cos_t, sin_t)
    kb_ref[...] = k.astype(BF16)
    v = z[:, 2 * ATT_W:3 * ATT_W]
    if v_cols:
        k_ref[0] = k.T
        hw = 2 * ATT_DH
        for h in range(ATT_HEADS):
            v_ref[pl.ds(h, v.shape[0], stride=ATT_HEADS), :] = v[:, h * hw:(h + 1) * hw]
        vb_ref[0] = v.T.astype(BF16)
    else:
        k_ref[...] = k
        v_ref[...] = v
        vb_ref[...] = v.astype(BF16)
    rw_ref[...] = z[:, 3 * ATT_W:3 * ATT_W + SHIFT_W]
    gate_ref[...] = jax.nn.sigmoid(z[:, 3 * ATT_W + SHIFT_W:]).astype(BF16)


def _mixproj_call(h, mix_pre_g, w_in, cos_t, sin_t, tm, v_cols_t):
    m = h.shape[0]
    ntab = cos_t.shape[0] // tm
    row = lambda w: pl.BlockSpec((tm, w), lambda i: (i, 0))
    tab = pl.BlockSpec((tm, LANES), lambda i: (i % ntab, 0))
    sds = lambda w, dt: jax.ShapeDtypeStruct((m, w), dt)
    if v_cols_t is None:
        k_spec, k_shape = row(ATT_W), sds(ATT_W, F32)
        v_spec, v_shape = row(ATT_W), sds(ATT_W, F32)
        vb_spec, vb_shape = row(ATT_W), sds(ATT_W, BF16)
    else:
        nt = v_cols_t // tm
        k_spec = vb_spec = pl.BlockSpec((1, ATT_W, tm), lambda i: (i // nt, 0, i % nt))
        k_shape = jax.ShapeDtypeStruct((m // v_cols_t, ATT_W, v_cols_t), F32)
        vb_shape = jax.ShapeDtypeStruct((m // v_cols_t, ATT_W, v_cols_t), BF16)
        v_spec = pl.BlockSpec((tm * ATT_HEADS, 2 * ATT_DH), lambda i: (i, 0))
        v_shape = jax.ShapeDtypeStruct((m * ATT_HEADS, 2 * ATT_DH), F32)
    return pl.pallas_call(
        functools.partial(_mixproj_body, v_cols_t is not None),
        grid=(m // tm,),
        in_specs=[row(D_MODEL), _const_spec((1, D_MODEL)), _const_spec((D_MODEL, IN_W)), tab, tab],
        out_specs=[row(ATT_W), k_spec, row(ATT_W), v_spec, vb_spec, row(SHIFT_W), row(GATE_W)],
        out_shape=[sds(ATT_W, BF16), k_shape, sds(ATT_W, BF16), v_shape, vb_shape,
                   sds(SHIFT_W, F32), sds(GATE_W, BF16)],
        compiler_params=_params(("parallel",)),
        name="mixproj",
    )(h, mix_pre_g, w_in, cos_t, sin_t)


def _lambda(lq1_ref, lk1_ref, lq2_ref, lk2_ref, lam_init):
    s1 = jnp.sum(lq1_ref[...] * lk1_ref[...], axis=-1, keepdims=True)
    s2 = jnp.sum(lq2_ref[...] * lk2_ref[...], axis=-1, keepdims=True)
    return jnp.exp(s1) - jnp.exp(s2) + lam_init


def _subln(o, g, lam_init):
    return o * lax.rsqrt(jnp.mean(o * o, axis=-1, keepdims=True) + EPS) * g * (1.0 - lam_init)


def _attn_prompt_body(lam_init, blk, q_ref, k_ref, vt_ref, lq1_ref, lk1_ref, lq2_ref, lk2_ref, g_ref, o_ref,
                      qq_ref, m_ref, l_ref, acc_ref, st_ref):
    i = pl.program_id(2)
    hw = 2 * ATT_DH
    heads = range(ATT_HPS)
    for h in heads:
        q = q_ref[0, :, h * hw:(h + 1) * hw]
        lane = lax.broadcasted_iota(jnp.int32, q.shape, 1)
        zero = jnp.zeros_like(q)
        qq_ref[h, :blk, :] = jnp.where(lane < ATT_DH, q, zero)
        qq_ref[h, blk:, :] = jnp.where(lane >= ATT_DH, q, zero)
    m_ref[...] = jnp.full(m_ref.shape, -jnp.inf, F32)
    l_ref[...] = jnp.zeros(l_ref.shape, F32)
    acc_ref[...] = jnp.zeros(acc_ref.shape, F32)

    def scores(h, j):
        start = pl.multiple_of(j * blk, blk)
        st_ref[h] = _dot_nt(k_ref[0, pl.ds(start, blk), h * hw:(h + 1) * hw], qq_ref[h])

    ones = jnp.ones((BF16_ROWS, blk), BF16)

    def consume(h, j, diagonal):
        start = pl.multiple_of(j * blk, blk)
        st = st_ref[h]
        if diagonal:
            key = lax.broadcasted_iota(jnp.int32, st.shape, 0)
            qry = lax.broadcasted_iota(jnp.int32, st.shape, 1) % blk
            st = jnp.where(key // CHUNK <= qry // CHUNK, st, -jnp.inf)
        m_prev = m_ref[h]
        m_new = jnp.maximum(m_prev, jnp.max(st, axis=0, keepdims=True))
        alpha = jnp.exp2(m_prev - m_new)
        p = jnp.exp2(st - m_new).astype(BF16)
        lhs = jnp.concatenate([vt_ref[0, h * hw:(h + 1) * hw, pl.ds(start, blk)], ones], axis=0)
        pv = _dot(lhs, p)
        l_ref[h] = alpha * l_ref[h] + pv[hw:hw + 1]
        acc_ref[h] = alpha * acc_ref[h] + pv[:hw]
        m_ref[h] = m_new

    def block(j, carry):
        scores(1, j)
        consume(0, j, False)
        scores(0, j + 1)
        consume(1, j, False)
        return carry

    scores(0, 0)
    lax.fori_loop(0, i, block, 0)
    scores(1, i)
    consume(0, i, True)
    consume(1, i, True)

    lam = _lambda(lq1_ref, lk1_ref, lq2_ref, lk2_ref, lam_init)
    for h in heads:
        o = acc_ref[h] / l_ref[h]
        o = o[:, :blk] - lam * o[:, blk:]
        o = o * lax.rsqrt(jnp.mean(o * o, axis=0, keepdims=True) + EPS) * g_ref[...] * (1.0 - lam_init)
        o_ref[0, :, h * hw:(h + 1) * hw] = o.T.astype(o_ref.dtype)


def _attn_prompt_call(q, k, vt, lq1, lk1, lq2, lk2, subln_g, lam_init, blk):
    b, t, _ = q.shape
    hw = 2 * ATT_DH
    gw = ATT_HPS * hw
    qspec = pl.BlockSpec((1, blk, gw), lambda bi, hi, i: (bi, i, hi))
    kspec = pl.BlockSpec((1, t, gw), lambda bi, hi, i: (bi, 0, hi))
    vtspec = pl.BlockSpec((1, gw, t), lambda bi, hi, i: (bi, hi, 0))
    return pl.pallas_call(
        functools.partial(_attn_prompt_body, lam_init, blk),
        grid=(b, ATT_HEADS // ATT_HPS, t // blk),
        in_specs=[qspec, kspec, vtspec] + [_const_spec((1, ATT_DH))] * 4 + [_const_spec((hw, 1))],
        out_specs=qspec,
        out_shape=jax.ShapeDtypeStruct((b, t, ATT_W), BF16),
        scratch_shapes=[pltpu.VMEM((ATT_HPS, 2 * blk, hw), BF16), pltpu.VMEM((ATT_HPS, 1, 2 * blk), F32),
                        pltpu.VMEM((ATT_HPS, 1, 2 * blk), F32), pltpu.VMEM((ATT_HPS, hw, 2 * blk), F32),
                        pltpu.VMEM((ATT_HPS, blk, 2 * blk), F32)],
        compiler_params=_params(("parallel", "parallel", "arbitrary")),
        name="attn_prompt",
    )(q, k, vt, lq1, lk1, lq2, lk2, subln_g.reshape(hw, 1))


def _attn_sample_body(lam_init, ts, q_ref, ckt_ref, cv_ref, kn_ref, vn_ref, lq1_ref, lk1_ref, lq2_ref, lk2_ref,
                      g_ref, o_ref, qq_ref, m_ref, l_ref, acc_ref):
    j = pl.program_id(1)
    hw = 2 * ATT_DH
    heads = range(ATT_HEADS)

    @pl.when(j == 0)
    def _():
        for h in heads:
            q = q_ref[0, :, h * hw:(h + 1) * hw]
            lane = lax.broadcasted_iota(jnp.int32, q.shape, 1)
            zero = jnp.zeros_like(q)
            qq_ref[h, :ts, :] = jnp.where(lane < ATT_DH, q, zero)
            qq_ref[h, ts:, :] = jnp.where(lane >= ATT_DH, q, zero)
        m_ref[...] = jnp.full(m_ref.shape, -jnp.inf, F32)
        l_ref[...] = jnp.zeros(l_ref.shape, F32)
        acc_ref[...] = jnp.zeros(acc_ref.shape, F32)

    def softmax_step(h, s, v_bf):
        m_prev = m_ref[h]
        m_new = jnp.maximum(m_prev, jnp.max(s, axis=-1, keepdims=True))
        alpha = jnp.exp2(m_prev - m_new)
        p = jnp.exp2(s - m_new)
        l_ref[h] = alpha * l_ref[h] + jnp.sum(p, axis=-1, keepdims=True)
        acc_ref[h] = alpha * acc_ref[h] + _dot(p.astype(BF16), v_bf)
        m_ref[h] = m_new

    for h in heads:
        s = _dot(qq_ref[h], ckt_ref[0, h * hw:(h + 1) * hw, :].astype(BF16))
        softmax_step(h, s, cv_ref[0, pl.ds(h, s.shape[1], stride=ATT_HEADS), :].astype(BF16))

    @pl.when(j == pl.num_programs(1) - 1)
    def _():
        lam = _lambda(lq1_ref, lk1_ref, lq2_ref, lk2_ref, lam_init)
        for h in heads:
            cols = slice(h * hw, (h + 1) * hw)
            softmax_step(h, _dot_nt(qq_ref[h], kn_ref[0, :, cols]), vn_ref[0, :, cols])
            o = acc_ref[h] / l_ref[h]
            o_ref[0, :, cols] = _subln(o[:ts] - lam * o[ts:], g_ref[...], lam_init).astype(o_ref.dtype)


def _attn_sample_call(q, cache_kt, cache_v, k_new, v_new, lq1, lk1, lq2, lk2, subln_g, lam_init, tk):
    b, ts, _ = q.shape
    past = cache_kt.shape[2]
    hw = 2 * ATT_DH
    new = pl.BlockSpec((1, ts, ATT_W), lambda bi, j: (bi, 0, 0))
    return pl.pallas_call(
        functools.partial(_attn_sample_body, lam_init, ts),
        grid=(b, past // tk),
        in_specs=[new, pl.BlockSpec((1, ATT_W, tk), lambda bi, j: (bi, 0, j)),
                  pl.BlockSpec((1, tk * ATT_HEADS, hw), lambda bi, j: (bi, j, 0)), new, new]
        + [_const_spec((1, ATT_DH))] * 4 + [_const_spec((1, hw))],
        out_specs=new,
        out_shape=jax.ShapeDtypeStruct((b, ts, ATT_W), BF16),
        scratch_shapes=[pltpu.VMEM((ATT_HEADS, 2 * ts, hw), BF16), pltpu.VMEM((ATT_HEADS, 2 * ts, 1), F32),
                        pltpu.VMEM((ATT_HEADS, 2 * ts, 1), F32), pltpu.VMEM((ATT_HEADS, 2 * ts, hw), F32)],
        compiler_params=_params(("parallel", "arbitrary")),
        name="attn_sample",
    )(q, cache_kt, cache_v, k_new, v_new, lq1, lk1, lq2, lk2, subln_g)


def _split2(x):
    hi = x.astype(BF16)
    return hi, (x - hi.astype(F32)).astype(BF16)


def _head_sum(x, ones_bd, passes=2):
    parts = []
    for g in range(x.shape[1] // MXU_W):
        xg = x[:, g * MXU_W:(g + 1) * MXU_W]
        if passes == 1:
            parts.append(_dot(xg.astype(BF16), ones_bd))
        else:
            hi, lo = _split2(xg)
            parts.append(_dot(hi, ones_bd) + _dot(lo, ones_bd))
    return jnp.concatenate(parts, axis=1)


def _bd_expand(x):
    c = x.shape[0]
    xt = jnp.concatenate([x] * WKV_GROUP, axis=0)
    row = lax.broadcasted_iota(jnp.int32, xt.shape, 0)
    lane = lax.broadcasted_iota(jnp.int32, xt.shape, 1)
    return jnp.where(row // c == lane // RW_HEAD, xt, jnp.zeros_like(xt))


def _wkv_chunk_terms(units):
    c = units[0][0].shape[0]
    slab = (c, WKV_GROUP * c)
    t_idx = lax.broadcasted_iota(jnp.int32, slab, 0)
    s_idx = lax.broadcasted_iota(jnp.int32, slab, 1) % c
    strict = s_idx < t_idx
    incl = s_idx <= t_idx
    eye = jnp.where(s_idx == t_idx, 1.0, 0.0)
    expand = lambda x: _bd_expand(x).astype(BF16)

    ar = [jnp.concatenate([u[0], u[1]], axis=0).astype(BF16) for u in units]
    sb = [_dot_nt(a, expand(u[2])) for a, u in zip(ar, units)]
    yield
    sk = [_dot_nt(a, expand(u[3])) for a, u in zip(ar, units)]
    yield
    l_ab = [jnp.where(strict, x[:c], 0.0) for x in sb]
    m_rb = [jnp.where(incl, x[c:], 0.0).astype(BF16) for x in sb]
    l_ak = [jnp.where(strict, x[:c], 0.0).astype(BF16) for x in sk]
    m_rk = [jnp.where(incl, x[c:], 0.0).astype(BF16) for x in sk]

    lp = l_ab
    tinv = [eye + x for x in l_ab]
    lp = [_dot(x.astype(BF16), expand(x)) for x in lp]
    yield
    power = 2
    while power < c:
        last = 2 * power >= c
        nxt = []
        for i, (x, t) in enumerate(zip(lp, tinv)):
            lhs = t if last else jnp.concatenate([x, t], axis=0)
            prod = _dot(lhs.astype(BF16), expand(x))
            if last:
                tinv[i] = t + prod
            else:
                nxt.append(prod[:c])
                tinv[i] = t + prod[c:]
        lp = nxt
        power *= 2
        yield

    v_exp = [expand(u[4]) for u in units]
    lakv = [_dot(x, ve) for x, ve in zip(l_ak, v_exp)]
    yield
    tinv_bf = [t.astype(BF16) for t in tinv]
    ta = [_dot(t, expand(u[0])) for t, u in zip(tinv_bf, units)]
    yield
    uv = [_dot(t, expand(x)) for t, x in zip(tinv_bf, lakv)]
    yield

    tr = [jnp.concatenate([x, u[1]], axis=0).astype(BF16) for x, u in zip(ta, units)]
    m_rbk = [jnp.concatenate([m, mk], axis=1) for m, mk in zip(m_rb, m_rk)]
    return list(zip(tr, uv, m_rbk, v_exp))


def _wkv_chunk_apply(states, terms, vs, bkhs, decays):
    c = vs[0].shape[0]
    gw = states[0].shape[0]
    bd = (lax.broadcasted_iota(jnp.int32, (gw, gw), 0) // RW_HEAD
          == lax.broadcasted_iota(jnp.int32, (gw, gw), 1) // RW_HEAD)
    ur = [_dot_nt(t[0], s.astype(BF16)) for t, s in zip(terms, states)]
    yield
    u = [x[:c] + t[1] for x, t in zip(ur, terms)]
    upd = [_dot_tn(jnp.concatenate([x, v], axis=0).astype(BF16), bkh) for x, v, bkh in zip(u, vs, bkhs)]
    yield
    ys = [x[c:] + _dot(t[2], jnp.concatenate([_bd_expand(uu).astype(BF16), t[3]], axis=0))
          for x, uu, t in zip(ur, u, terms)]
    yield
    return ys, [s * d + jnp.where(bd, x, 0.0) for s, d, x in zip(states, decays, upd)]


def _interleave(*gens):
    results = [None] * len(gens)
    live = list(range(len(gens)))
    while live:
        for i in list(live):
            try:
                next(gens[i])
            except StopIteration as stop:
                results[i] = stop.value
                live.remove(i)
    return results


def _rwkv_body(bb, tb, t_valid, z_ref, s0_ref, sh0_ref, mu_ref, w0_ref, wa_ref, a0_ref, g2_ref, kk_ref, ka_ref,
               rk_ref, lng_ref, lnb_ref, o_ref, s_ref, h_ref, carry_ref):
    ib = pl.program_id(1)
    c = WKV_C
    nc = tb // c
    lanes_of = lambda g: slice(g * WKV_GW, (g + 1) * WKV_GW)

    @pl.when(ib == 0)
    def _():
        for bi in range(bb):
            carry_ref[bi] = sh0_ref[bi]
            for g in range(WKV_NG):
                blocks = [s0_ref[bi, g * WKV_GROUP + h] for h in range(WKV_GROUP)]
                h_ref[bi, g] = _bd_expand(jnp.concatenate(blocks, axis=1))

    ones_bd = (lax.broadcasted_iota(jnp.int32, (MXU_W, MXU_W), 0) // RW_HEAD
               == lax.broadcasted_iota(jnp.int32, (MXU_W, MXU_W), 1) // RW_HEAD).astype(BF16)

    def prepare(seqs):
        n = len(seqs) * tb
        z = jnp.concatenate([z_ref[bi] for bi in seqs], axis=0)
        mu = mu_ref[...]
        zs = z + (pltpu.roll(z, 1, 0) - z) * mu
        first = lax.broadcasted_iota(jnp.int32, (F32_ROWS, SHIFT_W), 0) == 0
        pieces = []
        for i, bi in enumerate(seqs):
            head = z[i * tb:i * tb + F32_ROWS]
            pieces += [jnp.where(first, head + (carry_ref[bi] - head) * mu, zs[i * tb:i * tb + F32_ROWS]),
                       zs[i * tb + F32_ROWS:(i + 1) * tb]]
            carry_ref[bi] = z[(i + 1) * tb - 1:(i + 1) * tb, :]
        zs = jnp.concatenate(pieces, axis=0)
        yield

        r = zs[:, :RW_W]
        k = zs[:, RW_W:2 * RW_W]
        v = zs[:, 2 * RW_W:3 * RW_W]
        zwa = zs[:, 3 * RW_W:3 * RW_W + W_LORA + A_LORA]
        zg = zs[:, 3 * RW_W + W_LORA + A_LORA:]
        lane = lax.broadcasted_iota(jnp.int32, zwa.shape, 1)
        lora = _dot(jnp.where(lane < W_LORA, jnp.tanh(zwa), zwa).astype(BF16), wa_ref[...])
        ew = math.exp(-0.5) * jax.nn.sigmoid(w0_ref[...] + lora[:, :RW_W])
        a = jax.nn.sigmoid(a0_ref[...] + lora[:, RW_W:])
        yield
        gate = _dot(jax.nn.sigmoid(zg).astype(BF16), g2_ref[...])
        yield

        kk = k * kk_ref[...]
        kk = kk * lax.rsqrt(jnp.maximum(_head_sum(kk * kk, ones_bd), 1e-24))
        yield
        k = k * (1.0 + (a - 1.0) * ka_ref[...])
        bonus = _head_sum(r * k * rk_ref[...], ones_bd) * v
        yield

        if t_valid is not None:
            live = lax.broadcasted_iota(jnp.int32, ew.shape, 0) % tb + ib * tb < t_valid
            ew = jnp.where(live, ew, 0.0)
            k = jnp.where(live, k, 0.0)
            kk = jnp.where(live, kk, 0.0)
            v = jnp.where(live, v, 0.0)

        rows = min(n, MXU_W)
        ti = lax.broadcasted_iota(jnp.int32, (rows, rows), 0)
        si = lax.broadcasted_iota(jnp.int32, (rows, rows), 1)
        tri = jnp.logical_and(si <= ti, si // c == ti // c).astype(BF16)
        e_hi, e_lo = _split2(ew)
        cum = -jnp.concatenate([_dot(tri, e_hi[i0:i0 + rows]) + _dot(tri, e_lo[i0:i0 + rows])
                                for i0 in range(0, n, rows)], axis=0)
        cum_end = jnp.concatenate([jnp.broadcast_to(cum[i0 + c - 1:i0 + c, :], (c, RW_W))
                                   for i0 in range(0, n, c)], axis=0)
        yield
        grow = jnp.exp(-cum)
        to_end = jnp.exp(cum_end - cum)
        b = kk * a
        at = -kk * jnp.exp(cum + ew)
        rt = r * jnp.exp(cum)
        yield
        pre = dict(at=at, rt=rt, bt=b * grow, kt=k * grow, v=v, gate=gate, bonus=bonus)
        yield
        bh, kh, gamma_end = b * to_end, k * to_end, jnp.exp(cum_end)
        for ci in range(nc):
            for i in range(len(seqs)):
                sl = rows_of(ci, i)
                for g in range(WKV_NG):
                    gl = lanes_of(g)
                    pre[ci, i, g] = (v[sl, gl], jnp.concatenate([bh[sl, gl], kh[sl, gl]], axis=0).astype(BF16),
                                     gamma_end[sl, gl][:1])
        return pre

    rows_of = lambda ci, i: slice(i * tb + ci * c, i * tb + (ci + 1) * c)

    def chunk_terms(pre, nseq):
        order = [(ci, i, g) for ci in range(nc) for i in range(nseq) for g in range(WKV_NG)]
        units = [tuple(pre[name][rows_of(ci, i), lanes_of(g)] for name in ("at", "rt", "bt", "kt", "v"))
                 for ci, i, g in order]
        return dict(zip(order, (yield from _wkv_chunk_terms(units))))

    def apply_chunks(seqs, pre, terms):
        chains = [(i, g) for i in range(len(seqs)) for g in range(WKV_NG)]
        states = [h_ref[seqs[i], g] for i, g in chains]
        y_parts = {}
        for ci in range(nc):
            extra = [pre[ci, i, g] for i, g in chains]
            ys, states = yield from _wkv_chunk_apply(states, [terms[ci, i, g] for i, g in chains],
                                                     [e[0] for e in extra], [e[1] for e in extra],
                                                     [e[2] for e in extra])
            for (i, g), y_unit in zip(chains, ys):
                y_parts[ci, i, g] = y_unit
        for (i, g), s_out in zip(chains, states):
            h_ref[seqs[i], g] = s_out
        return jnp.concatenate([jnp.concatenate([y_parts[ci, i, g] for g in range(WKV_NG)], axis=1)
                                for i in range(len(seqs)) for ci in range(nc)], axis=0)

    def finish(seqs, pre, y):
        mean = _head_sum(y, ones_bd) * (1.0 / RW_HEAD)
        yield
        d = y - mean
        var = _head_sum(d * d, ones_bd) * (1.0 / RW_HEAD)
        yield
        yn = d * lax.rsqrt(var + LNX_EPS) * lng_ref[...] + lnb_ref[...]
        out = ((yn + pre["bonus"]) * pre["gate"]).astype(o_ref.dtype)
        for i, bi in enumerate(seqs):
            o_ref[bi] = out[i * tb:(i + 1) * tb]

    halves = [list(range(bb))] if bb == 1 else [list(range(bb // 2)), list(range(bb // 2, bb))]
    first, second = halves[0], halves[-1]
    pre_a, = _interleave(prepare(first))
    if len(halves) == 1:
        terms_a, = _interleave(chunk_terms(pre_a, len(first)))
        y_a, = _interleave(apply_chunks(first, pre_a, terms_a))
        _interleave(finish(first, pre_a, y_a))
    else:
        terms_a, pre_b = _interleave(chunk_terms(pre_a, len(first)), prepare(second))
        y_a, terms_b = _interleave(apply_chunks(first, pre_a, terms_a), chunk_terms(pre_b, len(second)))
        y_b, _ = _interleave(apply_chunks(second, pre_b, terms_b), finish(first, pre_a, y_a))
        _interleave(finish(second, pre_b, y_b))

    @pl.when(ib == pl.num_programs(1) - 1)
    def _():
        for bi in range(bb):
            for g in range(WKV_NG):
                h = h_ref[bi, g]
                for hh in range(WKV_GROUP):
                    blk = h[hh * RW_HEAD:(hh + 1) * RW_HEAD, hh * RW_HEAD:(hh + 1) * RW_HEAD]
                    s_ref[bi, g * WKV_GROUP + hh] = blk


def _rwkv_call(z_rw, s0, shift0, p, bb, tb, t_valid):
    b, t, _ = z_rw.shape
    vec = lambda w: _const_spec((1, w))
    return pl.pallas_call(
        functools.partial(_rwkv_body, bb, tb, t_valid),
        grid=(b // bb, t // tb),
        in_specs=[pl.BlockSpec((bb, tb, SHIFT_W), lambda bi, i: (bi, i, 0)),
                  pl.BlockSpec((bb, RW_HEADS, RW_HEAD, RW_HEAD), lambda bi, i: (bi, 0, 0, 0)),
                  pl.BlockSpec((bb, 1, SHIFT_W), lambda bi, i: (bi, 0, 0)),
                  vec(SHIFT_W), vec(RW_W), _const_spec((W_LORA + A_LORA, 2 * RW_W)), vec(RW_W),
                  _const_spec((G_LORA, RW_W)), vec(RW_W), vec(RW_W), vec(RW_W), vec(RW_W), vec(RW_W)],
        out_specs=[pl.BlockSpec((bb, tb, RW_W), lambda bi, i: (bi, i, 0)),
                   pl.BlockSpec((bb, RW_HEADS, RW_HEAD, RW_HEAD), lambda bi, i: (bi, 0, 0, 0))],
        out_shape=[jax.ShapeDtypeStruct((b, t, RW_W), BF16),
                   jax.ShapeDtypeStruct((b, RW_HEADS, RW_HEAD, RW_HEAD), F32)],
        scratch_shapes=[pltpu.VMEM((bb, WKV_NG, WKV_GW, WKV_GW), F32), pltpu.VMEM((bb, 1, SHIFT_W), F32)],
        compiler_params=_params(("parallel", "arbitrary")),
        name="rwkv",
    )(z_rw, s0, shift0, p["mu"], p["w0"], p["wa"], p["a0"], p["g2"], p["k_k"], p["k_a"], p["r_k"],
      p["lnx_g"], p["lnx_b"])


def _merge_body(h_ref, oa_ref, orw_ref, gate_ref, woa_ref, worw_ref, wout_ref, postg_ref, pre_ref, win_ref,
                wo2_ref, post2_ref, y_ref):
    gates = gate_ref[...]
    merged = (gates[:, :D_MODEL] * _dot(oa_ref[...], woa_ref[...])
              + gates[:, D_MODEL:] * _dot(orw_ref[...], worw_ref[...]))
    h2 = h_ref[...] + _rms(_dot(merged.astype(BF16), wout_ref[...]), postg_ref[...])
    f = _swiglu_ffn(_rms(h2, pre_ref[...]).astype(BF16), win_ref, wo2_ref)
    y_ref[...] = h2 + MACARON * _rms(f, post2_ref[...])


def _merge_call(h, o_att, o_rw, gates, w_o_att, w_o_rwkv, w_out, mix_post_g, pre_g, w_in, w_out2, post_g, tm):
    m = h.shape[0]
    row = lambda w: pl.BlockSpec((tm, w), lambda i: (i, 0))
    vec = _const_spec((1, D_MODEL))
    return pl.pallas_call(
        _merge_body,
        grid=(m // tm,),
        in_specs=[row(D_MODEL), row(ATT_W), row(RW_W), row(GATE_W),
                  _const_spec((ATT_W, D_MODEL)), _const_spec((RW_W, D_MODEL)), _const_spec((D_MODEL, D_MODEL)), vec,
                  vec, _const_spec((D_MODEL, 2 * D_FF)), _const_spec((D_FF, D_MODEL)), vec],
        out_specs=row(D_MODEL),
        out_shape=jax.ShapeDtypeStruct((m, D_MODEL), F32),
        compiler_params=_params(("parallel",)),
        name="merge_ffn",
    )(h, o_att, o_rw, gates, w_o_att, w_o_rwkv, w_out, mix_post_g, pre_g, w_in, w_out2, post_g)


def _pick_tile(m, want):
    t = min(m, want)
    assert m % t == 0
    return t


def _layer(x, pos, l, p, cache_k, cache_v, s0, shift0):
    b, t, _ = x.shape
    m = b * t
    tm = _pick_tile(m, 512)
    lam_init = 0.8 - 0.6 * math.exp(-0.3 * l)

    h = _ffn_call(x.reshape(m, D_MODEL), p["ffn1_pre_g"], p["ffn1_w_in"], p["ffn1_w_out"], p["ffn1_post_g"],
                  _pick_tile(m, 512))

    cos_t, sin_t = _rope_tables(pos)
    tmp = _pick_tile(m, 512)
    if t < tmp:
        cos_t, sin_t = jnp.tile(cos_t, (tmp // t, 1)), jnp.tile(sin_t, (tmp // t, 1))
    q, k, k_bf, v, v_bf, z_rw, gates = _mixproj_call(h, p["mix_pre_g"], p["w_in"], cos_t, sin_t, tmp,
                                                     t if cache_k is None else None)

    lam_args = (p["att_lambda_q1"], p["att_lambda_k1"], p["att_lambda_q2"], p["att_lambda_k2"], p["att_subln_g"])
    r3 = lambda a: a.reshape(b, t, a.shape[-1])
    if cache_k is None:
        o_att = _attn_prompt_call(r3(q), r3(k_bf), v_bf, *lam_args, lam_init, _pick_tile(t, 512))
    else:
        past = cache_k.shape[1]
        cache_kt = cache_k.reshape(b, past, ATT_W).transpose(0, 2, 1)
        cache_vr = cache_v.reshape(b, past * ATT_HEADS, 2 * ATT_DH)
        o_att = _attn_sample_call(r3(q), cache_kt, cache_vr, r3(k_bf), r3(v_bf), *lam_args, lam_init,
                                  _pick_tile(past, 1024))

    z_rw3 = r3(z_rw)
    t_pad = -(-t // WKV_C) * WKV_C
    z_in = z_rw3 if t_pad == t else jnp.pad(z_rw3, ((0, 0), (0, t_pad - t), (0, 0)))
    o_rw, s_new = _rwkv_call(z_in, s0, shift0, p["rwkv"], _pick_tile(b, WKV_SEQS), _pick_tile(t_pad, WKV_TB),
                             None if t_pad == t else t)
    o_rw = o_rw[:, :t]

    y = _merge_call(h, o_att.reshape(m, ATT_W), o_rw.reshape(m, RW_W), gates, p["w_o_att"], p["w_o_rwkv"],
                    p["w_out"], p["mix_post_g"], p["ffn2_pre_g"], p["ffn2_w_in"], p["ffn2_w_out"],
                    p["ffn2_post_g"], tm)
    if cache_k is None:
        k_rows = k.reshape(b, ATT_HEADS, 2, ATT_DH, t).transpose(0, 4, 1, 2, 3)
    else:
        k_rows = k.reshape(b, t, ATT_HEADS, 2, ATT_DH)
    return y.reshape(b, t, D_MODEL), k_rows, v.reshape(b, t, ATT_HEADS, 2 * ATT_DH), s_new, z_rw3[:, -1:]


def kernel(x_prompt, x_sample, cache_att_k, cache_att_v, state_rwkv, state_shift, ffn1_pre_g, ffn1_w_in, ffn1_w_out, ffn1_post_g, mix_pre_g, w_in, att_lambda_q1, att_lambda_k1, att_lambda_q2, att_lambda_k2, att_subln_g, rwkv_mu, rwkv_w0, rwkv_w2, rwkv_a0, rwkv_a2, rwkv_g2, rwkv_k_k, rwkv_k_a, rwkv_r_k, rwkv_lnx_g, rwkv_lnx_b, w_o_att, w_o_rwkv, w_out, mix_post_g, ffn2_pre_g, ffn2_w_in, ffn2_w_out, ffn2_post_g):
    depth = w_in.shape[0]
    bp, tp, _ = x_prompt.shape
    bs, ts, _ = x_sample.shape
    past = cache_att_k.shape[2]
    pos_p = jnp.arange(tp)
    pos_s = past + jnp.arange(ts)
    xp, xs = x_prompt, x_sample
    outs_p, outs_s = [], []
    vec = lambda a: a.reshape(1, -1)
    for l in range(depth):
        zeros = jnp.zeros((W_LORA, RW_W), F32)
        wa = jnp.concatenate([jnp.concatenate([rwkv_w2[l], zeros], axis=1),
                              jnp.concatenate([zeros, rwkv_a2[l]], axis=1)], axis=0)
        p = dict(
            ffn1_pre_g=vec(ffn1_pre_g[l]), ffn1_w_in=ffn1_w_in[l].astype(BF16), ffn1_w_out=ffn1_w_out[l].astype(BF16),
            ffn1_post_g=vec(ffn1_post_g[l]), mix_pre_g=vec(mix_pre_g[l]), w_in=w_in[l].astype(BF16),
            att_lambda_q1=vec(att_lambda_q1[l]), att_lambda_k1=vec(att_lambda_k1[l]),
            att_lambda_q2=vec(att_lambda_q2[l]), att_lambda_k2=vec(att_lambda_k2[l]),
            att_subln_g=vec(att_subln_g[l]),
            rwkv=dict(mu=vec(rwkv_mu[l]), w0=vec(rwkv_w0[l]), wa=wa.astype(BF16), a0=vec(rwkv_a0[l]),
                      g2=rwkv_g2[l].astype(BF16), k_k=vec(rwkv_k_k[l]), k_a=vec(rwkv_k_a[l]), r_k=vec(rwkv_r_k[l]),
                      lnx_g=vec(rwkv_lnx_g[l]), lnx_b=vec(rwkv_lnx_b[l])),
            w_o_att=w_o_att[l].astype(BF16), w_o_rwkv=w_o_rwkv[l].astype(BF16), w_out=w_out[l].astype(BF16),
            mix_post_g=vec(mix_post_g[l]), ffn2_pre_g=vec(ffn2_pre_g[l]), ffn2_w_in=ffn2_w_in[l].astype(BF16),
            ffn2_w_out=ffn2_w_out[l].astype(BF16), ffn2_post_g=vec(ffn2_post_g[l]),
        )
        s0p = jnp.zeros((bp, RW_HEADS, RW_HEAD, RW_HEAD), F32)
        sh0p = jnp.zeros((bp, 1, SHIFT_W), F32)
        xp, *rest_p = _layer(xp, pos_p, l, p, None, None, s0p, sh0p)
        xs, *rest_s = _layer(xs, pos_s, l, p, cache_att_k[l], cache_att_v[l], state_rwkv[l], state_shift[l])
        outs_p.append(rest_p)
        outs_s.append(rest_s)
    stack = lambda outs, i: jnp.stack([o[i] for o in outs], 0)
    return (xp, xs, stack(outs_p, 0), stack(outs_p, 1), stack(outs_p, 2), stack(outs_p, 3),
            stack(outs_s, 0), stack(outs_s, 1), stack(outs_s, 2), stack(outs_s, 3))
```

```python
import functools
import math

import jax
import jax.numpy as jnp
from jax import lax
from jax.experimental import pallas as pl
from jax.experimental.pallas import tpu as pltpu

F32 = jnp.float32
BF16 = jnp.bfloat16

D_MODEL = 1024
D_FF = 2816
CHUNK = 64
ROPE_THETA = 10000.0
EPS = 1e-6
MACARON = 0.5
ATT_HEADS = 4
ATT_DH = 64
ATT_W = ATT_HEADS * 2 * ATT_DH
ATT_HPS = 2
Q_SCALE = ATT_DH ** -0.5 * math.log2(math.e)
RW_HEAD = 64
RW_W = D_MODEL // 2
RW_HEADS = RW_W // RW_HEAD
W_LORA = 64
A_LORA = 64
G_LORA = 128
SHIFT_W = 3 * RW_W + W_LORA + A_LORA + G_LORA
LNX_EPS = 64e-5
GATE_W = 2 * D_MODEL
IN_W = 3 * ATT_W + SHIFT_W + GATE_W

LANES = 128
F32_ROWS = 8
BF16_ROWS = 16
MXU_W = 256
VMEM_LIMIT = 56 * 1024 * 1024

WKV_C = 64
WKV_GROUP = MXU_W // RW_HEAD
WKV_GW = WKV_GROUP * RW_HEAD
WKV_NG = RW_W // WKV_GW
WKV_SEQS = 4
WKV_TB = 256


def _params(sem):
    return pltpu.CompilerParams(dimension_semantics=sem, vmem_limit_bytes=VMEM_LIMIT)


def _const_spec(shape):
    nd = len(shape)
    return pl.BlockSpec(shape, lambda *_: (0,) * nd, pipeline_mode=pl.Buffered(1))


def _rms(x, g):
    return x * lax.rsqrt(jnp.mean(x * x, axis=-1, keepdims=True) + EPS) * g


def _dot(a, b):
    return jnp.dot(a, b, preferred_element_type=F32)


def _dot_nt(a, b):
    return lax.dot_general(a, b, (((1,), (1,)), ((), ())), preferred_element_type=F32)


def _dot_tn(a, b):
    return lax.dot_general(a, b, (((0,), (0,)), ((), ())), preferred_element_type=F32)


def _swiglu_ffn(xn_bf, w_in_ref, w_out_ref):
    hh = _dot(xn_bf, w_in_ref[...])
    gate = hh[:, :D_FF]
    up = hh[:, D_FF:]
    act = (gate * jax.nn.sigmoid(gate) * up).astype(BF16)
    return _dot(act, w_out_ref[...])


def _ffn_body(cast_steps, x_ref, pre_ref, win_ref, wout_ref, post_ref, *refs):
    ncast = len(cast_steps)
    cast_in, o_ref, cast_out = refs[:ncast], refs[ncast], refs[ncast + 1:]
    x = x_ref[...]
    f = _swiglu_ffn(_rms(x, pre_ref[...]).astype(BF16), win_ref, wout_ref)
    o_ref[...] = x + MACARON * _rms(f, post_ref[...])
    for w_ref, c_ref, steps in zip(cast_in, cast_out, cast_steps):
        @pl.when(pl.program_id(0) < steps)
        def _():
            c_ref[...] = w_ref[...].astype(BF16)


def _cast_rows(rows, nsteps):
    return next(r for r in range(BF16_ROWS, rows + 1, BF16_ROWS) if rows % r == 0 and rows // r <= nsteps)


def _ffn_call(x, pre_g, w_in, w_out, post_g, tm, cast=()):
    m = x.shape[0]
    nsteps = m // tm
    row = lambda w: pl.BlockSpec((tm, w), lambda i: (i, 0))
    slabs = [_cast_rows(w.shape[0], nsteps) for w in cast]
    steps = tuple(w.shape[0] // r for w, r in zip(cast, slabs))
    slab_index = lambda i, last: (jnp.minimum(i, last), 0)
    cast_specs = [pl.BlockSpec((r, w.shape[1]), functools.partial(slab_index, last=s - 1))
                  for w, r, s in zip(cast, slabs, steps)]
    out = pl.pallas_call(
        functools.partial(_ffn_body, steps),
        grid=(nsteps,),
        in_specs=[row(D_MODEL), _const_spec((1, D_MODEL)), _const_spec((D_MODEL, 2 * D_FF)),
                  _const_spec((D_FF, D_MODEL)), _const_spec((1, D_MODEL))] + cast_specs,
        out_specs=[row(D_MODEL)] + cast_specs,
        out_shape=[jax.ShapeDtypeStruct((m, D_MODEL), F32)] + [jax.ShapeDtypeStruct(w.shape, BF16) for w in cast],
        compiler_params=_params(("arbitrary",) if cast else ("parallel",)),
        name="ffn",
    )(x, pre_g, w_in, w_out, post_g, *cast)
    return (out[0], out[1:]) if cast else out[0]


def _rope_tables(pos):
    half = ATT_DH // 2
    inv = ROPE_THETA ** (-jnp.arange(half, dtype=F32) / half)
    ang = pos.astype(F32)[:, None] * inv[None, :]
    cos, sin = jnp.cos(ang), jnp.sin(ang)
    cos_t = jnp.tile(jnp.concatenate([cos, cos], axis=-1), (1, LANES // ATT_DH))
    sin_t = jnp.tile(jnp.concatenate([-sin, sin], axis=-1), (1, LANES // ATT_DH))
    return cos_t, sin_t


def _rope(x, cos_t, sin_t):
    n = x.shape[-1]
    half = ATT_DH // 2
    lane = lax.broadcasted_iota(jnp.int32, x.shape, 1)
    swapped = jnp.where(lane % ATT_DH < half, pltpu.roll(x, n - half, 1), pltpu.roll(x, half, 1))
    return x * cos_t + swapped * sin_t


def _mixproj_body(v_cols, h_ref, g_ref, w_ref, cos_ref, sin_ref, q_ref, k_ref, kb_ref, v_ref, vb_ref, rw_ref,
                  gate_ref):
    u = _rms(h_ref[...], g_ref[...]).astype(BF16)
    z = _dot(u, w_ref[...])
    cos_t = jnp.tile(cos_ref[...], (1, ATT_W // LANES))
    sin_t = jnp.tile(sin_ref[...], (1, ATT_W // LANES))
    q_ref[...] = (_rope(z[:, :ATT_W], cos_t, sin_t) * Q_SCALE).astype(BF16)
    k = _rope(z[:, ATT_W:2 * ATT_W], cos_t, sin_t)
    kb_ref[...] = k.astype(BF16)
    v = z[:, 2 * ATT_W:3 * ATT_W]
    if v_cols:
        k_ref[0] = k.T
        hw = 2 * ATT_DH
        for h in range(ATT_HEADS):
            v_ref[pl.ds(h, v.shape[0], stride=ATT_HEADS), :] = v[:, h * hw:(h + 1) * hw]
        vb_ref[0] = v.T.astype(BF16)
    else:
        k_ref[...] = k
        v_ref[...] = v
        vb_ref[...] = v.astype(BF16)
    rw_ref[...] = z[:, 3 * ATT_W:3 * ATT_W + SHIFT_W]
    gate_ref[...] = jax.nn.sigmoid(z[:, 3 * ATT_W + SHIFT_W:]).astype(BF16)


def _mixproj_call(h, mix_pre_g, w_in, cos_t, sin_t, tm, v_cols_t):
    m = h.shape[0]
    ntab = cos_t.shape[0] // tm
    row = lambda w: pl.BlockSpec((tm, w), lambda i: (i, 0))
    tab = pl.BlockSpec((tm, LANES), lambda i: (i % ntab, 0))
    sds = lambda w, dt: jax.ShapeDtypeStruct((m, w), dt)
    if v_cols_t is None:
        k_spec, k_shape = row(ATT_W), sds(ATT_W, F32)
        v_spec, v_shape = row(ATT_W), sds(ATT_W, F32)
        vb_spec, vb_shape = row(ATT_W), sds(ATT_W, BF16)
    else:
        nt = v_cols_t // tm
        k_spec = vb_spec = pl.BlockSpec((1, ATT_W, tm), lambda i: (i // nt, 0, i % nt))
        k_shape = jax.ShapeDtypeStruct((m // v_cols_t, ATT_W, v_cols_t), F32)
        vb_shape = jax.ShapeDtypeStruct((m // v_cols_t, ATT_W, v_cols_t), BF16)
        v_spec = pl.BlockSpec((tm * ATT_HEADS, 2 * ATT_DH), lambda i: (i, 0))
        v_shape = jax.ShapeDtypeStruct((m * ATT_HEADS, 2 * ATT_DH), F32)
    return pl.pallas_call(
        functools.partial(_mixproj_body, v_cols_t is not None),
        grid=(m // tm,),
        in_specs=[row(D_MODEL), _const_spec((1, D_MODEL)), _const_spec((D_MODEL, IN_W)), tab, tab],
        out_specs=[row(ATT_W), k_spec, row(ATT_W), v_spec, vb_spec, row(SHIFT_W), row(GATE_W)],
        out_shape=[sds(ATT_W, BF16), k_shape, sds(ATT_W, BF16), v_shape, vb_shape,
                   sds(SHIFT_W, F32), sds(GATE_W, BF16)],
        compiler_params=_params(("parallel",)),
        name="mixproj",
    )(h, mix_pre_g, w_in, cos_t, sin_t)


def _lambda(lq1_ref, lk1_ref, lq2_ref, lk2_ref, lam_init):
    s1 = jnp.sum(lq1_ref[...] * lk1_ref[...], axis=-1, keepdims=True)
    s2 = jnp.sum(lq2_ref[...] * lk2_ref[...], axis=-1, keepdims=True)
    return jnp.exp(s1) - jnp.exp(s2) + lam_init


def _subln(o, g, lam_init):
    return o * lax.rsqrt(jnp.mean(o * o, axis=-1, keepdims=True) + EPS) * g * (1.0 - lam_init)


def _attn_prompt_body(lam_init, blk, q_ref, k_ref, vt_ref, lq1_ref, lk1_ref, lq2_ref, lk2_ref, g_ref, o_ref,
                      qq_ref, m_ref, l_ref, acc_ref, st_ref):
    i = pl.program_id(2)
    hw = 2 * ATT_DH
    heads = range(ATT_HPS)
    for h in heads:
        q = q_ref[0, :, h * hw:(h + 1) * hw]
        lane = lax.broadcasted_iota(jnp.int32, q.shape, 1)
        zero = jnp.zeros_like(q)
        qq_ref[h, :blk, :] = jnp.where(lane < ATT_DH, q, zero)
        qq_ref[h, blk:, :] = jnp.where(lane >= ATT_DH, q, zero)
    m_ref[...] = jnp.full(m_ref.shape, -jnp.inf, F32)
    l_ref[...] = jnp.zeros(l_ref.shape, F32)
    acc_ref[...] = jnp.zeros(acc_ref.shape, F32)

    def scores(h, j):
        start = pl.multiple_of(j * blk, blk)
        st_ref[h] = _dot_nt(k_ref[0, pl.ds(start, blk), h * hw:(h + 1) * hw], qq_ref[h])

    ones = jnp.ones((BF16_ROWS, blk), BF16)

    def consume(h, j, diagonal):
        start = pl.multiple_of(j * blk, blk)
        st = st_ref[h]
        if diagonal:
            key = lax.broadcasted_iota(jnp.int32, st.shape, 0)
            qry = lax.broadcasted_iota(jnp.int32, st.shape, 1) % blk
            st = jnp.where(key // CHUNK <= qry // CHUNK, st, -jnp.inf)
        m_prev = m_ref[h]
        m_new = jnp.maximum(m_prev, jnp.max(st, axis=0, keepdims=True))
        alpha = jnp.exp2(m_prev - m_new)
        p = jnp.exp2(st - m_new).astype(BF16)
        lhs = jnp.concatenate([vt_ref[0, h * hw:(h + 1) * hw, pl.ds(start, blk)], ones], axis=0)
        pv = _dot(lhs, p)
        l_ref[h] = alpha * l_ref[h] + pv[hw:hw + 1]
        acc_ref[h] = alpha * acc_ref[h] + pv[:hw]
        m_ref[h] = m_new

    def block(j, carry):
        scores(1, j)
        consume(0, j, False)
        scores(0, j + 1)
        consume(1, j, False)
        return carry

    scores(0, 0)
    lax.fori_loop(0, i, block, 0)
    scores(1, i)
    consume(0, i, True)
    consume(1, i, True)

    lam = _lambda(lq1_ref, lk1_ref, lq2_ref, lk2_ref, lam_init)
    for h in heads:
        o = acc_ref[h] / l_ref[h]
        o = o[:, :blk] - lam * o[:, blk:]
        o = o * lax.rsqrt(jnp.mean(o * o, axis=0, keepdims=True) + EPS) * g_ref[...] * (1.0 - lam_init)
        o_ref[0, :, h * hw:(h + 1) * hw] = o.T.astype(o_ref.dtype)


def _attn_prompt_call(q, k, vt, lq1, lk1, lq2, lk2, subln_g, lam_init, blk):
    b, t, _ = q.shape
    hw = 2 * ATT_DH
    gw = ATT_HPS * hw
    qspec = pl.BlockSpec((1, blk, gw), lambda bi, hi, i: (bi, i, hi))
    kspec = pl.BlockSpec((1, t, gw), lambda bi, hi, i: (bi, 0, hi))
    vtspec = pl.BlockSpec((1, gw, t), lambda bi, hi, i: (bi, hi, 0))
    return pl.pallas_call(
        functools.partial(_attn_prompt_body, lam_init, blk),
        grid=(b, ATT_HEADS // ATT_HPS, t // blk),
        in_specs=[qspec, kspec, vtspec] + [_const_spec((1, ATT_DH))] * 4 + [_const_spec((hw, 1))],
        out_specs=qspec,
        out_shape=jax.ShapeDtypeStruct((b, t, ATT_W), BF16),
        scratch_shapes=[pltpu.VMEM((ATT_HPS, 2 * blk, hw), BF16), pltpu.VMEM((ATT_HPS, 1, 2 * blk), F32),
                        pltpu.VMEM((ATT_HPS, 1, 2 * blk), F32), pltpu.VMEM((ATT_HPS, hw, 2 * blk), F32),
                        pltpu.VMEM((ATT_HPS, blk, 2 * blk), F32)],
        compiler_params=_params(("parallel", "parallel", "arbitrary")),
        name="attn_prompt",
    )(q, k, vt, lq1, lk1, lq2, lk2, subln_g.reshape(hw, 1))


def _attn_sample_body(lam_init, ts, q_ref, ckt_ref, cv_ref, kn_ref, vn_ref, lq1_ref, lk1_ref, lq2_ref, lk2_ref,
                      g_ref, o_ref, qq_ref, m_ref, l_ref, acc_ref):
    j = pl.program_id(1)
    hw = 2 * ATT_DH
    heads = range(ATT_HEADS)

    @pl.when(j == 0)
    def _():
        for h in heads:
            q = q_ref[0, :, h * hw:(h + 1) * hw]
            lane = lax.broadcasted_iota(jnp.int32, q.shape, 1)
            zero = jnp.zeros_like(q)
            qq_ref[h, :ts, :] = jnp.where(lane < ATT_DH, q, zero)
            qq_ref[h, ts:, :] = jnp.where(lane >= ATT_DH, q, zero)
        m_ref[...] = jnp.full(m_ref.shape, -jnp.inf, F32)
        l_ref[...] = jnp.zeros(l_ref.shape, F32)
        acc_ref[...] = jnp.zeros(acc_ref.shape, F32)

    def softmax_step(ss, vs):
        m_prev = [m_ref[h] for h in heads]
        m_new = [jnp.maximum(m, jnp.max(s, axis=-1, keepdims=True)) for m, s in zip(m_prev, ss)]
        alpha = [jnp.exp2(m - mn) for m, mn in zip(m_prev, m_new)]
        p = [jnp.exp2(s - mn) for s, mn in zip(ss, m_new)]
        pv = [_dot(x.astype(BF16), v) for x, v in zip(p, vs)]
        for h in heads:
            l_ref[h] = alpha[h] * l_ref[h] + jnp.sum(p[h], axis=-1, keepdims=True)
            acc_ref[h] = alpha[h] * acc_ref[h] + pv[h]
            m_ref[h] = m_new[h]

    tk = ckt_ref.shape[2]
    softmax_step([_dot(qq_ref[h], ckt_ref[0, h * hw:(h + 1) * hw, :].astype(BF16)) for h in heads],
                 [cv_ref[0, pl.ds(h, tk, stride=ATT_HEADS), :].astype(BF16) for h in heads])

    @pl.when(j == pl.num_programs(1) - 1)
    def _():
        lam = _lambda(lq1_ref, lk1_ref, lq2_ref, lk2_ref, lam_init)
        cols = [slice(h * hw, (h + 1) * hw) for h in heads]
        softmax_step([_dot_nt(qq_ref[h], kn_ref[0, :, cols[h]]) for h in heads],
                     [vn_ref[0, :, cols[h]] for h in heads])
        for h in heads:
            o = acc_ref[h] / l_ref[h]
            o_ref[0, :, cols[h]] = _subln(o[:ts] - lam * o[ts:], g_ref[...], lam_init).astype(o_ref.dtype)


def _attn_sample_call(q, cache_kt, cache_v, k_new, v_new, lq1, lk1, lq2, lk2, subln_g, lam_init, tk):
    b, ts, _ = q.shape
    past = cache_kt.shape[2]
    hw = 2 * ATT_DH
    new = pl.BlockSpec((1, ts, ATT_W), lambda bi, j: (bi, 0, 0))
    return pl.pallas_call(
        functools.partial(_attn_sample_body, lam_init, ts),
        grid=(b, past // tk),
        in_specs=[new, pl.BlockSpec((1, ATT_W, tk), lambda bi, j: (bi, 0, j)),
                  pl.BlockSpec((1, tk * ATT_HEADS, hw), lambda bi, j: (bi, j, 0)), new, new]
        + [_const_spec((1, ATT_DH))] * 4 + [_const_spec((1, hw))],
        out_specs=new,
        out_shape=jax.ShapeDtypeStruct((b, ts, ATT_W), BF16),
        scratch_shapes=[pltpu.VMEM((ATT_HEADS, 2 * ts, hw), BF16), pltpu.VMEM((ATT_HEADS, 2 * ts, 1), F32),
                        pltpu.VMEM((ATT_HEADS, 2 * ts, 1), F32), pltpu.VMEM((ATT_HEADS, 2 * ts, hw), F32)],
        compiler_params=_params(("parallel", "arbitrary")),
        name="attn_sample",
    )(q, cache_kt, cache_v, k_new, v_new, lq1, lk1, lq2, lk2, subln_g)


def _split2(x):
    hi = x.astype(BF16)
    return hi, (x - hi.astype(F32)).astype(BF16)


def _head_sum(x, ones_bd, passes=2):
    parts = []
    for g in range(x.shape[1] // MXU_W):
        xg = x[:, g * MXU_W:(g + 1) * MXU_W]
        if passes == 1:
            parts.append(_dot(xg.astype(BF16), ones_bd))
        else:
            hi, lo = _split2(xg)
            parts.append(_dot(hi, ones_bd) + _dot(lo, ones_bd))
    return jnp.concatenate(parts, axis=1)


def _bd_expand(x):
    c = x.shape[0]
    xt = jnp.concatenate([x] * WKV_GROUP, axis=0)
    row = lax.broadcasted_iota(jnp.int32, xt.shape, 0)
    lane = lax.broadcasted_iota(jnp.int32, xt.shape, 1)
    return jnp.where(row // c == lane // RW_HEAD, xt, jnp.zeros_like(xt))


def _wkv_chunk_terms(units):
    c = units[0][0].shape[0]
    slab = (c, WKV_GROUP * c)
    t_idx = lax.broadcasted_iota(jnp.int32, slab, 0)
    s_idx = lax.broadcasted_iota(jnp.int32, slab, 1) % c
    strict = s_idx < t_idx
    incl = s_idx <= t_idx
    eye = jnp.where(s_idx == t_idx, 1.0, 0.0)
    expand = lambda x: _bd_expand(x).astype(BF16)

    ar = [jnp.concatenate([u[0], u[1]], axis=0).astype(BF16) for u in units]
    sb = [_dot_nt(a, expand(u[2])) for a, u in zip(ar, units)]
    yield
    sk = [_dot_nt(a, expand(u[3])) for a, u in zip(ar, units)]
    yield
    l_ab = [jnp.where(strict, x[:c], 0.0) for x in sb]
    m_rb = [jnp.where(incl, x[c:], 0.0).astype(BF16) for x in sb]
    l_ak = [jnp.where(strict, x[:c], 0.0).astype(BF16) for x in sk]
    m_rk = [jnp.where(incl, x[c:], 0.0).astype(BF16) for x in sk]

    lp = l_ab
    tinv = [eye + x for x in l_ab]
    lp = [_dot(x.astype(BF16), expand(x)) for x in lp]
    yield
    power = 2
    while power < c:
        last = 2 * power >= c
        nxt = []
        for i, (x, t) in enumerate(zip(lp, tinv)):
            lhs = t if last else jnp.concatenate([x, t], axis=0)
            prod = _dot(lhs.astype(BF16), expand(x))
            if last:
                tinv[i] = t + prod
            else:
                nxt.append(prod[:c])
                tinv[i] = t + prod[c:]
        lp = nxt
        power *= 2
        yield

    v_exp = [expand(u[4]) for u in units]
    lakv = [_dot(x, ve) for x, ve in zip(l_ak, v_exp)]
    yield
    tinv_bf = [t.astype(BF16) for t in tinv]
    ta = [_dot(t, expand(u[0])) for t, u in zip(tinv_bf, units)]
    yield
    uv = [_dot(t, expand(x)) for t, x in zip(tinv_bf, lakv)]
    yield

    tr = [jnp.concatenate([x, u[1]], axis=0).astype(BF16) for x, u in zip(ta, units)]
    m_rbk = [jnp.concatenate([m, mk], axis=1) for m, mk in zip(m_rb, m_rk)]
    return list(zip(tr, uv, m_rbk, v_exp))


def _wkv_chunk_apply(states, terms, vs, bkhs, decays):
    c = vs[0].shape[0]
    gw = states[0].shape[0]
    bd = (lax.broadcasted_iota(jnp.int32, (gw, gw), 0) // RW_HEAD
          == lax.broadcasted_iota(jnp.int32, (gw, gw), 1) // RW_HEAD)
    ur = [_dot_nt(t[0], s.astype(BF16)) for t, s in zip(terms, states)]
    yield
    u = [x[:c] + t[1] for x, t in zip(ur, terms)]
    upd = [_dot_tn(jnp.concatenate([x, v], axis=0).astype(BF16), bkh) for x, v, bkh in zip(u, vs, bkhs)]
    yield
    ys = [x[c:] + _dot(t[2], jnp.concatenate([_bd_expand(uu).astype(BF16), t[3]], axis=0))
          for x, uu, t in zip(ur, u, terms)]
    yield
    return ys, [s * d + jnp.where(bd, x, 0.0) for s, d, x in zip(states, decays, upd)]


def _interleave(*gens):
    results = [None] * len(gens)
    live = list(range(len(gens)))
    while live:
        for i in list(live):
            try:
                next(gens[i])
            except StopIteration as stop:
                results[i] = stop.value
                live.remove(i)
    return results


def _rwkv_body(bb, tb, t_valid, z_ref, s0_ref, sh0_ref, mu_ref, w0_ref, wa_ref, a0_ref, g2_ref, kk_ref, ka_ref,
               rk_ref, lng_ref, lnb_ref, o_ref, s_ref, h_ref, carry_ref):
    ib = pl.program_id(1)
    c = WKV_C
    nc = tb // c
    lanes_of = lambda g: slice(g * WKV_GW, (g + 1) * WKV_GW)

    @pl.when(ib == 0)
    def _():
        for bi in range(bb):
            carry_ref[bi] = sh0_ref[bi]
            for g in range(WKV_NG):
                blocks = [s0_ref[bi, g * WKV_GROUP + h] for h in range(WKV_GROUP)]
                h_ref[bi, g] = _bd_expand(jnp.concatenate(blocks, axis=1))

    ones_bd = (lax.broadcasted_iota(jnp.int32, (MXU_W, MXU_W), 0) // RW_HEAD
               == lax.broadcasted_iota(jnp.int32, (MXU_W, MXU_W), 1) // RW_HEAD).astype(BF16)

    def prepare(seqs):
        n = len(seqs) * tb
        z = jnp.concatenate([z_ref[bi] for bi in seqs], axis=0)
        mu = mu_ref[...]
        zs = z + (pltpu.roll(z, 1, 0) - z) * mu
        first = lax.broadcasted_iota(jnp.int32, (F32_ROWS, SHIFT_W), 0) == 0
        pieces = []
        for i, bi in enumerate(seqs):
            head = z[i * tb:i * tb + F32_ROWS]
            pieces += [jnp.where(first, head + (carry_ref[bi] - head) * mu, zs[i * tb:i * tb + F32_ROWS]),
                       zs[i * tb + F32_ROWS:(i + 1) * tb]]
            carry_ref[bi] = z[(i + 1) * tb - 1:(i + 1) * tb, :]
        zs = jnp.concatenate(pieces, axis=0)
        yield

        r = zs[:, :RW_W]
        k = zs[:, RW_W:2 * RW_W]
        v = zs[:, 2 * RW_W:3 * RW_W]
        zwa = zs[:, 3 * RW_W:3 * RW_W + W_LORA + A_LORA]
        zg = zs[:, 3 * RW_W + W_LORA + A_LORA:]
        lane = lax.broadcasted_iota(jnp.int32, zwa.shape, 1)
        lora = _dot(jnp.where(lane < W_LORA, jnp.tanh(zwa), zwa).astype(BF16), wa_ref[...])
        ew = math.exp(-0.5) * jax.nn.sigmoid(w0_ref[...] + lora[:, :RW_W])
        a = jax.nn.sigmoid(a0_ref[...] + lora[:, RW_W:])
        yield
        gate = _dot(jax.nn.sigmoid(zg).astype(BF16), g2_ref[...])
        yield

        kk = k * kk_ref[...]
        kk = kk * lax.rsqrt(jnp.maximum(_head_sum(kk * kk, ones_bd), 1e-24))
        yield
        k = k * (1.0 + (a - 1.0) * ka_ref[...])
        bonus = _head_sum(r * k * rk_ref[...], ones_bd) * v
        yield

        if t_valid is not None:
            live = lax.broadcasted_iota(jnp.int32, ew.shape, 0) % tb + ib * tb < t_valid
            ew = jnp.where(live, ew, 0.0)
            k = jnp.where(live, k, 0.0)
            kk = jnp.where(live, kk, 0.0)
            v = jnp.where(live, v, 0.0)

        rows = min(n, MXU_W)
        ti = lax.broadcasted_iota(jnp.int32, (rows, rows), 0)
        si = lax.broadcasted_iota(jnp.int32, (rows, rows), 1)
        tri = jnp.logical_and(si <= ti, si // c == ti // c).astype(BF16)
        e_hi, e_lo = _split2(ew)
        cum = -jnp.concatenate([_dot(tri, e_hi[i0:i0 + rows]) + _dot(tri, e_lo[i0:i0 + rows])
                                for i0 in range(0, n, rows)], axis=0)
        cum_end = jnp.concatenate([jnp.broadcast_to(cum[i0 + c - 1:i0 + c, :], (c, RW_W))
                                   for i0 in range(0, n, c)], axis=0)
        yield
        grow = jnp.exp(-cum)
        to_end = jnp.exp(cum_end - cum)
        b = kk * a
        at = -kk * jnp.exp(cum + ew)
        rt = r * jnp.exp(cum)
        yield
        pre = dict(at=at, rt=rt, bt=b * grow, kt=k * grow, v=v, gate=gate, bonus=bonus)
        yield
        bh, kh, gamma_end = b * to_end, k * to_end, jnp.exp(cum_end)
        for ci in range(nc):
            for i in range(len(seqs)):
                sl = rows_of(ci, i)
                for g in range(WKV_NG):
                    gl = lanes_of(g)
                    pre[ci, i, g] = (v[sl, gl], jnp.concatenate([bh[sl, gl], kh[sl, gl]], axis=0).astype(BF16),
                                     gamma_end[sl, gl][:1])
        return pre

    rows_of = lambda ci, i: slice(i * tb + ci * c, i * tb + (ci + 1) * c)

    def chunk_terms(pre, nseq):
        order = [(ci, i, g) for ci in range(nc) for i in range(nseq) for g in range(WKV_NG)]
        units = [tuple(pre[name][rows_of(ci, i), lanes_of(g)] for name in ("at", "rt", "bt", "kt", "v"))
                 for ci, i, g in order]
        return dict(zip(order, (yield from _wkv_chunk_terms(units))))

    def apply_chunks(seqs, pre, terms):
        chains = [(i, g) for i in range(len(seqs)) for g in range(WKV_NG)]
        states = [h_ref[seqs[i], g] for i, g in chains]
        y_parts = {}
        for ci in range(nc):
            extra = [pre[ci, i, g] for i, g in chains]
            ys, states = yield from _wkv_chunk_apply(states, [terms[ci, i, g] for i, g in chains],
                                                     [e[0] for e in extra], [e[1] for e in extra],
                                                     [e[2] for e in extra])
            for (i, g), y_unit in zip(chains, ys):
                y_parts[ci, i, g] = y_unit
        for (i, g), s_out in zip(chains, states):
            h_ref[seqs[i], g] = s_out
        return jnp.concatenate([jnp.concatenate([y_parts[ci, i, g] for g in range(WKV_NG)], axis=1)
                                for i in range(len(seqs)) for ci in range(nc)], axis=0)

    def finish(seqs, pre, y):
        mean = _head_sum(y, ones_bd) * (1.0 / RW_HEAD)
        yield
        d = y - mean
        var = _head_sum(d * d, ones_bd) * (1.0 / RW_HEAD)
        yield
        yn = d * lax.rsqrt(var + LNX_EPS) * lng_ref[...] + lnb_ref[...]
        out = ((yn + pre["bonus"]) * pre["gate"]).astype(o_ref.dtype)
        for i, bi in enumerate(seqs):
            o_ref[bi] = out[i * tb:(i + 1) * tb]

    halves = [list(range(bb))] if bb == 1 else [list(range(bb // 2)), list(range(bb // 2, bb))]
    first, second = halves[0], halves[-1]
    pre_a, = _interleave(prepare(first))
    if len(halves) == 1:
        terms_a, = _interleave(chunk_terms(pre_a, len(first)))
        y_a, = _interleave(apply_chunks(first, pre_a, terms_a))
        _interleave(finish(first, pre_a, y_a))
    else:
        terms_a, pre_b = _interleave(chunk_terms(pre_a, len(first)), prepare(second))
        y_a, terms_b = _interleave(apply_chunks(first, pre_a, terms_a), chunk_terms(pre_b, len(second)))
        y_b, _ = _interleave(apply_chunks(second, pre_b, terms_b), finish(first, pre_a, y_a))
        _interleave(finish(second, pre_b, y_b))

    @pl.when(ib == pl.num_programs(1) - 1)
    def _():
        for bi in range(bb):
            for g in range(WKV_NG):
                h = h_ref[bi, g]
                for hh in range(WKV_GROUP):
                    blk = h[hh * RW_HEAD:(hh + 1) * RW_HEAD, hh * RW_HEAD:(hh + 1) * RW_HEAD]
                    s_ref[bi, g * WKV_GROUP + hh] = blk


def _rwkv_call(z_rw, s0, shift0, p, bb, tb, t_valid):
    b, t, _ = z_rw.shape
    vec = lambda w: _const_spec((1, w))
    return pl.pallas_call(
        functools.partial(_rwkv_body, bb, tb, t_valid),
        grid=(b // bb, t // tb),
        in_specs=[pl.BlockSpec((bb, tb, SHIFT_W), lambda bi, i: (bi, i, 0)),
                  pl.BlockSpec((bb, RW_HEADS, RW_HEAD, RW_HEAD), lambda bi, i: (bi, 0, 0, 0)),
                  pl.BlockSpec((bb, 1, SHIFT_W), lambda bi, i: (bi, 0, 0)),
                  vec(SHIFT_W), vec(RW_W), _const_spec((W_LORA + A_LORA, 2 * RW_W)), vec(RW_W),
                  _const_spec((G_LORA, RW_W)), vec(RW_W), vec(RW_W), vec(RW_W), vec(RW_W), vec(RW_W)],
        out_specs=[pl.BlockSpec((bb, tb, RW_W), lambda bi, i: (bi, i, 0)),
                   pl.BlockSpec((bb, RW_HEADS, RW_HEAD, RW_HEAD), lambda bi, i: (bi, 0, 0, 0))],
        out_shape=[jax.ShapeDtypeStruct((b, t, RW_W), BF16),
                   jax.ShapeDtypeStruct((b, RW_HEADS, RW_HEAD, RW_HEAD), F32)],
        scratch_shapes=[pltpu.VMEM((bb, WKV_NG, WKV_GW, WKV_GW), F32), pltpu.VMEM((bb, 1, SHIFT_W), F32)],
        compiler_params=_params(("parallel", "arbitrary")),
        name="rwkv",
    )(z_rw, s0, shift0, p["mu"], p["w0"], p["wa"], p["a0"], p["g2"], p["k_k"], p["k_a"], p["r_k"],
      p["lnx_g"], p["lnx_b"])


def _merge_body(h_ref, oa_ref, orw_ref, gate_ref, woa_ref, worw_ref, wout_ref, postg_ref, pre_ref, win_ref,
                wo2_ref, post2_ref, y_ref):
    gates = gate_ref[...]
    merged = (gates[:, :D_MODEL] * _dot(oa_ref[...], woa_ref[...])
              + gates[:, D_MODEL:] * _dot(orw_ref[...], worw_ref[...]))
    h2 = h_ref[...] + _rms(_dot(merged.astype(BF16), wout_ref[...]), postg_ref[...])
    f = _swiglu_ffn(_rms(h2, pre_ref[...]).astype(BF16), win_ref, wo2_ref)
    y_ref[...] = h2 + MACARON * _rms(f, post2_ref[...])


def _merge_call(h, o_att, o_rw, gates, w_o_att, w_o_rwkv, w_out, mix_post_g, pre_g, w_in, w_out2, post_g, tm):
    m = h.shape[0]
    row = lambda w: pl.BlockSpec((tm, w), lambda i: (i, 0))
    vec = _const_spec((1, D_MODEL))
    return pl.pallas_call(
        _merge_body,
        grid=(m // tm,),
        in_specs=[row(D_MODEL), row(ATT_W), row(RW_W), row(GATE_W),
                  _const_spec((ATT_W, D_MODEL)), _const_spec((RW_W, D_MODEL)), _const_spec((D_MODEL, D_MODEL)), vec,
                  vec, _const_spec((D_MODEL, 2 * D_FF)), _const_spec((D_FF, D_MODEL)), vec],
        out_specs=row(D_MODEL),
        out_shape=jax.ShapeDtypeStruct((m, D_MODEL), F32),
        compiler_params=_params(("parallel",)),
        name="merge_ffn",
    )(h, o_att, o_rw, gates, w_o_att, w_o_rwkv, w_out, mix_post_g, pre_g, w_in, w_out2, post_g)


def _pick_tile(m, want):
    t = min(m, want)
    assert m % t == 0
    return t


def _layer(x, pos, l, p, cache_k, cache_v, s0, shift0):
    b, t, _ = x.shape
    m = b * t
    tm = _pick_tile(m, 512)
    lam_init = 0.8 - 0.6 * math.exp(-0.3 * l)

    late = p.get("late_f32")
    h = _ffn_call(x.reshape(m, D_MODEL), p["ffn1_pre_g"], p["ffn1_w_in"], p["ffn1_w_out"], p["ffn1_post_g"],
                  _pick_tile(m, 512), cast=tuple(late.values()) if late else ())
    if late:
        h, casted = h
        p = {**p, **dict(zip(late, casted)), "late_f32": None}

    cos_t, sin_t = _rope_tables(pos)
    tmp = _pick_tile(m, 512)
    if t < tmp:
        cos_t, sin_t = jnp.tile(cos_t, (tmp // t, 1)), jnp.tile(sin_t, (tmp // t, 1))
    q, k, k_bf, v, v_bf, z_rw, gates = _mixproj_call(h, p["mix_pre_g"], p["w_in"], cos_t, sin_t, tmp,
                                                     t if cache_k is None else None)

    lam_args = (p["att_lambda_q1"], p["att_lambda_k1"], p["att_lambda_q2"], p["att_lambda_k2"], p["att_subln_g"])
    r3 = lambda a: a.reshape(b, t, a.shape[-1])
    if cache_k is None:
        o_att = _attn_prompt_call(r3(q), r3(k_bf), v_bf, *lam_args, lam_init, _pick_tile(t, 512))
    else:
        past = cache_k.shape[1]
        cache_kt = cache_k.reshape(b, past, ATT_W).transpose(0, 2, 1)
        cache_vr = cache_v.reshape(b, past * ATT_HEADS, 2 * ATT_DH)
        o_att = _attn_sample_call(r3(q), cache_kt, cache_vr, r3(k_bf), r3(v_bf), *lam_args, lam_init,
                                  _pick_tile(past, 2048))

    z_rw3 = r3(z_rw)
    t_pad = -(-t // WKV_C) * WKV_C
    z_in = z_rw3 if t_pad == t else jnp.pad(z_rw3, ((0, 0), (0, t_pad - t), (0, 0)))
    o_rw, s_new = _rwkv_call(z_in, s0, shift0, p["rwkv"], _pick_tile(b, WKV_SEQS), _pick_tile(t_pad, WKV_TB),
                             None if t_pad == t else t)
    o_rw = o_rw[:, :t]

    y = _merge_call(h, o_att.reshape(m, ATT_W), o_rw.reshape(m, RW_W), gates, p["w_o_att"], p["w_o_rwkv"],
                    p["w_out"], p["mix_post_g"], p["ffn2_pre_g"], p["ffn2_w_in"], p["ffn2_w_out"],
                    p["ffn2_post_g"], tm)
    if cache_k is None:
        k_rows = k.reshape(b, ATT_HEADS, 2, ATT_DH, t).transpose(0, 4, 1, 2, 3)
    else:
        k_rows = k.reshape(b, t, ATT_HEADS, 2, ATT_DH)
    return (y.reshape(b, t, D_MODEL), k_rows, v.reshape(b, t, ATT_HEADS, 2 * ATT_DH), s_new, z_rw3[:, -1:]), p


def kernel(x_prompt, x_sample, cache_att_k, cache_att_v, state_rwkv, state_shift, ffn1_pre_g, ffn1_w_in, ffn1_w_out, ffn1_post_g, mix_pre_g, w_in, att_lambda_q1, att_lambda_k1, att_lambda_q2, att_lambda_k2, att_subln_g, rwkv_mu, rwkv_w0, rwkv_w2, rwkv_a0, rwkv_a2, rwkv_g2, rwkv_k_k, rwkv_k_a, rwkv_r_k, rwkv_lnx_g, rwkv_lnx_b, w_o_att, w_o_rwkv, w_out, mix_post_g, ffn2_pre_g, ffn2_w_in, ffn2_w_out, ffn2_post_g):
    depth = w_in.shape[0]
    bp, tp, _ = x_prompt.shape
    bs, ts, _ = x_sample.shape
    past = cache_att_k.shape[2]
    pos_p = jnp.arange(tp)
    pos_s = past + jnp.arange(ts)
    xp, xs = x_prompt, x_sample
    outs_p, outs_s = [], []
    vec = lambda a: a.reshape(1, -1)
    for l in range(depth):
        zeros = jnp.zeros((W_LORA, RW_W), F32)
        wa = jnp.concatenate([jnp.concatenate([rwkv_w2[l], zeros], axis=1),
                              jnp.concatenate([zeros, rwkv_a2[l]], axis=1)], axis=0)
        p = dict(
            ffn1_pre_g=vec(ffn1_pre_g[l]), ffn1_w_in=ffn1_w_in[l].astype(BF16), ffn1_w_out=ffn1_w_out[l].astype(BF16),
            ffn1_post_g=vec(ffn1_post_g[l]), mix_pre_g=vec(mix_pre_g[l]),
            late_f32=dict(w_in=w_in[l], w_o_att=w_o_att[l], w_o_rwkv=w_o_rwkv[l], w_out=w_out[l],
                          ffn2_w_in=ffn2_w_in[l], ffn2_w_out=ffn2_w_out[l]),
            att_lambda_q1=vec(att_lambda_q1[l]), att_lambda_k1=vec(att_lambda_k1[l]),
            att_lambda_q2=vec(att_lambda_q2[l]), att_lambda_k2=vec(att_lambda_k2[l]),
            att_subln_g=vec(att_subln_g[l]),
            rwkv=dict(mu=vec(rwkv_mu[l]), w0=vec(rwkv_w0[l]), wa=wa.astype(BF16), a0=vec(rwkv_a0[l]),
                      g2=rwkv_g2[l].astype(BF16), k_k=vec(rwkv_k_k[l]), k_a=vec(rwkv_k_a[l]), r_k=vec(rwkv_r_k[l]),
                      lnx_g=vec(rwkv_lnx_g[l]), lnx_b=vec(rwkv_lnx_b[l])),
            mix_post_g=vec(mix_post_g[l]), ffn2_pre_g=vec(ffn2_pre_g[l]), ffn2_post_g=vec(ffn2_post_g[l]),
        )
        s0p = jnp.zeros((bp, RW_HEADS, RW_HEAD, RW_HEAD), F32)
        sh0p = jnp.zeros((bp, 1, SHIFT_W), F32)
        (xp, *rest_p), p = _layer(xp, pos_p, l, p, None, None, s0p, sh0p)
        (xs, *rest_s), p = _layer(xs, pos_s, l, p, cache_att_k[l], cache_att_v[l], state_rwkv[l], state_shift[l])
        outs_p.append(rest_p)
        outs_s.append(rest_s)
    stack = lambda outs, i: jnp.stack([o[i] for o in outs], 0)
    return (xp, xs, stack(outs_p, 0), stack(outs_p, 1), stack(outs_p, 2), stack(outs_p, 3),
            stack(outs_s, 0), stack(outs_s, 1), stack(outs_s, 2), stack(outs_s, 3))
```

```python
import functools
import math

import jax
import jax.numpy as jnp
from jax import lax
from jax.experimental import pallas as pl
from jax.experimental.pallas import tpu as pltpu

F32 = jnp.float32
BF16 = jnp.bfloat16

D_MODEL = 1024
D_FF = 2816
CHUNK = 64
ROPE_THETA = 10000.0
EPS = 1e-6
MACARON = 0.5
ATT_HEADS = 4
ATT_DH = 64
ATT_W = ATT_HEADS * 2 * ATT_DH
ATT_HPS = 2
Q_SCALE = ATT_DH ** -0.5 * math.log2(math.e)
RW_HEAD = 64
RW_W = D_MODEL // 2
RW_HEADS = RW_W // RW_HEAD
W_LORA = 64
A_LORA = 64
G_LORA = 128
SHIFT_W = 3 * RW_W + W_LORA + A_LORA + G_LORA
LNX_EPS = 64e-5
GATE_W = 2 * D_MODEL
IN_W = 3 * ATT_W + SHIFT_W + GATE_W

LANES = 128
F32_ROWS = 8
BF16_ROWS = 16
MXU_W = 256
VMEM_LIMIT = 56 * 1024 * 1024

WKV_C = 64
WKV_GROUP = MXU_W // RW_HEAD
WKV_GW = WKV_GROUP * RW_HEAD
WKV_NG = RW_W // WKV_GW
WKV_SEQS = 4
WKV_TB = 256


def _params(sem):
    return pltpu.CompilerParams(dimension_semantics=sem, vmem_limit_bytes=VMEM_LIMIT)


def _const_spec(shape):
    nd = len(shape)
    return pl.BlockSpec(shape, lambda *_: (0,) * nd, pipeline_mode=pl.Buffered(1))


def _rms(x, g):
    return x * lax.rsqrt(jnp.mean(x * x, axis=-1, keepdims=True) + EPS) * g


def _dot(a, b):
    return jnp.dot(a, b, preferred_element_type=F32)


def _dot_nt(a, b):
    return lax.dot_general(a, b, (((1,), (1,)), ((), ())), preferred_element_type=F32)


def _dot_tn(a, b):
    return lax.dot_general(a, b, (((0,), (0,)), ((), ())), preferred_element_type=F32)


def _swiglu_ffn(xn_bf, w_in_ref, w_out_ref):
    hh = _dot(xn_bf, w_in_ref[...])
    gate = hh[:, :D_FF]
    up = hh[:, D_FF:]
    act = (gate * jax.nn.sigmoid(gate) * up).astype(BF16)
    return _dot(act, w_out_ref[...])


def _ffn_body(cast_steps, x_ref, pre_ref, win_ref, wout_ref, post_ref, *refs):
    ncast = len(cast_steps)
    cast_in, o_ref, cast_out = refs[:ncast], refs[ncast], refs[ncast + 1:]
    x = x_ref[...]
    f = _swiglu_ffn(_rms(x, pre_ref[...]).astype(BF16), win_ref, wout_ref)
    o_ref[...] = x + MACARON * _rms(f, post_ref[...])
    for w_ref, c_ref, steps in zip(cast_in, cast_out, cast_steps):
        @pl.when(pl.program_id(0) < steps)
        def _():
            c_ref[...] = w_ref[...].astype(BF16)


def _cast_rows(rows, nsteps):
    return next(r for r in range(BF16_ROWS, rows + 1, BF16_ROWS) if rows % r == 0 and rows // r <= nsteps)


def _ffn_call(x, pre_g, w_in, w_out, post_g, tm, cast=()):
    m = x.shape[0]
    nsteps = m // tm
    row = lambda w: pl.BlockSpec((tm, w), lambda i: (i, 0))
    slabs = [_cast_rows(w.shape[0], nsteps) for w in cast]
    steps = tuple(w.shape[0] // r for w, r in zip(cast, slabs))
    slab_index = lambda i, last: (jnp.minimum(i, last), 0)
    cast_specs = [pl.BlockSpec((r, w.shape[1]), functools.partial(slab_index, last=s - 1))
                  for w, r, s in zip(cast, slabs, steps)]
    out = pl.pallas_call(
        functools.partial(_ffn_body, steps),
        grid=(nsteps,),
        in_specs=[row(D_MODEL), _const_spec((1, D_MODEL)), _const_spec((D_MODEL, 2 * D_FF)),
                  _const_spec((D_FF, D_MODEL)), _const_spec((1, D_MODEL))] + cast_specs,
        out_specs=[row(D_MODEL)] + cast_specs,
        out_shape=[jax.ShapeDtypeStruct((m, D_MODEL), F32)] + [jax.ShapeDtypeStruct(w.shape, BF16) for w in cast],
        compiler_params=_params(("arbitrary",) if cast else ("parallel",)),
        name="ffn",
    )(x, pre_g, w_in, w_out, post_g, *cast)
    return (out[0], out[1:]) if cast else out[0]


def _rope_tables(pos):
    half = ATT_DH // 2
    inv = ROPE_THETA ** (-jnp.arange(half, dtype=F32) / half)
    ang = pos.astype(F32)[:, None] * inv[None, :]
    cos, sin = jnp.cos(ang), jnp.sin(ang)
    cos_t = jnp.tile(jnp.concatenate([cos, cos], axis=-1), (1, LANES // ATT_DH))
    sin_t = jnp.tile(jnp.concatenate([-sin, sin], axis=-1), (1, LANES // ATT_DH))
    return cos_t, sin_t


def _rope(x, cos_t, sin_t):
    n = x.shape[-1]
    half = ATT_DH // 2
    lane = lax.broadcasted_iota(jnp.int32, x.shape, 1)
    swapped = jnp.where(lane % ATT_DH < half, pltpu.roll(x, n - half, 1), pltpu.roll(x, half, 1))
    return x * cos_t + swapped * sin_t


def _mixproj_body(v_cols, h_ref, g_ref, w_ref, cos_ref, sin_ref, q_ref, k_ref, kb_ref, v_ref, vb_ref, rw_ref,
                  gate_ref):
    u = _rms(h_ref[...], g_ref[...]).astype(BF16)
    z = _dot(u, w_ref[...])
    cos_t = jnp.tile(cos_ref[...], (1, ATT_W // LANES))
    sin_t = jnp.tile(sin_ref[...], (1, ATT_W // LANES))
    q = _rope(z[:, :ATT_W], cos_t, sin_t) * Q_SCALE
    if v_cols:
        q_ref[0] = q.T.astype(BF16)
    else:
        q_ref[...] = q.astype(BF16)
    k = _rope(z[:, ATT_W:2 * ATT_W], cos_t, sin_t)
    kb_ref[...] = k.astype(BF16)
    v = z[:, 2 * ATT_W:3 * ATT_W]
    if v_cols:
        k_ref[0] = k.T
        hw = 2 * ATT_DH
        for h in range(ATT_HEADS):
            v_ref[pl.ds(h, v.shape[0], stride=ATT_HEADS), :] = v[:, h * hw:(h + 1) * hw]
        vb_ref[0] = v.T.astype(BF16)
    else:
        k_ref[...] = k
        v_ref[...] = v
        vb_ref[...] = v.astype(BF16)
    rw_ref[...] = z[:, 3 * ATT_W:3 * ATT_W + SHIFT_W]
    gate_ref[...] = jax.nn.sigmoid(z[:, 3 * ATT_W + SHIFT_W:]).astype(BF16)


def _mixproj_call(h, mix_pre_g, w_in, cos_t, sin_t, tm, v_cols_t):
    m = h.shape[0]
    ntab = cos_t.shape[0] // tm
    row = lambda w: pl.BlockSpec((tm, w), lambda i: (i, 0))
    tab = pl.BlockSpec((tm, LANES), lambda i: (i % ntab, 0))
    sds = lambda w, dt: jax.ShapeDtypeStruct((m, w), dt)
    if v_cols_t is None:
        q_spec, q_shape = row(ATT_W), sds(ATT_W, BF16)
        k_spec, k_shape = row(ATT_W), sds(ATT_W, F32)
        v_spec, v_shape = row(ATT_W), sds(ATT_W, F32)
        vb_spec, vb_shape = row(ATT_W), sds(ATT_W, BF16)
    else:
        nt = v_cols_t // tm
        q_spec = k_spec = vb_spec = pl.BlockSpec((1, ATT_W, tm), lambda i: (i // nt, 0, i % nt))
        k_shape = jax.ShapeDtypeStruct((m // v_cols_t, ATT_W, v_cols_t), F32)
        q_shape = vb_shape = jax.ShapeDtypeStruct((m // v_cols_t, ATT_W, v_cols_t), BF16)
        v_spec = pl.BlockSpec((tm * ATT_HEADS, 2 * ATT_DH), lambda i: (i, 0))
        v_shape = jax.ShapeDtypeStruct((m * ATT_HEADS, 2 * ATT_DH), F32)
    return pl.pallas_call(
        functools.partial(_mixproj_body, v_cols_t is not None),
        grid=(m // tm,),
        in_specs=[row(D_MODEL), _const_spec((1, D_MODEL)), _const_spec((D_MODEL, IN_W)), tab, tab],
        out_specs=[q_spec, k_spec, row(ATT_W), v_spec, vb_spec, row(SHIFT_W), row(GATE_W)],
        out_shape=[q_shape, k_shape, sds(ATT_W, BF16), v_shape, vb_shape,
                   sds(SHIFT_W, F32), sds(GATE_W, BF16)],
        compiler_params=_params(("parallel",)),
        name="mixproj",
    )(h, mix_pre_g, w_in, cos_t, sin_t)


def _lambda(lq1_ref, lk1_ref, lq2_ref, lk2_ref, lam_init):
    s1 = jnp.sum(lq1_ref[...] * lk1_ref[...], axis=-1, keepdims=True)
    s2 = jnp.sum(lq2_ref[...] * lk2_ref[...], axis=-1, keepdims=True)
    return jnp.exp(s1) - jnp.exp(s2) + lam_init


def _subln(o, g, lam_init):
    return o * lax.rsqrt(jnp.mean(o * o, axis=-1, keepdims=True) + EPS) * g * (1.0 - lam_init)


def _attn_prompt_body(lam_init, blk, qt_ref, k_ref, vt_ref, lq1_ref, lk1_ref, lq2_ref, lk2_ref, g_ref, o_ref,
                      qq_ref, m_ref, l_ref, acc_ref, st_ref):
    i = pl.program_id(2)
    hw = 2 * ATT_DH
    heads = range(ATT_HPS)
    for h in heads:
        qt = qt_ref[0, h * hw:(h + 1) * hw, :]
        feat = lax.broadcasted_iota(jnp.int32, qt.shape, 0)
        zero = jnp.zeros_like(qt)
        qq_ref[h, :, :blk] = jnp.where(feat < ATT_DH, qt, zero)
        qq_ref[h, :, blk:] = jnp.where(feat >= ATT_DH, qt, zero)
    m_ref[...] = jnp.full(m_ref.shape, -jnp.inf, F32)
    l_ref[...] = jnp.zeros(l_ref.shape, F32)
    acc_ref[...] = jnp.zeros(acc_ref.shape, F32)

    def scores(h, j):
        start = pl.multiple_of(j * blk, blk)
        st_ref[h] = _dot(k_ref[0, pl.ds(start, blk), h * hw:(h + 1) * hw], qq_ref[h])

    ones = jnp.ones((BF16_ROWS, blk), BF16)

    def consume(h, j, diagonal):
        start = pl.multiple_of(j * blk, blk)
        st = st_ref[h]
        if diagonal:
            key = lax.broadcasted_iota(jnp.int32, st.shape, 0)
            qry = lax.broadcasted_iota(jnp.int32, st.shape, 1) % blk
            st = jnp.where(key // CHUNK <= qry // CHUNK, st, -jnp.inf)
        m_prev = m_ref[h]
        m_new = jnp.maximum(m_prev, jnp.max(st, axis=0, keepdims=True))
        alpha = jnp.exp2(m_prev - m_new)
        p = jnp.exp2(st - m_new).astype(BF16)
        lhs = jnp.concatenate([vt_ref[0, h * hw:(h + 1) * hw, pl.ds(start, blk)], ones], axis=0)
        pv = _dot(lhs, p)
        l_ref[h] = alpha * l_ref[h] + pv[hw:hw + 1]
        acc_ref[h] = alpha * acc_ref[h] + pv[:hw]
        m_ref[h] = m_new

    def block(j, carry):
        scores(1, j)
        consume(0, j, False)
        scores(0, j + 1)
        consume(1, j, False)
        return carry

    scores(0, 0)
    lax.fori_loop(0, i, block, 0)
    scores(1, i)
    consume(0, i, True)
    consume(1, i, True)

    lam = _lambda(lq1_ref, lk1_ref, lq2_ref, lk2_ref, lam_init)
    for h in heads:
        o = acc_ref[h] / l_ref[h]
        o = o[:, :blk] - lam * o[:, blk:]
        o = o * lax.rsqrt(jnp.mean(o * o, axis=0, keepdims=True) + EPS) * g_ref[...] * (1.0 - lam_init)
        o_ref[0, :, h * hw:(h + 1) * hw] = o.T.astype(o_ref.dtype)


def _attn_prompt_call(qt, k, vt, lq1, lk1, lq2, lk2, subln_g, lam_init, blk):
    b, t, _ = k.shape
    hw = 2 * ATT_DH
    gw = ATT_HPS * hw
    qtspec = pl.BlockSpec((1, gw, blk), lambda bi, hi, i: (bi, hi, i))
    kspec = pl.BlockSpec((1, t, gw), lambda bi, hi, i: (bi, 0, hi))
    vtspec = pl.BlockSpec((1, gw, t), lambda bi, hi, i: (bi, hi, 0))
    return pl.pallas_call(
        functools.partial(_attn_prompt_body, lam_init, blk),
        grid=(b, ATT_HEADS // ATT_HPS, t // blk),
        in_specs=[qtspec, kspec, vtspec] + [_const_spec((1, ATT_DH))] * 4 + [_const_spec((hw, 1))],
        out_specs=pl.BlockSpec((1, blk, gw), lambda bi, hi, i: (bi, i, hi)),
        out_shape=jax.ShapeDtypeStruct((b, t, ATT_W), BF16),
        scratch_shapes=[pltpu.VMEM((ATT_HPS, hw, 2 * blk), BF16), pltpu.VMEM((ATT_HPS, 1, 2 * blk), F32),
                        pltpu.VMEM((ATT_HPS, 1, 2 * blk), F32), pltpu.VMEM((ATT_HPS, hw, 2 * blk), F32),
                        pltpu.VMEM((ATT_HPS, blk, 2 * blk), F32)],
        compiler_params=_params(("parallel", "parallel", "arbitrary")),
        name="attn_prompt",
    )(qt, k, vt, lq1, lk1, lq2, lk2, subln_g.reshape(hw, 1))


def _attn_sample_body(lam_init, ts, q_ref, ckt_ref, cv_ref, kn_ref, vn_ref, lq1_ref, lk1_ref, lq2_ref, lk2_ref,
                      g_ref, o_ref, qq_ref, m_ref, l_ref, acc_ref):
    j = pl.program_id(1)
    hw = 2 * ATT_DH
    heads = range(ATT_HEADS)

    @pl.when(j == 0)
    def _():
        for h in heads:
            q = q_ref[0, :, h * hw:(h + 1) * hw]
            lane = lax.broadcasted_iota(jnp.int32, q.shape, 1)
            zero = jnp.zeros_like(q)
            qq_ref[h, :ts, :] = jnp.where(lane < ATT_DH, q, zero)
            qq_ref[h, ts:, :] = jnp.where(lane >= ATT_DH, q, zero)
        m_ref[...] = jnp.full(m_ref.shape, -jnp.inf, F32)
        l_ref[...] = jnp.zeros(l_ref.shape, F32)
        acc_ref[...] = jnp.zeros(acc_ref.shape, F32)

    def softmax_step(ss, vs):
        m_prev = [m_ref[h] for h in heads]
        m_new = [jnp.maximum(m, jnp.max(s, axis=-1, keepdims=True)) for m, s in zip(m_prev, ss)]
        alpha = [jnp.exp2(m - mn) for m, mn in zip(m_prev, m_new)]
        p = [jnp.exp2(s - mn) for s, mn in zip(ss, m_new)]
        pv = [_dot(x.astype(BF16), v) for x, v in zip(p, vs)]
        for h in heads:
            l_ref[h] = alpha[h] * l_ref[h] + jnp.sum(p[h], axis=-1, keepdims=True)
            acc_ref[h] = alpha[h] * acc_ref[h] + pv[h]
            m_ref[h] = m_new[h]

    tk = ckt_ref.shape[2]
    softmax_step([_dot(qq_ref[h], ckt_ref[0, h * hw:(h + 1) * hw, :].astype(BF16)) for h in heads],
                 [cv_ref[0, pl.ds(h, tk, stride=ATT_HEADS), :].astype(BF16) for h in heads])

    @pl.when(j == pl.num_programs(1) - 1)
    def _():
        lam = _lambda(lq1_ref, lk1_ref, lq2_ref, lk2_ref, lam_init)
        cols = [slice(h * hw, (h + 1) * hw) for h in heads]
        softmax_step([_dot_nt(qq_ref[h], kn_ref[0, :, cols[h]]) for h in heads],
                     [vn_ref[0, :, cols[h]] for h in heads])
        for h in heads:
            o = acc_ref[h] / l_ref[h]
            o_ref[0, :, cols[h]] = _subln(o[:ts] - lam * o[ts:], g_ref[...], lam_init).astype(o_ref.dtype)


def _attn_sample_call(q, cache_kt, cache_v, k_new, v_new, lq1, lk1, lq2, lk2, subln_g, lam_init, tk):
    b, ts, _ = q.shape
    past = cache_kt.shape[2]
    hw = 2 * ATT_DH
    new = pl.BlockSpec((1, ts, ATT_W), lambda bi, j: (bi, 0, 0))
    return pl.pallas_call(
        functools.partial(_attn_sample_body, lam_init, ts),
        grid=(b, past // tk),
        in_specs=[new, pl.BlockSpec((1, ATT_W, tk), lambda bi, j: (bi, 0, j)),
                  pl.BlockSpec((1, tk * ATT_HEADS, hw), lambda bi, j: (bi, j, 0)), new, new]
        + [_const_spec((1, ATT_DH))] * 4 + [_const_spec((1, hw))],
        out_specs=new,
        out_shape=jax.ShapeDtypeStruct((b, ts, ATT_W), BF16),
        scratch_shapes=[pltpu.VMEM((ATT_HEADS, 2 * ts, hw), BF16), pltpu.VMEM((ATT_HEADS, 2 * ts, 1), F32),
                        pltpu.VMEM((ATT_HEADS, 2 * ts, 1), F32), pltpu.VMEM((ATT_HEADS, 2 * ts, hw), F32)],
        compiler_params=_params(("parallel", "arbitrary")),
        name="attn_sample",
    )(q, cache_kt, cache_v, k_new, v_new, lq1, lk1, lq2, lk2, subln_g)


def _split2(x):
    hi = x.astype(BF16)
    return hi, (x - hi.astype(F32)).astype(BF16)


def _head_sum(x, ones_bd, passes=2):
    parts = []
    for g in range(x.shape[1] // MXU_W):
        xg = x[:, g * MXU_W:(g + 1) * MXU_W]
        if passes == 1:
            parts.append(_dot(xg.astype(BF16), ones_bd))
        else:
            hi, lo = _split2(xg)
            parts.append(_dot(hi, ones_bd) + _dot(lo, ones_bd))
    return jnp.concatenate(parts, axis=1)


def _bd_expand(x):
    c = x.shape[0]
    xt = jnp.concatenate([x] * WKV_GROUP, axis=0)
    row = lax.broadcasted_iota(jnp.int32, xt.shape, 0)
    lane = lax.broadcasted_iota(jnp.int32, xt.shape, 1)
    return jnp.where(row // c == lane // RW_HEAD, xt, jnp.zeros_like(xt))


def _wkv_chunk_terms(units):
    c = units[0][0].shape[0]
    slab = (c, WKV_GROUP * c)
    t_idx = lax.broadcasted_iota(jnp.int32, slab, 0)
    s_idx = lax.broadcasted_iota(jnp.int32, slab, 1) % c
    strict = s_idx < t_idx
    incl = s_idx <= t_idx
    eye = jnp.where(s_idx == t_idx, 1.0, 0.0)
    expand = lambda x: _bd_expand(x).astype(BF16)

    ar = [jnp.concatenate([u[0], u[1]], axis=0).astype(BF16) for u in units]
    sb = [_dot_nt(a, expand(u[2])) for a, u in zip(ar, units)]
    yield
    sk = [_dot_nt(a, expand(u[3])) for a, u in zip(ar, units)]
    yield
    l_ab = [jnp.where(strict, x[:c], 0.0) for x in sb]
    m_rb = [jnp.where(incl, x[c:], 0.0).astype(BF16) for x in sb]
    l_ak = [jnp.where(strict, x[:c], 0.0).astype(BF16) for x in sk]
    m_rk = [jnp.where(incl, x[c:], 0.0).astype(BF16) for x in sk]

    lp = l_ab
    tinv = [eye + x for x in l_ab]
    lp = [_dot(x.astype(BF16), expand(x)) for x in lp]
    yield
    power = 2
    while power < c:
        last = 2 * power >= c
        nxt = []
        for i, (x, t) in enumerate(zip(lp, tinv)):
            lhs = t if last else jnp.concatenate([x, t], axis=0)
            prod = _dot(lhs.astype(BF16), expand(x))
            if last:
                tinv[i] = t + prod
            else:
                nxt.append(prod[:c])
                tinv[i] = t + prod[c:]
        lp = nxt
        power *= 2
        yield

    v_exp = [expand(u[4]) for u in units]
    lakv = [_dot(x, ve) for x, ve in zip(l_ak, v_exp)]
    yield
    tinv_bf = [t.astype(BF16) for t in tinv]
    ta = [_dot(t, expand(u[0])) for t, u in zip(tinv_bf, units)]
    yield
    uv = [_dot(t, expand(x)) for t, x in zip(tinv_bf, lakv)]
    yield

    tr = [jnp.concatenate([x, u[1]], axis=0).astype(BF16) for x, u in zip(ta, units)]
    m_rbk = [jnp.concatenate([m, mk], axis=1) for m, mk in zip(m_rb, m_rk)]
    return list(zip(tr, uv, m_rbk, v_exp))


def _wkv_chunk_apply(states, terms, vs, bkhs, decays):
    c = vs[0].shape[0]
    gw = states[0].shape[0]
    bd = (lax.broadcasted_iota(jnp.int32, (gw, gw), 0) // RW_HEAD
          == lax.broadcasted_iota(jnp.int32, (gw, gw), 1) // RW_HEAD)
    ur = [_dot_nt(t[0], s.astype(BF16)) for t, s in zip(terms, states)]
    yield
    u = [x[:c] + t[1] for x, t in zip(ur, terms)]
    upd = [_dot_tn(jnp.concatenate([x, v], axis=0).astype(BF16), bkh) for x, v, bkh in zip(u, vs, bkhs)]
    yield
    ys = [x[c:] + _dot(t[2], jnp.concatenate([_bd_expand(uu).astype(BF16), t[3]], axis=0))
          for x, uu, t in zip(ur, u, terms)]
    yield
    return ys, [s * d + jnp.where(bd, x, 0.0) for s, d, x in zip(states, decays, upd)]


def _interleave(*gens):
    results = [None] * len(gens)
    live = list(range(len(gens)))
    while live:
        for i in list(live):
            try:
                next(gens[i])
            except StopIteration as stop:
                results[i] = stop.value
                live.remove(i)
    return results


def _rwkv_body(bb, tb, t_valid, z_ref, s0_ref, sh0_ref, mu_ref, w0_ref, wa_ref, a0_ref, g2_ref, kk_ref, ka_ref,
               rk_ref, lng_ref, lnb_ref, o_ref, s_ref, h_ref, carry_ref):
    ib = pl.program_id(1)
    c = WKV_C
    nc = tb // c
    lanes_of = lambda g: slice(g * WKV_GW, (g + 1) * WKV_GW)

    @pl.when(ib == 0)
    def _():
        for bi in range(bb):
            carry_ref[bi] = sh0_ref[bi]
            for g in range(WKV_NG):
                blocks = [s0_ref[bi, g * WKV_GROUP + h] for h in range(WKV_GROUP)]
                h_ref[bi, g] = _bd_expand(jnp.concatenate(blocks, axis=1))

    ones_bd = (lax.broadcasted_iota(jnp.int32, (MXU_W, MXU_W), 0) // RW_HEAD
               == lax.broadcasted_iota(jnp.int32, (MXU_W, MXU_W), 1) // RW_HEAD).astype(BF16)

    def prepare(seqs):
        n = len(seqs) * tb
        z = jnp.concatenate([z_ref[bi] for bi in seqs], axis=0)
        mu = mu_ref[...]
        zs = z + (pltpu.roll(z, 1, 0) - z) * mu
        first = lax.broadcasted_iota(jnp.int32, (F32_ROWS, SHIFT_W), 0) == 0
        pieces = []
        for i, bi in enumerate(seqs):
            head = z[i * tb:i * tb + F32_ROWS]
            pieces += [jnp.where(first, head + (carry_ref[bi] - head) * mu, zs[i * tb:i * tb + F32_ROWS]),
                       zs[i * tb + F32_ROWS:(i + 1) * tb]]
            carry_ref[bi] = z[(i + 1) * tb - 1:(i + 1) * tb, :]
        zs = jnp.concatenate(pieces, axis=0)
        yield

        r = zs[:, :RW_W]
        k = zs[:, RW_W:2 * RW_W]
        v = zs[:, 2 * RW_W:3 * RW_W]
        zwa = zs[:, 3 * RW_W:3 * RW_W + W_LORA + A_LORA]
        zg = zs[:, 3 * RW_W + W_LORA + A_LORA:]
        lane = lax.broadcasted_iota(jnp.int32, zwa.shape, 1)
        lora = _dot(jnp.where(lane < W_LORA, jnp.tanh(zwa), zwa).astype(BF16), wa_ref[...])
        ew = math.exp(-0.5) * jax.nn.sigmoid(w0_ref[...] + lora[:, :RW_W])
        a = jax.nn.sigmoid(a0_ref[...] + lora[:, RW_W:])
        yield
        gate = _dot(jax.nn.sigmoid(zg).astype(BF16), g2_ref[...])
        yield

        kk = k * kk_ref[...]
        kk = kk * lax.rsqrt(jnp.maximum(_head_sum(kk * kk, ones_bd), 1e-24))
        yield
        k = k * (1.0 + (a - 1.0) * ka_ref[...])
        bonus = _head_sum(r * k * rk_ref[...], ones_bd) * v
        yield

        if t_valid is not None:
            live = lax.broadcasted_iota(jnp.int32, ew.shape, 0) % tb + ib * tb < t_valid
            ew = jnp.where(live, ew, 0.0)
            k = jnp.where(live, k, 0.0)
            kk = jnp.where(live, kk, 0.0)
            v = jnp.where(live, v, 0.0)

        rows = min(n, MXU_W)
        ti = lax.broadcasted_iota(jnp.int32, (rows, rows), 0)
        si = lax.broadcasted_iota(jnp.int32, (rows, rows), 1)
        tri = jnp.logical_and(si <= ti, si // c == ti // c).astype(BF16)
        e_hi, e_lo = _split2(ew)
        cum = -jnp.concatenate([_dot(tri, e_hi[i0:i0 + rows]) + _dot(tri, e_lo[i0:i0 + rows])
                                for i0 in range(0, n, rows)], axis=0)
        cum_end = jnp.concatenate([jnp.broadcast_to(cum[i0 + c - 1:i0 + c, :], (c, RW_W))
                                   for i0 in range(0, n, c)], axis=0)
        yield
        grow = jnp.exp(-cum)
        to_end = jnp.exp(cum_end - cum)
        b = kk * a
        at = -kk * jnp.exp(cum + ew)
        rt = r * jnp.exp(cum)
        yield
        pre = dict(at=at, rt=rt, bt=b * grow, kt=k * grow, v=v, gate=gate, bonus=bonus)
        yield
        bh, kh, gamma_end = b * to_end, k * to_end, jnp.exp(cum_end)
        for ci in range(nc):
            for i in range(len(seqs)):
                sl = rows_of(ci, i)
                for g in range(WKV_NG):
                    gl = lanes_of(g)
                    pre[ci, i, g] = (v[sl, gl], jnp.concatenate([bh[sl, gl], kh[sl, gl]], axis=0).astype(BF16),
                                     gamma_end[sl, gl][:1])
        return pre

    rows_of = lambda ci, i: slice(i * tb + ci * c, i * tb + (ci + 1) * c)

    def chunk_terms(pre, nseq):
        order = [(ci, i, g) for ci in range(nc) for i in range(nseq) for g in range(WKV_NG)]
        units = [tuple(pre[name][rows_of(ci, i), lanes_of(g)] for name in ("at", "rt", "bt", "kt", "v"))
                 for ci, i, g in order]
        return dict(zip(order, (yield from _wkv_chunk_terms(units))))

    def apply_chunks(seqs, pre, terms):
        chains = [(i, g) for i in range(len(seqs)) for g in range(WKV_NG)]
        states = [h_ref[seqs[i], g] for i, g in chains]
        y_parts = {}
        for ci in range(nc):
            extra = [pre[ci, i, g] for i, g in chains]
            ys, states = yield from _wkv_chunk_apply(states, [terms[ci, i, g] for i, g in chains],
                                                     [e[0] for e in extra], [e[1] for e in extra],
                                                     [e[2] for e in extra])
            for (i, g), y_unit in zip(chains, ys):
                y_parts[ci, i, g] = y_unit
        for (i, g), s_out in zip(chains, states):
            h_ref[seqs[i], g] = s_out
        return jnp.concatenate([jnp.concatenate([y_parts[ci, i, g] for g in range(WKV_NG)], axis=1)
                                for i in range(len(seqs)) for ci in range(nc)], axis=0)

    def finish(seqs, pre, y):
        mean = _head_sum(y, ones_bd) * (1.0 / RW_HEAD)
        yield
        d = y - mean
        var = _head_sum(d * d, ones_bd) * (1.0 / RW_HEAD)
        yield
        yn = d * lax.rsqrt(var + LNX_EPS) * lng_ref[...] + lnb_ref[...]
        out = ((yn + pre["bonus"]) * pre["gate"]).astype(o_ref.dtype)
        for i, bi in enumerate(seqs):
            o_ref[bi] = out[i * tb:(i + 1) * tb]

    halves = [list(range(bb))] if bb == 1 else [list(range(bb // 2)), list(range(bb // 2, bb))]
    first, second = halves[0], halves[-1]
    pre_a, = _interleave(prepare(first))
    if len(halves) == 1:
        terms_a, = _interleave(chunk_terms(pre_a, len(first)))
        y_a, = _interleave(apply_chunks(first, pre_a, terms_a))
        _interleave(finish(first, pre_a, y_a))
    else:
        terms_a, pre_b = _interleave(chunk_terms(pre_a, len(first)), prepare(second))
        y_a, terms_b = _interleave(apply_chunks(first, pre_a, terms_a), chunk_terms(pre_b, len(second)))
        y_b, _ = _interleave(apply_chunks(second, pre_b, terms_b), finish(first, pre_a, y_a))
        _interleave(finish(second, pre_b, y_b))

    @pl.when(ib == pl.num_programs(1) - 1)
    def _():
        for bi in range(bb):
            for g in range(WKV_NG):
                h = h_ref[bi, g]
                for hh in range(WKV_GROUP):
                    blk = h[hh * RW_HEAD:(hh + 1) * RW_HEAD, hh * RW_HEAD:(hh + 1) * RW_HEAD]
                    s_ref[bi, g * WKV_GROUP + hh] = blk


def _rwkv_call(z_rw, s0, shift0, p, bb, tb, t_valid):
    b, t, _ = z_rw.shape
    vec = lambda w: _const_spec((1, w))
    return pl.pallas_call(
        functools.partial(_rwkv_body, bb, tb, t_valid),
        grid=(b // bb, t // tb),
        in_specs=[pl.BlockSpec((bb, tb, SHIFT_W), lambda bi, i: (bi, i, 0)),
                  pl.BlockSpec((bb, RW_HEADS, RW_HEAD, RW_HEAD), lambda bi, i: (bi, 0, 0, 0)),
                  pl.BlockSpec((bb, 1, SHIFT_W), lambda bi, i: (bi, 0, 0)),
                  vec(SHIFT_W), vec(RW_W), _const_spec((W_LORA + A_LORA, 2 * RW_W)), vec(RW_W),
                  _const_spec((G_LORA, RW_W)), vec(RW_W), vec(RW_W), vec(RW_W), vec(RW_W), vec(RW_W)],
        out_specs=[pl.BlockSpec((bb, tb, RW_W), lambda bi, i: (bi, i, 0)),
                   pl.BlockSpec((bb, RW_HEADS, RW_HEAD, RW_HEAD), lambda bi, i: (bi, 0, 0, 0))],
        out_shape=[jax.ShapeDtypeStruct((b, t, RW_W), BF16),
                   jax.ShapeDtypeStruct((b, RW_HEADS, RW_HEAD, RW_HEAD), F32)],
        scratch_shapes=[pltpu.VMEM((bb, WKV_NG, WKV_GW, WKV_GW), F32), pltpu.VMEM((bb, 1, SHIFT_W), F32)],
        compiler_params=_params(("parallel", "arbitrary")),
        name="rwkv",
    )(z_rw, s0, shift0, p["mu"], p["w0"], p["wa"], p["a0"], p["g2"], p["k_k"], p["k_a"], p["r_k"],
      p["lnx_g"], p["lnx_b"])


def _merge_body(h_ref, oa_ref, orw_ref, gate_ref, woa_ref, worw_ref, wout_ref, postg_ref, pre_ref, win_ref,
                wo2_ref, post2_ref, y_ref):
    gates = gate_ref[...]
    merged = (gates[:, :D_MODEL] * _dot(oa_ref[...], woa_ref[...])
              + gates[:, D_MODEL:] * _dot(orw_ref[...], worw_ref[...]))
    h2 = h_ref[...] + _rms(_dot(merged.astype(BF16), wout_ref[...]), postg_ref[...])
    f = _swiglu_ffn(_rms(h2, pre_ref[...]).astype(BF16), win_ref, wo2_ref)
    y_ref[...] = h2 + MACARON * _rms(f, post2_ref[...])


def _merge_call(h, o_att, o_rw, gates, w_o_att, w_o_rwkv, w_out, mix_post_g, pre_g, w_in, w_out2, post_g, tm):
    m = h.shape[0]
    row = lambda w: pl.BlockSpec((tm, w), lambda i: (i, 0))
    vec = _const_spec((1, D_MODEL))
    return pl.pallas_call(
        _merge_body,
        grid=(m // tm,),
        in_specs=[row(D_MODEL), row(ATT_W), row(RW_W), row(GATE_W),
                  _const_spec((ATT_W, D_MODEL)), _const_spec((RW_W, D_MODEL)), _const_spec((D_MODEL, D_MODEL)), vec,
                  vec, _const_spec((D_MODEL, 2 * D_FF)), _const_spec((D_FF, D_MODEL)), vec],
        out_specs=row(D_MODEL),
        out_shape=jax.ShapeDtypeStruct((m, D_MODEL), F32),
        compiler_params=_params(("parallel",)),
        name="merge_ffn",
    )(h, o_att, o_rw, gates, w_o_att, w_o_rwkv, w_out, mix_post_g, pre_g, w_in, w_out2, post_g)


def _pick_tile(m, want):
    t = min(m, want)
    assert m % t == 0
    return t


def _layer(x, pos, l, p, cache_k, cache_v, s0, shift0):
    b, t, _ = x.shape
    m = b * t
    tm = _pick_tile(m, 512)
    lam_init = 0.8 - 0.6 * math.exp(-0.3 * l)

    late = p.get("late_f32")
    h = _ffn_call(x.reshape(m, D_MODEL), p["ffn1_pre_g"], p["ffn1_w_in"], p["ffn1_w_out"], p["ffn1_post_g"],
                  _pick_tile(m, 512), cast=tuple(late.values()) if late else ())
    if late:
        h, casted = h
        p = {**p, **dict(zip(late, casted)), "late_f32": None}

    cos_t, sin_t = _rope_tables(pos)
    tmp = _pick_tile(m, 512)
    if t < tmp:
        cos_t, sin_t = jnp.tile(cos_t, (tmp // t, 1)), jnp.tile(sin_t, (tmp // t, 1))
    q, k, k_bf, v, v_bf, z_rw, gates = _mixproj_call(h, p["mix_pre_g"], p["w_in"], cos_t, sin_t, tmp,
                                                     t if cache_k is None else None)

    lam_args = (p["att_lambda_q1"], p["att_lambda_k1"], p["att_lambda_q2"], p["att_lambda_k2"], p["att_subln_g"])
    r3 = lambda a: a.reshape(b, t, a.shape[-1])
    if cache_k is None:
        o_att = _attn_prompt_call(q, r3(k_bf), v_bf, *lam_args, lam_init, _pick_tile(t, 512))
    else:
        past = cache_k.shape[1]
        cache_kt = cache_k.reshape(b, past, ATT_W).transpose(0, 2, 1)
        cache_vr = cache_v.reshape(b, past * ATT_HEADS, 2 * ATT_DH)
        o_att = _attn_sample_call(r3(q), cache_kt, cache_vr, r3(k_bf), r3(v_bf), *lam_args, lam_init,
                                  _pick_tile(past, 2048))

    z_rw3 = r3(z_rw)
    t_pad = -(-t // WKV_C) * WKV_C
    z_in = z_rw3 if t_pad == t else jnp.pad(z_rw3, ((0, 0), (0, t_pad - t), (0, 0)))
    o_rw, s_new = _rwkv_call(z_in, s0, shift0, p["rwkv"], _pick_tile(b, WKV_SEQS), _pick_tile(t_pad, WKV_TB),
                             None if t_pad == t else t)
    o_rw = o_rw[:, :t]

    y = _merge_call(h, o_att.reshape(m, ATT_W), o_rw.reshape(m, RW_W), gates, p["w_o_att"], p["w_o_rwkv"],
                    p["w_out"], p["mix_post_g"], p["ffn2_pre_g"], p["ffn2_w_in"], p["ffn2_w_out"],
                    p["ffn2_post_g"], tm)
    if cache_k is None:
        k_rows = k.reshape(b, ATT_HEADS, 2, ATT_DH, t).transpose(0, 4, 1, 2, 3)
    else:
        k_rows = k.reshape(b, t, ATT_HEADS, 2, ATT_DH)
    return (y.reshape(b, t, D_MODEL), k_rows, v.reshape(b, t, ATT_HEADS, 2 * ATT_DH), s_new, z_rw3[:, -1:]), p


def kernel(x_prompt, x_sample, cache_att_k, cache_att_v, state_rwkv, state_shift, ffn1_pre_g, ffn1_w_in, ffn1_w_out, ffn1_post_g, mix_pre_g, w_in, att_lambda_q1, att_lambda_k1, att_lambda_q2, att_lambda_k2, att_subln_g, rwkv_mu, rwkv_w0, rwkv_w2, rwkv_a0, rwkv_a2, rwkv_g2, rwkv_k_k, rwkv_k_a, rwkv_r_k, rwkv_lnx_g, rwkv_lnx_b, w_o_att, w_o_rwkv, w_out, mix_post_g, ffn2_pre_g, ffn2_w_in, ffn2_w_out, ffn2_post_g):
    depth = w_in.shape[0]
    bp, tp, _ = x_prompt.shape
    bs, ts, _ = x_sample.shape
    past = cache_att_k.shape[2]
    pos_p = jnp.arange(tp)
    pos_s = past + jnp.arange(ts)
    xp, xs = x_prompt, x_sample
    outs_p, outs_s = [], []
    vec = lambda a: a.reshape(1, -1)
    for l in range(depth):
        zeros = jnp.zeros((W_LORA, RW_W), F32)
        wa = jnp.concatenate([jnp.concatenate([rwkv_w2[l], zeros], axis=1),
                              jnp.concatenate([zeros, rwkv_a2[l]], axis=1)], axis=0)
        p = dict(
            ffn1_pre_g=vec(ffn1_pre_g[l]), ffn1_w_in=ffn1_w_in[l].astype(BF16), ffn1_w_out=ffn1_w_out[l].astype(BF16),
            ffn1_post_g=vec(ffn1_post_g[l]), mix_pre_g=vec(mix_pre_g[l]),
            late_f32=dict(w_in=w_in[l], w_o_att=w_o_att[l], w_o_rwkv=w_o_rwkv[l], w_out=w_out[l],
                          ffn2_w_in=ffn2_w_in[l], ffn2_w_out=ffn2_w_out[l]),
            att_lambda_q1=vec(att_lambda_q1[l]), att_lambda_k1=vec(att_lambda_k1[l]),
            att_lambda_q2=vec(att_lambda_q2[l]), att_lambda_k2=vec(att_lambda_k2[l]),
            att_subln_g=vec(att_subln_g[l]),
            rwkv=dict(mu=vec(rwkv_mu[l]), w0=vec(rwkv_w0[l]), wa=wa.astype(BF16), a0=vec(rwkv_a0[l]),
                      g2=rwkv_g2[l].astype(BF16), k_k=vec(rwkv_k_k[l]), k_a=vec(rwkv_k_a[l]), r_k=vec(rwkv_r_k[l]),
                      lnx_g=vec(rwkv_lnx_g[l]), lnx_b=vec(rwkv_lnx_b[l])),
            mix_post_g=vec(mix_post_g[l]), ffn2_pre_g=vec(ffn2_pre_g[l]), ffn2_post_g=vec(ffn2_post_g[l]),
        )
        s0p = jnp.zeros((bp, RW_HEADS, RW_HEAD, RW_HEAD), F32)
        sh0p = jnp.zeros((bp, 1, SHIFT_W), F32)
        (xp, *rest_p), p = _layer(xp, pos_p, l, p, None, None, s0p, sh0p)
        (xs, *rest_s), p = _layer(xs, pos_s, l, p, cache_att_k[l], cache_att_v[l], state_rwkv[l], state_shift[l])
        outs_p.append(rest_p)
        outs_s.append(rest_s)
    stack = lambda outs, i: jnp.stack([o[i] for o in outs], 0)
    return (xp, xs, stack(outs_p, 0), stack(outs_p, 1), stack(outs_p, 2), stack(outs_p, 3),
            stack(outs_s, 0), stack(outs_s, 1), stack(outs_s, 2), stack(outs_s, 3))
```

```python
import functools
import math

import jax
import jax.numpy as jnp
from jax import lax
from jax.experimental import pallas as pl
from jax.experimental.pallas import tpu as pltpu

F32 = jnp.float32
BF16 = jnp.bfloat16

D_MODEL = 1024
D_FF = 2816
CHUNK = 64
ROPE_THETA = 10000.0
EPS = 1e-6
MACARON = 0.5
ATT_HEADS = 4
ATT_DH = 64
ATT_W = ATT_HEADS * 2 * ATT_DH
ATT_HPS = 2
Q_SCALE = ATT_DH ** -0.5 * math.log2(math.e)
RW_HEAD = 64
RW_W = D_MODEL // 2
RW_HEADS = RW_W // RW_HEAD
W_LORA = 64
A_LORA = 64
G_LORA = 128
SHIFT_W = 3 * RW_W + W_LORA + A_LORA + G_LORA
LNX_EPS = 64e-5
GATE_W = 2 * D_MODEL
IN_W = 3 * ATT_W + SHIFT_W + GATE_W

LANES = 128
F32_ROWS = 8
BF16_ROWS = 16
MXU_W = 256
VMEM_LIMIT = 56 * 1024 * 1024

WKV_C = 64
WKV_GROUP = MXU_W // RW_HEAD
WKV_GW = WKV_GROUP * RW_HEAD
WKV_NG = RW_W // WKV_GW
WKV_SEQS = 4
WKV_TB = 256


def _params(sem):
    return pltpu.CompilerParams(dimension_semantics=sem, vmem_limit_bytes=VMEM_LIMIT)


def _const_spec(shape):
    nd = len(shape)
    return pl.BlockSpec(shape, lambda *_: (0,) * nd, pipeline_mode=pl.Buffered(1))


def _rms(x, g):
    return x * lax.rsqrt(jnp.mean(x * x, axis=-1, keepdims=True) + EPS) * g


def _dot(a, b):
    return jnp.dot(a, b, preferred_element_type=F32)


def _dot_nt(a, b):
    return lax.dot_general(a, b, (((1,), (1,)), ((), ())), preferred_element_type=F32)


def _dot_tn(a, b):
    return lax.dot_general(a, b, (((0,), (0,)), ((), ())), preferred_element_type=F32)


def _swiglu_ffn(xn_bf, w_in_ref, w_out_ref):
    hh = _dot(xn_bf, w_in_ref[...])
    gate = hh[:, :D_FF]
    up = hh[:, D_FF:]
    act = (gate * jax.nn.sigmoid(gate) * up).astype(BF16)
    return _dot(act, w_out_ref[...])


def _ffn_body(cast_steps, x_ref, pre_ref, win_ref, wout_ref, post_ref, *refs):
    ncast = len(cast_steps)
    cast_in, o_ref, cast_out = refs[:ncast], refs[ncast], refs[ncast + 1:]
    x = x_ref[...]
    f = _swiglu_ffn(_rms(x, pre_ref[...]).astype(BF16), win_ref, wout_ref)
    o_ref[...] = x + MACARON * _rms(f, post_ref[...])
    for w_ref, c_ref, steps in zip(cast_in, cast_out, cast_steps):
        @pl.when(pl.program_id(0) < steps)
        def _():
            c_ref[...] = w_ref[...].astype(BF16)


def _cast_rows(rows, nsteps):
    return next(r for r in range(BF16_ROWS, rows + 1, BF16_ROWS) if rows % r == 0 and rows // r <= nsteps)


def _ffn_call(x, pre_g, w_in, w_out, post_g, tm, cast=()):
    m = x.shape[0]
    nsteps = m // tm
    row = lambda w: pl.BlockSpec((tm, w), lambda i: (i, 0))
    slabs = [_cast_rows(w.shape[0], nsteps) for w in cast]
    steps = tuple(w.shape[0] // r for w, r in zip(cast, slabs))
    slab_index = lambda i, last: (jnp.minimum(i, last), 0)
    cast_specs = [pl.BlockSpec((r, w.shape[1]), functools.partial(slab_index, last=s - 1))
                  for w, r, s in zip(cast, slabs, steps)]
    out = pl.pallas_call(
        functools.partial(_ffn_body, steps),
        grid=(nsteps,),
        in_specs=[row(D_MODEL), _const_spec((1, D_MODEL)), _const_spec((D_MODEL, 2 * D_FF)),
                  _const_spec((D_FF, D_MODEL)), _const_spec((1, D_MODEL))] + cast_specs,
        out_specs=[row(D_MODEL)] + cast_specs,
        out_shape=[jax.ShapeDtypeStruct((m, D_MODEL), F32)] + [jax.ShapeDtypeStruct(w.shape, BF16) for w in cast],
        compiler_params=_params(("arbitrary",) if cast else ("parallel",)),
        name="ffn",
    )(x, pre_g, w_in, w_out, post_g, *cast)
    return (out[0], out[1:]) if cast else out[0]


def _rope_tables(pos):
    half = ATT_DH // 2
    inv = ROPE_THETA ** (-jnp.arange(half, dtype=F32) / half)
    ang = pos.astype(F32)[:, None] * inv[None, :]
    cos, sin = jnp.cos(ang), jnp.sin(ang)
    cos_t = jnp.tile(jnp.concatenate([cos, cos], axis=-1), (1, LANES // ATT_DH))
    sin_t = jnp.tile(jnp.concatenate([-sin, sin], axis=-1), (1, LANES // ATT_DH))
    return cos_t, sin_t


def _rope(x, cos_t, sin_t):
    n = x.shape[-1]
    half = ATT_DH // 2
    lane = lax.broadcasted_iota(jnp.int32, x.shape, 1)
    swapped = jnp.where(lane % ATT_DH < half, pltpu.roll(x, n - half, 1), pltpu.roll(x, half, 1))
    return x * cos_t + swapped * sin_t


def _mixproj_body(v_cols, h_ref, g_ref, w_ref, cos_ref, sin_ref, q_ref, k_ref, kb_ref, v_ref, vb_ref, rw_ref,
                  gate_ref):
    u = _rms(h_ref[...], g_ref[...]).astype(BF16)
    z = _dot(u, w_ref[...])
    cos_t = jnp.tile(cos_ref[...], (1, ATT_W // LANES))
    sin_t = jnp.tile(sin_ref[...], (1, ATT_W // LANES))
    q = _rope(z[:, :ATT_W], cos_t, sin_t) * Q_SCALE
    if v_cols:
        q_ref[0] = q.T.astype(BF16)
    else:
        q_ref[...] = q.astype(BF16)
    k = _rope(z[:, ATT_W:2 * ATT_W], cos_t, sin_t)
    kb_ref[...] = k.astype(BF16)
    v = z[:, 2 * ATT_W:3 * ATT_W]
    if v_cols:
        k_ref[0] = k.T
        hw = 2 * ATT_DH
        for h in range(ATT_HEADS):
            v_ref[pl.ds(h, v.shape[0], stride=ATT_HEADS), :] = v[:, h * hw:(h + 1) * hw]
        vb_ref[0] = v.T.astype(BF16)
    else:
        k_ref[...] = k
        v_ref[...] = v
        vb_ref[...] = v.astype(BF16)
    rw_ref[...] = z[:, 3 * ATT_W:3 * ATT_W + SHIFT_W]
    gate_ref[...] = jax.nn.sigmoid(z[:, 3 * ATT_W + SHIFT_W:]).astype(BF16)


def _mixproj_call(h, mix_pre_g, w_in, cos_t, sin_t, tm, v_cols_t):
    m = h.shape[0]
    ntab = cos_t.shape[0] // tm
    row = lambda w: pl.BlockSpec((tm, w), lambda i: (i, 0))
    tab = pl.BlockSpec((tm, LANES), lambda i: (i % ntab, 0))
    sds = lambda w, dt: jax.ShapeDtypeStruct((m, w), dt)
    if v_cols_t is None:
        q_spec, q_shape = row(ATT_W), sds(ATT_W, BF16)
        k_spec, k_shape = row(ATT_W), sds(ATT_W, F32)
        v_spec, v_shape = row(ATT_W), sds(ATT_W, F32)
        vb_spec, vb_shape = row(ATT_W), sds(ATT_W, BF16)
    else:
        nt = v_cols_t // tm
        q_spec = k_spec = vb_spec = pl.BlockSpec((1, ATT_W, tm), lambda i: (i // nt, 0, i % nt))
        k_shape = jax.ShapeDtypeStruct((m // v_cols_t, ATT_W, v_cols_t), F32)
        q_shape = vb_shape = jax.ShapeDtypeStruct((m // v_cols_t, ATT_W, v_cols_t), BF16)
        v_spec = pl.BlockSpec((tm * ATT_HEADS, 2 * ATT_DH), lambda i: (i, 0))
        v_shape = jax.ShapeDtypeStruct((m * ATT_HEADS, 2 * ATT_DH), F32)
    return pl.pallas_call(
        functools.partial(_mixproj_body, v_cols_t is not None),
        grid=(m // tm,),
        in_specs=[row(D_MODEL), _const_spec((1, D_MODEL)), _const_spec((D_MODEL, IN_W)), tab, tab],
        out_specs=[q_spec, k_spec, row(ATT_W), v_spec, vb_spec, row(SHIFT_W), row(GATE_W)],
        out_shape=[q_shape, k_shape, sds(ATT_W, BF16), v_shape, vb_shape,
                   sds(SHIFT_W, F32), sds(GATE_W, BF16)],
        compiler_params=_params(("parallel",)),
        name="mixproj",
    )(h, mix_pre_g, w_in, cos_t, sin_t)


def _lambda(lq1_ref, lk1_ref, lq2_ref, lk2_ref, lam_init):
    s1 = jnp.sum(lq1_ref[...] * lk1_ref[...], axis=-1, keepdims=True)
    s2 = jnp.sum(lq2_ref[...] * lk2_ref[...], axis=-1, keepdims=True)
    return jnp.exp(s1) - jnp.exp(s2) + lam_init


def _subln(o, g, lam_init):
    return o * lax.rsqrt(jnp.mean(o * o, axis=-1, keepdims=True) + EPS) * g * (1.0 - lam_init)


def _attn_prompt_body(lam_init, blk, qt_ref, k_ref, vt_ref, lq1_ref, lk1_ref, lq2_ref, lk2_ref, g_ref, o_ref,
                      qq_ref, m_ref, l_ref, acc_ref, st_ref):
    i = pl.program_id(2)
    hw = 2 * ATT_DH
    heads = range(ATT_HPS)
    for h in heads:
        qt = qt_ref[0, h * hw:(h + 1) * hw, :]
        feat = lax.broadcasted_iota(jnp.int32, qt.shape, 0)
        zero = jnp.zeros_like(qt)
        qq_ref[h, :, :blk] = jnp.where(feat < ATT_DH, qt, zero)
        qq_ref[h, :, blk:] = jnp.where(feat >= ATT_DH, qt, zero)
    m_ref[...] = jnp.full(m_ref.shape, -jnp.inf, F32)
    l_ref[...] = jnp.zeros(l_ref.shape, F32)
    acc_ref[...] = jnp.zeros(acc_ref.shape, F32)

    def scores(h, j):
        start = pl.multiple_of(j * blk, blk)
        st_ref[h] = _dot(k_ref[0, pl.ds(start, blk), h * hw:(h + 1) * hw], qq_ref[h])

    ones = jnp.ones((BF16_ROWS, blk), BF16)

    def consume(h, j, diagonal):
        start = pl.multiple_of(j * blk, blk)
        st = st_ref[h]
        if diagonal:
            key = lax.broadcasted_iota(jnp.int32, st.shape, 0)
            qry = lax.broadcasted_iota(jnp.int32, st.shape, 1) % blk
            st = jnp.where(key // CHUNK <= qry // CHUNK, st, -jnp.inf)
        m_prev = m_ref[h]
        m_new = jnp.maximum(m_prev, jnp.max(st, axis=0, keepdims=True))
        alpha = jnp.exp2(m_prev - m_new)
        p = jnp.exp2(st - m_new).astype(BF16)
        lhs = jnp.concatenate([vt_ref[0, h * hw:(h + 1) * hw, pl.ds(start, blk)], ones], axis=0)
        pv = _dot(lhs, p)
        l_ref[h] = alpha * l_ref[h] + pv[hw:hw + 1]
        acc_ref[h] = alpha * acc_ref[h] + pv[:hw]
        m_ref[h] = m_new

    def block(j, carry):
        scores(1, j)
        consume(0, j, False)
        scores(0, j + 1)
        consume(1, j, False)
        return carry

    scores(0, 0)
    lax.fori_loop(0, i, block, 0)
    scores(1, i)
    consume(0, i, True)
    consume(1, i, True)

    lam = _lambda(lq1_ref, lk1_ref, lq2_ref, lk2_ref, lam_init)
    for h in heads:
        o = acc_ref[h] / l_ref[h]
        o = o[:, :blk] - lam * o[:, blk:]
        o = o * lax.rsqrt(jnp.mean(o * o, axis=0, keepdims=True) + EPS) * g_ref[...] * (1.0 - lam_init)
        o_ref[0, :, h * hw:(h + 1) * hw] = o.T.astype(o_ref.dtype)


def _attn_prompt_call(qt, k, vt, lq1, lk1, lq2, lk2, subln_g, lam_init, blk):
    b, t, _ = k.shape
    hw = 2 * ATT_DH
    gw = ATT_HPS * hw
    qtspec = pl.BlockSpec((1, gw, blk), lambda bi, hi, i: (bi, hi, i))
    kspec = pl.BlockSpec((1, t, gw), lambda bi, hi, i: (bi, 0, hi))
    vtspec = pl.BlockSpec((1, gw, t), lambda bi, hi, i: (bi, hi, 0))
    return pl.pallas_call(
        functools.partial(_attn_prompt_body, lam_init, blk),
        grid=(b, ATT_HEADS // ATT_HPS, t // blk),
        in_specs=[qtspec, kspec, vtspec] + [_const_spec((1, ATT_DH))] * 4 + [_const_spec((hw, 1))],
        out_specs=pl.BlockSpec((1, blk, gw), lambda bi, hi, i: (bi, i, hi)),
        out_shape=jax.ShapeDtypeStruct((b, t, ATT_W), BF16),
        scratch_shapes=[pltpu.VMEM((ATT_HPS, hw, 2 * blk), BF16), pltpu.VMEM((ATT_HPS, 1, 2 * blk), F32),
                        pltpu.VMEM((ATT_HPS, 1, 2 * blk), F32), pltpu.VMEM((ATT_HPS, hw, 2 * blk), F32),
                        pltpu.VMEM((ATT_HPS, blk, 2 * blk), F32)],
        compiler_params=_params(("parallel", "parallel", "arbitrary")),
        name="attn_prompt",
    )(qt, k, vt, lq1, lk1, lq2, lk2, subln_g.reshape(hw, 1))


def _attn_sample_body(lam_init, ts, q_ref, ckt_ref, cv_ref, kn_ref, vn_ref, lq1_ref, lk1_ref, lq2_ref, lk2_ref,
                      g_ref, o_ref, qq_ref, m_ref, l_ref, acc_ref):
    j = pl.program_id(1)
    hw = 2 * ATT_DH
    heads = range(ATT_HEADS)

    @pl.when(j == 0)
    def _():
        for h in heads:
            q = q_ref[0, :, h * hw:(h + 1) * hw]
            lane = lax.broadcasted_iota(jnp.int32, q.shape, 1)
            zero = jnp.zeros_like(q)
            qq_ref[h, :ts, :] = jnp.where(lane < ATT_DH, q, zero)
            qq_ref[h, ts:, :] = jnp.where(lane >= ATT_DH, q, zero)
        m_ref[...] = jnp.full(m_ref.shape, -jnp.inf, F32)
        l_ref[...] = jnp.zeros(l_ref.shape, F32)
        acc_ref[...] = jnp.zeros(acc_ref.shape, F32)

    def softmax_step(ss, vs):
        m_prev = [m_ref[h] for h in heads]
        m_new = [jnp.maximum(m, jnp.max(s, axis=-1, keepdims=True)) for m, s in zip(m_prev, ss)]
        alpha = [jnp.exp2(m - mn) for m, mn in zip(m_prev, m_new)]
        p = [jnp.exp2(s - mn) for s, mn in zip(ss, m_new)]
        pv = [_dot(x.astype(BF16), v) for x, v in zip(p, vs)]
        for h in heads:
            l_ref[h] = alpha[h] * l_ref[h] + jnp.sum(p[h], axis=-1, keepdims=True)
            acc_ref[h] = alpha[h] * acc_ref[h] + pv[h]
            m_ref[h] = m_new[h]

    tk = ckt_ref.shape[2]
    softmax_step([_dot(qq_ref[h], ckt_ref[0, h * hw:(h + 1) * hw, :].astype(BF16)) for h in heads],
                 [cv_ref[0, pl.ds(h, tk, stride=ATT_HEADS), :].astype(BF16) for h in heads])

    @pl.when(j == pl.num_programs(1) - 1)
    def _():
        lam = _lambda(lq1_ref, lk1_ref, lq2_ref, lk2_ref, lam_init)
        cols = [slice(h * hw, (h + 1) * hw) for h in heads]
        softmax_step([_dot_nt(qq_ref[h], kn_ref[0, :, cols[h]]) for h in heads],
                     [vn_ref[0, :, cols[h]] for h in heads])
        for h in heads:
            o = acc_ref[h] / l_ref[h]
            o_ref[0, :, cols[h]] = _subln(o[:ts] - lam * o[ts:], g_ref[...], lam_init).astype(o_ref.dtype)


def _attn_sample_call(q, cache_kt, cache_v, k_new, v_new, lq1, lk1, lq2, lk2, subln_g, lam_init, tk):
    b, ts, _ = q.shape
    past = cache_kt.shape[2]
    hw = 2 * ATT_DH
    new = pl.BlockSpec((1, ts, ATT_W), lambda bi, j: (bi, 0, 0))
    return pl.pallas_call(
        functools.partial(_attn_sample_body, lam_init, ts),
        grid=(b, past // tk),
        in_specs=[new, pl.BlockSpec((1, ATT_W, tk), lambda bi, j: (bi, 0, j)),
                  pl.BlockSpec((1, tk * ATT_HEADS, hw), lambda bi, j: (bi, j, 0)), new, new]
        + [_const_spec((1, ATT_DH))] * 4 + [_const_spec((1, hw))],
        out_specs=new,
        out_shape=jax.ShapeDtypeStruct((b, ts, ATT_W), BF16),
        scratch_shapes=[pltpu.VMEM((ATT_HEADS, 2 * ts, hw), BF16), pltpu.VMEM((ATT_HEADS, 2 * ts, 1), F32),
                        pltpu.VMEM((ATT_HEADS, 2 * ts, 1), F32), pltpu.VMEM((ATT_HEADS, 2 * ts, hw), F32)],
        compiler_params=_params(("parallel", "arbitrary")),
        name="attn_sample",
    )(q, cache_kt, cache_v, k_new, v_new, lq1, lk1, lq2, lk2, subln_g)


def _split2(x):
    hi = x.astype(BF16)
    return hi, (x - hi.astype(F32)).astype(BF16)


def _head_sum(x, ones_bd, passes=2):
    parts = []
    for g in range(x.shape[1] // MXU_W):
        xg = x[:, g * MXU_W:(g + 1) * MXU_W]
        if passes == 1:
            parts.append(_dot(xg.astype(BF16), ones_bd))
        else:
            hi, lo = _split2(xg)
            parts.append(_dot(hi, ones_bd) + _dot(lo, ones_bd))
    return jnp.concatenate(parts, axis=1)


def _bd_expand(x):
    c = x.shape[0]
    xt = jnp.concatenate([x] * WKV_GROUP, axis=0)
    row = lax.broadcasted_iota(jnp.int32, xt.shape, 0)
    lane = lax.broadcasted_iota(jnp.int32, xt.shape, 1)
    return jnp.where(row // c == lane // RW_HEAD, xt, jnp.zeros_like(xt))


def _wkv_chunk_terms(units):
    c = units[0][0].shape[0]
    slab = (c, WKV_GROUP * c)
    t_idx = lax.broadcasted_iota(jnp.int32, slab, 0)
    s_idx = lax.broadcasted_iota(jnp.int32, slab, 1) % c
    strict = s_idx < t_idx
    incl = s_idx <= t_idx
    eye = jnp.where(s_idx == t_idx, 1.0, 0.0)
    expand = lambda x: _bd_expand(x).astype(BF16)

    ar = [jnp.concatenate([u[0], u[1]], axis=0).astype(BF16) for u in units]
    sb = [_dot_nt(a, expand(u[2])) for a, u in zip(ar, units)]
    yield
    sk = [_dot_nt(a, expand(u[3])) for a, u in zip(ar, units)]
    yield
    l_ab = [jnp.where(strict, x[:c], 0.0) for x in sb]
    m_rb = [jnp.where(incl, x[c:], 0.0).astype(BF16) for x in sb]
    l_ak = [jnp.where(strict, x[:c], 0.0).astype(BF16) for x in sk]
    m_rk = [jnp.where(incl, x[c:], 0.0).astype(BF16) for x in sk]

    lp = l_ab
    tinv = [eye + x for x in l_ab]
    lp = [_dot(x.astype(BF16), expand(x)) for x in lp]
    yield
    power = 2
    while power < c:
        last = 2 * power >= c
        nxt = []
        for i, (x, t) in enumerate(zip(lp, tinv)):
            lhs = t if last else jnp.concatenate([x, t], axis=0)
            prod = _dot(lhs.astype(BF16), expand(x))
            if last:
                tinv[i] = t + prod
            else:
                nxt.append(prod[:c])
                tinv[i] = t + prod[c:]
        lp = nxt
        power *= 2
        yield

    v_exp = [expand(u[4]) for u in units]
    lakv = [_dot(x, ve) for x, ve in zip(l_ak, v_exp)]
    yield
    tinv_bf = [t.astype(BF16) for t in tinv]
    ta = [_dot(t, expand(u[0])) for t, u in zip(tinv_bf, units)]
    yield
    uv = [_dot(t, expand(x)) for t, x in zip(tinv_bf, lakv)]
    yield

    tr = [jnp.concatenate([x, u[1]], axis=0).astype(BF16) for x, u in zip(ta, units)]
    m_rbk = [jnp.concatenate([m, mk], axis=1) for m, mk in zip(m_rb, m_rk)]
    return list(zip(tr, uv, m_rbk, v_exp))


def _wkv_chunk_apply(states, terms, vs, bkhs, decays):
    c = vs[0].shape[0]
    gw = states[0].shape[0]
    bd = (lax.broadcasted_iota(jnp.int32, (gw, gw), 0) // RW_HEAD
          == lax.broadcasted_iota(jnp.int32, (gw, gw), 1) // RW_HEAD)
    ur = [_dot_nt(t[0], s.astype(BF16)) for t, s in zip(terms, states)]
    yield
    u = [x[:c] + t[1] for x, t in zip(ur, terms)]
    upd = [_dot_tn(jnp.concatenate([x, v], axis=0).astype(BF16), bkh) for x, v, bkh in zip(u, vs, bkhs)]
    yield
    ys = [x[c:] + _dot(t[2], jnp.concatenate([_bd_expand(uu).astype(BF16), t[3]], axis=0))
          for x, uu, t in zip(ur, u, terms)]
    yield
    return ys, [s * d + jnp.where(bd, x, 0.0) for s, d, x in zip(states, decays, upd)]


def _interleave(*gens):
    results = [None] * len(gens)
    live = list(range(len(gens)))
    while live:
        for i in list(live):
            try:
                next(gens[i])
            except StopIteration as stop:
                results[i] = stop.value
                live.remove(i)
    return results


def _rwkv_body(bb, tb, t_valid, z_ref, s0_ref, sh0_ref, mu_ref, w0_ref, wa_ref, a0_ref, g2_ref, kk_ref, ka_ref,
               rk_ref, lng_ref, lnb_ref, o_ref, s_ref, h_ref, carry_ref):
    ib = pl.program_id(1)
    c = WKV_C
    nc = tb // c
    lanes_of = lambda g: slice(g * WKV_GW, (g + 1) * WKV_GW)

    @pl.when(ib == 0)
    def _():
        for bi in range(bb):
            carry_ref[bi] = sh0_ref[bi]
            for g in range(WKV_NG):
                blocks = [s0_ref[bi, g * WKV_GROUP + h] for h in range(WKV_GROUP)]
                h_ref[bi, g] = _bd_expand(jnp.concatenate(blocks, axis=1))

    ones_bd = (lax.broadcasted_iota(jnp.int32, (MXU_W, MXU_W), 0) // RW_HEAD
               == lax.broadcasted_iota(jnp.int32, (MXU_W, MXU_W), 1) // RW_HEAD).astype(BF16)

    def prepare(seqs):
        n = len(seqs) * tb
        z = jnp.concatenate([z_ref[bi] for bi in seqs], axis=0)
        mu = mu_ref[...]
        zs = z + (pltpu.roll(z, 1, 0) - z) * mu
        first = lax.broadcasted_iota(jnp.int32, (F32_ROWS, SHIFT_W), 0) == 0
        pieces = []
        for i, bi in enumerate(seqs):
            head = z[i * tb:i * tb + F32_ROWS]
            pieces += [jnp.where(first, head + (carry_ref[bi] - head) * mu, zs[i * tb:i * tb + F32_ROWS]),
                       zs[i * tb + F32_ROWS:(i + 1) * tb]]
            carry_ref[bi] = z[(i + 1) * tb - 1:(i + 1) * tb, :]
        zs = jnp.concatenate(pieces, axis=0)
        yield

        r = zs[:, :RW_W]
        k = zs[:, RW_W:2 * RW_W]
        v = zs[:, 2 * RW_W:3 * RW_W]
        zwa = zs[:, 3 * RW_W:3 * RW_W + W_LORA + A_LORA]
        zg = zs[:, 3 * RW_W + W_LORA + A_LORA:]
        lane = lax.broadcasted_iota(jnp.int32, zwa.shape, 1)
        lora = _dot(jnp.where(lane < W_LORA, jnp.tanh(zwa), zwa).astype(BF16), wa_ref[...])
        ew = math.exp(-0.5) * jax.nn.sigmoid(w0_ref[...] + lora[:, :RW_W])
        a = jax.nn.sigmoid(a0_ref[...] + lora[:, RW_W:])
        yield
        gate = _dot(jax.nn.sigmoid(zg).astype(BF16), g2_ref[...])
        yield

        kk = k * kk_ref[...]
        kk = kk * lax.rsqrt(jnp.maximum(_head_sum(kk * kk, ones_bd, passes=1), 1e-24))
        yield
        k = k * (1.0 + (a - 1.0) * ka_ref[...])
        bonus = _head_sum(r * k * rk_ref[...], ones_bd) * v
        yield

        if t_valid is not None:
            live = lax.broadcasted_iota(jnp.int32, ew.shape, 0) % tb + ib * tb < t_valid
            ew = jnp.where(live, ew, 0.0)
            k = jnp.where(live, k, 0.0)
            kk = jnp.where(live, kk, 0.0)
            v = jnp.where(live, v, 0.0)

        rows = min(n, MXU_W)
        ti = lax.broadcasted_iota(jnp.int32, (rows, rows), 0)
        si = lax.broadcasted_iota(jnp.int32, (rows, rows), 1)
        tri = jnp.logical_and(si <= ti, si // c == ti // c).astype(BF16)
        e_hi, e_lo = _split2(ew)
        cum = -jnp.concatenate([_dot(tri, e_hi[i0:i0 + rows]) + _dot(tri, e_lo[i0:i0 + rows])
                                for i0 in range(0, n, rows)], axis=0)
        gamma_end = jnp.concatenate([jnp.broadcast_to(jnp.exp(cum[i0 + c - 1:i0 + c, :]), (c, RW_W))
                                     for i0 in range(0, n, c)], axis=0)
        yield
        grow = jnp.exp(-cum)
        b = kk * a
        at = -kk * jnp.exp(cum + ew)
        rt = r / grow
        yield
        bt, kt = b * grow, k * grow
        pre = dict(at=at, rt=rt, bt=bt, kt=kt, v=v, gate=gate, bonus=bonus)
        yield
        bh, kh = bt * gamma_end, kt * gamma_end
        for ci in range(nc):
            for i in range(len(seqs)):
                sl = rows_of(ci, i)
                for g in range(WKV_NG):
                    gl = lanes_of(g)
                    pre[ci, i, g] = (v[sl, gl], jnp.concatenate([bh[sl, gl], kh[sl, gl]], axis=0).astype(BF16),
                                     gamma_end[sl, gl][:1])
        return pre

    rows_of = lambda ci, i: slice(i * tb + ci * c, i * tb + (ci + 1) * c)

    def chunk_terms(pre, nseq):
        order = [(ci, i, g) for ci in range(nc) for i in range(nseq) for g in range(WKV_NG)]
        units = [tuple(pre[name][rows_of(ci, i), lanes_of(g)] for name in ("at", "rt", "bt", "kt", "v"))
                 for ci, i, g in order]
        return dict(zip(order, (yield from _wkv_chunk_terms(units))))

    def apply_chunks(seqs, pre, terms):
        chains = [(i, g) for i in range(len(seqs)) for g in range(WKV_NG)]
        states = [h_ref[seqs[i], g] for i, g in chains]
        y_parts = {}
        for ci in range(nc):
            extra = [pre[ci, i, g] for i, g in chains]
            ys, states = yield from _wkv_chunk_apply(states, [terms[ci, i, g] for i, g in chains],
                                                     [e[0] for e in extra], [e[1] for e in extra],
                                                     [e[2] for e in extra])
            for (i, g), y_unit in zip(chains, ys):
                y_parts[ci, i, g] = y_unit
        for (i, g), s_out in zip(chains, states):
            h_ref[seqs[i], g] = s_out
        return jnp.concatenate([jnp.concatenate([y_parts[ci, i, g] for g in range(WKV_NG)], axis=1)
                                for i in range(len(seqs)) for ci in range(nc)], axis=0)

    def finish(seqs, pre, y):
        mean = _head_sum(y, ones_bd) * (1.0 / RW_HEAD)
        yield
        d = y - mean
        var = _head_sum(d * d, ones_bd, passes=1) * (1.0 / RW_HEAD)
        yield
        yn = d * lax.rsqrt(var + LNX_EPS) * lng_ref[...] + lnb_ref[...]
        out = ((yn + pre["bonus"]) * pre["gate"]).astype(o_ref.dtype)
        for i, bi in enumerate(seqs):
            o_ref[bi] = out[i * tb:(i + 1) * tb]

    halves = [list(range(bb))] if bb == 1 else [list(range(bb // 2)), list(range(bb // 2, bb))]
    first, second = halves[0], halves[-1]
    pre_a, = _interleave(prepare(first))
    if len(halves) == 1:
        terms_a, = _interleave(chunk_terms(pre_a, len(first)))
        y_a, = _interleave(apply_chunks(first, pre_a, terms_a))
        _interleave(finish(first, pre_a, y_a))
    else:
        terms_a, pre_b = _interleave(chunk_terms(pre_a, len(first)), prepare(second))
        y_a, terms_b = _interleave(apply_chunks(first, pre_a, terms_a), chunk_terms(pre_b, len(second)))
        y_b, _ = _interleave(apply_chunks(second, pre_b, terms_b), finish(first, pre_a, y_a))
        _interleave(finish(second, pre_b, y_b))

    @pl.when(ib == pl.num_programs(1) - 1)
    def _():
        for bi in range(bb):
            for g in range(WKV_NG):
                h = h_ref[bi, g]
                for hh in range(WKV_GROUP):
                    blk = h[hh * RW_HEAD:(hh + 1) * RW_HEAD, hh * RW_HEAD:(hh + 1) * RW_HEAD]
                    s_ref[bi, g * WKV_GROUP + hh] = blk


def _rwkv_call(z_rw, s0, shift0, p, bb, tb, t_valid):
    b, t, _ = z_rw.shape
    vec = lambda w: _const_spec((1, w))
    return pl.pallas_call(
        functools.partial(_rwkv_body, bb, tb, t_valid),
        grid=(b // bb, t // tb),
        in_specs=[pl.BlockSpec((bb, tb, SHIFT_W), lambda bi, i: (bi, i, 0)),
                  pl.BlockSpec((bb, RW_HEADS, RW_HEAD, RW_HEAD), lambda bi, i: (bi, 0, 0, 0)),
                  pl.BlockSpec((bb, 1, SHIFT_W), lambda bi, i: (bi, 0, 0)),
                  vec(SHIFT_W), vec(RW_W), _const_spec((W_LORA + A_LORA, 2 * RW_W)), vec(RW_W),
                  _const_spec((G_LORA, RW_W)), vec(RW_W), vec(RW_W), vec(RW_W), vec(RW_W), vec(RW_W)],
        out_specs=[pl.BlockSpec((bb, tb, RW_W), lambda bi, i: (bi, i, 0)),
                   pl.BlockSpec((bb, RW_HEADS, RW_HEAD, RW_HEAD), lambda bi, i: (bi, 0, 0, 0))],
        out_shape=[jax.ShapeDtypeStruct((b, t, RW_W), BF16),
                   jax.ShapeDtypeStruct((b, RW_HEADS, RW_HEAD, RW_HEAD), F32)],
        scratch_shapes=[pltpu.VMEM((bb, WKV_NG, WKV_GW, WKV_GW), F32), pltpu.VMEM((bb, 1, SHIFT_W), F32)],
        compiler_params=_params(("parallel", "arbitrary")),
        name="rwkv",
    )(z_rw, s0, shift0, p["mu"], p["w0"], p["wa"], p["a0"], p["g2"], p["k_k"], p["k_a"], p["r_k"],
      p["lnx_g"], p["lnx_b"])


def _merge_body(h_ref, oa_ref, orw_ref, gate_ref, woa_ref, worw_ref, wout_ref, postg_ref, pre_ref, win_ref,
                wo2_ref, post2_ref, y_ref):
    gates = gate_ref[...]
    merged = (gates[:, :D_MODEL] * _dot(oa_ref[...], woa_ref[...])
              + gates[:, D_MODEL:] * _dot(orw_ref[...], worw_ref[...]))
    h2 = h_ref[...] + _rms(_dot(merged.astype(BF16), wout_ref[...]), postg_ref[...])
    f = _swiglu_ffn(_rms(h2, pre_ref[...]).astype(BF16), win_ref, wo2_ref)
    y_ref[...] = h2 + MACARON * _rms(f, post2_ref[...])


def _merge_call(h, o_att, o_rw, gates, w_o_att, w_o_rwkv, w_out, mix_post_g, pre_g, w_in, w_out2, post_g, tm):
    m = h.shape[0]
    row = lambda w: pl.BlockSpec((tm, w), lambda i: (i, 0))
    vec = _const_spec((1, D_MODEL))
    return pl.pallas_call(
        _merge_body,
        grid=(m // tm,),
        in_specs=[row(D_MODEL), row(ATT_W), row(RW_W), row(GATE_W),
                  _const_spec((ATT_W, D_MODEL)), _const_spec((RW_W, D_MODEL)), _const_spec((D_MODEL, D_MODEL)), vec,
                  vec, _const_spec((D_MODEL, 2 * D_FF)), _const_spec((D_FF, D_MODEL)), vec],
        out_specs=row(D_MODEL),
        out_shape=jax.ShapeDtypeStruct((m, D_MODEL), F32),
        compiler_params=_params(("parallel",)),
        name="merge_ffn",
    )(h, o_att, o_rw, gates, w_o_att, w_o_rwkv, w_out, mix_post_g, pre_g, w_in, w_out2, post_g)


def _pick_tile(m, want):
    t = min(m, want)
    assert m % t == 0
    return t


def _layer(x, pos, l, p, cache_k, cache_v, s0, shift0):
    b, t, _ = x.shape
    m = b * t
    tm = _pick_tile(m, 512)
    lam_init = 0.8 - 0.6 * math.exp(-0.3 * l)

    late = p.get("late_f32")
    h = _ffn_call(x.reshape(m, D_MODEL), p["ffn1_pre_g"], p["ffn1_w_in"], p["ffn1_w_out"], p["ffn1_post_g"],
                  _pick_tile(m, 512), cast=tuple(late.values()) if late else ())
    if late:
        h, casted = h
        p = {**p, **dict(zip(late, casted)), "late_f32": None}

    cos_t, sin_t = _rope_tables(pos)
    tmp = _pick_tile(m, 512)
    if t < tmp:
        cos_t, sin_t = jnp.tile(cos_t, (tmp // t, 1)), jnp.tile(sin_t, (tmp // t, 1))
    q, k, k_bf, v, v_bf, z_rw, gates = _mixproj_call(h, p["mix_pre_g"], p["w_in"], cos_t, sin_t, tmp,
                                                     t if cache_k is None else None)

    lam_args = (p["att_lambda_q1"], p["att_lambda_k1"], p["att_lambda_q2"], p["att_lambda_k2"], p["att_subln_g"])
    r3 = lambda a: a.reshape(b, t, a.shape[-1])
    if cache_k is None:
        o_att = _attn_prompt_call(q, r3(k_bf), v_bf, *lam_args, lam_init, _pick_tile(t, 512))
    else:
        past = cache_k.shape[1]
        cache_kt = cache_k.reshape(b, past, ATT_W).transpose(0, 2, 1)
        cache_vr = cache_v.reshape(b, past * ATT_HEADS, 2 * ATT_DH)
        o_att = _attn_sample_call(r3(q), cache_kt, cache_vr, r3(k_bf), r3(v_bf), *lam_args, lam_init,
                                  _pick_tile(past, 2048))

    z_rw3 = r3(z_rw)
    t_pad = -(-t // WKV_C) * WKV_C
    z_in = z_rw3 if t_pad == t else jnp.pad(z_rw3, ((0, 0), (0, t_pad - t), (0, 0)))
    o_rw, s_new = _rwkv_call(z_in, s0, shift0, p["rwkv"], _pick_tile(b, WKV_SEQS), _pick_tile(t_pad, WKV_TB),
                             None if t_pad == t else t)
    o_rw = o_rw[:, :t]

    y = _merge_call(h, o_att.reshape(m, ATT_W), o_rw.reshape(m, RW_W), gates, p["w_o_att"], p["w_o_rwkv"],
                    p["w_out"], p["mix_post_g"], p["ffn2_pre_g"], p["ffn2_w_in"], p["ffn2_w_out"],
                    p["ffn2_post_g"], tm)
    if cache_k is None:
        k_rows = k.reshape(b, ATT_HEADS, 2, ATT_DH, t).transpose(0, 4, 1, 2, 3)
    else:
        k_rows = k.reshape(b, t, ATT_HEADS, 2, ATT_DH)
    return (y.reshape(b, t, D_MODEL), k_rows, v.reshape(b, t, ATT_HEADS, 2 * ATT_DH), s_new, z_rw3[:, -1:]), p


def kernel(x_prompt, x_sample, cache_att_k, cache_att_v, state_rwkv, state_shift, ffn1_pre_g, ffn1_w_in, ffn1_w_out, ffn1_post_g, mix_pre_g, w_in, att_lambda_q1, att_lambda_k1, att_lambda_q2, att_lambda_k2, att_subln_g, rwkv_mu, rwkv_w0, rwkv_w2, rwkv_a0, rwkv_a2, rwkv_g2, rwkv_k_k, rwkv_k_a, rwkv_r_k, rwkv_lnx_g, rwkv_lnx_b, w_o_att, w_o_rwkv, w_out, mix_post_g, ffn2_pre_g, ffn2_w_in, ffn2_w_out, ffn2_post_g):
    depth = w_in.shape[0]
    bp, tp, _ = x_prompt.shape
    bs, ts, _ = x_sample.shape
    past = cache_att_k.shape[2]
    pos_p = jnp.arange(tp)
    pos_s = past + jnp.arange(ts)
    xp, xs = x_prompt, x_sample
    outs_p, outs_s = [], []
    vec = lambda a: a.reshape(1, -1)
    for l in range(depth):
        zeros = jnp.zeros((W_LORA, RW_W), F32)
        wa = jnp.concatenate([jnp.concatenate([rwkv_w2[l], zeros], axis=1),
                              jnp.concatenate([zeros, rwkv_a2[l]], axis=1)], axis=0)
        p = dict(
            ffn1_pre_g=vec(ffn1_pre_g[l]), ffn1_w_in=ffn1_w_in[l].astype(BF16), ffn1_w_out=ffn1_w_out[l].astype(BF16),
            ffn1_post_g=vec(ffn1_post_g[l]), mix_pre_g=vec(mix_pre_g[l]),
            late_f32=dict(w_in=w_in[l], w_o_att=w_o_att[l], w_o_rwkv=w_o_rwkv[l], w_out=w_out[l],
                          ffn2_w_in=ffn2_w_in[l], ffn2_w_out=ffn2_w_out[l]),
            att_lambda_q1=vec(att_lambda_q1[l]), att_lambda_k1=vec(att_lambda_k1[l]),
            att_lambda_q2=vec(att_lambda_q2[l]), att_lambda_k2=vec(att_lambda_k2[l]),
            att_subln_g=vec(att_subln_g[l]),
            rwkv=dict(mu=vec(rwkv_mu[l]), w0=vec(rwkv_w0[l]), wa=wa.astype(BF16), a0=vec(rwkv_a0[l]),
                      g2=rwkv_g2[l].astype(BF16), k_k=vec(rwkv_k_k[l]), k_a=vec(rwkv_k_a[l]), r_k=vec(rwkv_r_k[l]),
                      lnx_g=vec(rwkv_lnx_g[l]), lnx_b=vec(rwkv_lnx_b[l])),
            mix_post_g=vec(mix_post_g[l]), ffn2_pre_g=vec(ffn2_pre_g[l]), ffn2_post_g=vec(ffn2_post_g[l]),
        )
        s0p = jnp.zeros((bp, RW_HEADS, RW_HEAD, RW_HEAD), F32)
        sh0p = jnp.zeros((bp, 1, SHIFT_W), F32)
        (xp, *rest_p), p = _layer(xp, pos_p, l, p, None, None, s0p, sh0p)
        (xs, *rest_s), p = _layer(xs, pos_s, l, p, cache_att_k[l], cache_att_v[l], state_rwkv[l], state_shift[l])
        outs_p.append(rest_p)
        outs_s.append(rest_s)
    stack = lambda outs, i: jnp.stack([o[i] for o in outs], 0)
    return (xp, xs, stack(outs_p, 0), stack(outs_p, 1), stack(outs_p, 2), stack(outs_p, 3),
            stack(outs_s, 0), stack(outs_s, 1), stack(outs_s, 2), stack(outs_s, 3))
```

```python
import functools
import math

import jax
import jax.numpy as jnp
from jax import lax
from jax.experimental import pallas as pl
from jax.experimental.pallas import tpu as pltpu

F32 = jnp.float32
BF16 = jnp.bfloat16

D_MODEL = 1024
D_FF = 2816
CHUNK = 64
ROPE_THETA = 10000.0
EPS = 1e-6
MACARON = 0.5
ATT_HEADS = 4
ATT_DH = 64
ATT_W = ATT_HEADS * 2 * ATT_DH
ATT_HPS = 2
Q_SCALE = ATT_DH ** -0.5 * math.log2(math.e)
RW_HEAD = 64
RW_W = D_MODEL // 2
RW_HEADS = RW_W // RW_HEAD
W_LORA = 64
A_LORA = 64
G_LORA = 128
SHIFT_W = 3 * RW_W + W_LORA + A_LORA + G_LORA
LNX_EPS = 64e-5
GATE_W = 2 * D_MODEL
IN_W = 3 * ATT_W + SHIFT_W + GATE_W

LANES = 128
F32_ROWS = 8
BF16_ROWS = 16
MXU_W = 256
VMEM_LIMIT = 56 * 1024 * 1024

WKV_C = 64
WKV_GROUP = MXU_W // RW_HEAD
WKV_GW = WKV_GROUP * RW_HEAD
WKV_NG = RW_W // WKV_GW
WKV_SEQS = 8
WKV_TB = 256


def _params(sem):
    return pltpu.CompilerParams(dimension_semantics=sem, vmem_limit_bytes=VMEM_LIMIT)


def _const_spec(shape):
    nd = len(shape)
    return pl.BlockSpec(shape, lambda *_: (0,) * nd, pipeline_mode=pl.Buffered(1))


def _rms(x, g):
    return x * lax.rsqrt(jnp.mean(x * x, axis=-1, keepdims=True) + EPS) * g


def _dot(a, b):
    return jnp.dot(a, b, preferred_element_type=F32)


def _dot_nt(a, b):
    return lax.dot_general(a, b, (((1,), (1,)), ((), ())), preferred_element_type=F32)


def _dot_tn(a, b):
    return lax.dot_general(a, b, (((0,), (0,)), ((), ())), preferred_element_type=F32)


def _swiglu_ffn(xn_bf, w_in_ref, w_out_ref):
    hh = _dot(xn_bf, w_in_ref[...])
    gate = hh[:, :D_FF]
    up = hh[:, D_FF:]
    act = (gate * jax.nn.sigmoid(gate) * up).astype(BF16)
    return _dot(act, w_out_ref[...])


def _ffn_body(cast_steps, x_ref, pre_ref, win_ref, wout_ref, post_ref, *refs):
    ncast = len(cast_steps)
    cast_in, o_ref, cast_out = refs[:ncast], refs[ncast], refs[ncast + 1:]
    x = x_ref[...]
    f = _swiglu_ffn(_rms(x, pre_ref[...]).astype(BF16), win_ref, wout_ref)
    o_ref[...] = x + MACARON * _rms(f, post_ref[...])
    for w_ref, c_ref, steps in zip(cast_in, cast_out, cast_steps):
        @pl.when(pl.program_id(0) < steps)
        def _():
            c_ref[...] = w_ref[...].astype(BF16)


def _cast_rows(rows, nsteps):
    return next(r for r in range(BF16_ROWS, rows + 1, BF16_ROWS) if rows % r == 0 and rows // r <= nsteps)


def _ffn_call(x, pre_g, w_in, w_out, post_g, tm, cast=()):
    m = x.shape[0]
    nsteps = m // tm
    row = lambda w: pl.BlockSpec((tm, w), lambda i: (i, 0))
    slabs = [_cast_rows(w.shape[0], nsteps) for w in cast]
    steps = tuple(w.shape[0] // r for w, r in zip(cast, slabs))
    slab_index = lambda i, last: (jnp.minimum(i, last), 0)
    cast_specs = [pl.BlockSpec((r, w.shape[1]), functools.partial(slab_index, last=s - 1))
                  for w, r, s in zip(cast, slabs, steps)]
    out = pl.pallas_call(
        functools.partial(_ffn_body, steps),
        grid=(nsteps,),
        in_specs=[row(D_MODEL), _const_spec((1, D_MODEL)), _const_spec((D_MODEL, 2 * D_FF)),
                  _const_spec((D_FF, D_MODEL)), _const_spec((1, D_MODEL))] + cast_specs,
        out_specs=[row(D_MODEL)] + cast_specs,
        out_shape=[jax.ShapeDtypeStruct((m, D_MODEL), F32)] + [jax.ShapeDtypeStruct(w.shape, BF16) for w in cast],
        compiler_params=_params(("arbitrary",) if cast else ("parallel",)),
        name="ffn",
    )(x, pre_g, w_in, w_out, post_g, *cast)
    return (out[0], out[1:]) if cast else out[0]


def _rope_tables(pos):
    half = ATT_DH // 2
    inv = ROPE_THETA ** (-jnp.arange(half, dtype=F32) / half)
    ang = pos.astype(F32)[:, None] * inv[None, :]
    cos, sin = jnp.cos(ang), jnp.sin(ang)
    cos_t = jnp.tile(jnp.concatenate([cos, cos], axis=-1), (1, LANES // ATT_DH))
    sin_t = jnp.tile(jnp.concatenate([-sin, sin], axis=-1), (1, LANES // ATT_DH))
    return cos_t, sin_t


def _rope(x, cos_t, sin_t):
    n = x.shape[-1]
    half = ATT_DH // 2
    lane = lax.broadcasted_iota(jnp.int32, x.shape, 1)
    swapped = jnp.where(lane % ATT_DH < half, pltpu.roll(x, n - half, 1), pltpu.roll(x, half, 1))
    return x * cos_t + swapped * sin_t


def _mixproj_body(v_cols, h_ref, g_ref, w_ref, cos_ref, sin_ref, q_ref, k_ref, kb_ref, v_ref, vb_ref, rw_ref,
                  gate_ref):
    u = _rms(h_ref[...], g_ref[...]).astype(BF16)
    z = _dot(u, w_ref[...])
    cos_t = jnp.tile(cos_ref[...], (1, ATT_W // LANES))
    sin_t = jnp.tile(sin_ref[...], (1, ATT_W // LANES))
    q = _rope(z[:, :ATT_W], cos_t, sin_t) * Q_SCALE
    if v_cols:
        q_ref[0] = q.T.astype(BF16)
    else:
        q_ref[...] = q.astype(BF16)
    k = _rope(z[:, ATT_W:2 * ATT_W], cos_t, sin_t)
    kb_ref[...] = k.astype(BF16)
    v = z[:, 2 * ATT_W:3 * ATT_W]
    if v_cols:
        k_ref[0] = k.T
        hw = 2 * ATT_DH
        for h in range(ATT_HEADS):
            v_ref[pl.ds(h, v.shape[0], stride=ATT_HEADS), :] = v[:, h * hw:(h + 1) * hw]
        vb_ref[0] = v.T.astype(BF16)
    else:
        k_ref[...] = k
        v_ref[...] = v
        vb_ref[...] = v.astype(BF16)
    rw_ref[...] = z[:, 3 * ATT_W:3 * ATT_W + SHIFT_W]
    gate_ref[...] = jax.nn.sigmoid(z[:, 3 * ATT_W + SHIFT_W:]).astype(BF16)


def _mixproj_call(h, mix_pre_g, w_in, cos_t, sin_t, tm, v_cols_t):
    m = h.shape[0]
    ntab = cos_t.shape[0] // tm
    row = lambda w: pl.BlockSpec((tm, w), lambda i: (i, 0))
    tab = pl.BlockSpec((tm, LANES), lambda i: (i % ntab, 0))
    sds = lambda w, dt: jax.ShapeDtypeStruct((m, w), dt)
    if v_cols_t is None:
        q_spec, q_shape = row(ATT_W), sds(ATT_W, BF16)
        k_spec, k_shape = row(ATT_W), sds(ATT_W, F32)
        v_spec, v_shape = row(ATT_W), sds(ATT_W, F32)
        vb_spec, vb_shape = row(ATT_W), sds(ATT_W, BF16)
    else:
        nt = v_cols_t // tm
        q_spec = k_spec = vb_spec = pl.BlockSpec((1, ATT_W, tm), lambda i: (i // nt, 0, i % nt))
        k_shape = jax.ShapeDtypeStruct((m // v_cols_t, ATT_W, v_cols_t), F32)
        q_shape = vb_shape = jax.ShapeDtypeStruct((m // v_cols_t, ATT_W, v_cols_t), BF16)
        v_spec = pl.BlockSpec((tm * ATT_HEADS, 2 * ATT_DH), lambda i: (i, 0))
        v_shape = jax.ShapeDtypeStruct((m * ATT_HEADS, 2 * ATT_DH), F32)
    return pl.pallas_call(
        functools.partial(_mixproj_body, v_cols_t is not None),
        grid=(m // tm,),
        in_specs=[row(D_MODEL), _const_spec((1, D_MODEL)), _const_spec((D_MODEL, IN_W)), tab, tab],
        out_specs=[q_spec, k_spec, row(ATT_W), v_spec, vb_spec, row(SHIFT_W), row(GATE_W)],
        out_shape=[q_shape, k_shape, sds(ATT_W, BF16), v_shape, vb_shape,
                   sds(SHIFT_W, F32), sds(GATE_W, BF16)],
        compiler_params=_params(("parallel",)),
        name="mixproj",
    )(h, mix_pre_g, w_in, cos_t, sin_t)


def _lambda(lq1_ref, lk1_ref, lq2_ref, lk2_ref, lam_init):
    s1 = jnp.sum(lq1_ref[...] * lk1_ref[...], axis=-1, keepdims=True)
    s2 = jnp.sum(lq2_ref[...] * lk2_ref[...], axis=-1, keepdims=True)
    return jnp.exp(s1) - jnp.exp(s2) + lam_init


def _subln(o, g, lam_init):
    return o * lax.rsqrt(jnp.mean(o * o, axis=-1, keepdims=True) + EPS) * g * (1.0 - lam_init)


def _attn_prompt_body(lam_init, blk, qt_ref, k_ref, vt_ref, lq1_ref, lk1_ref, lq2_ref, lk2_ref, g_ref, o_ref,
                      qq_ref, m_ref, l_ref, acc_ref, st_ref):
    i = pl.program_id(2)
    hw = 2 * ATT_DH
    heads = range(ATT_HPS)
    for h in heads:
        qt = qt_ref[0, h * hw:(h + 1) * hw, :]
        feat = lax.broadcasted_iota(jnp.int32, qt.shape, 0)
        zero = jnp.zeros_like(qt)
        qq_ref[h, :, :blk] = jnp.where(feat < ATT_DH, qt, zero)
        qq_ref[h, :, blk:] = jnp.where(feat >= ATT_DH, qt, zero)
    m_ref[...] = jnp.full(m_ref.shape, -jnp.inf, F32)
    l_ref[...] = jnp.zeros(l_ref.shape, F32)
    acc_ref[...] = jnp.zeros(acc_ref.shape, F32)

    def scores(h, j):
        start = pl.multiple_of(j * blk, blk)
        st_ref[h] = _dot(k_ref[0, pl.ds(start, blk), h * hw:(h + 1) * hw], qq_ref[h])

    ones = jnp.ones((BF16_ROWS, blk), BF16)

    def consume(h, j, diagonal):
        start = pl.multiple_of(j * blk, blk)
        st = st_ref[h]
        if diagonal:
            key = lax.broadcasted_iota(jnp.int32, st.shape, 0)
            qry = lax.broadcasted_iota(jnp.int32, st.shape, 1) % blk
            st = jnp.where(key // CHUNK <= qry // CHUNK, st, -jnp.inf)
        m_prev = m_ref[h]
        m_new = jnp.maximum(m_prev, jnp.max(st, axis=0, keepdims=True))
        alpha = jnp.exp2(m_prev - m_new)
        p = jnp.exp2(st - m_new).astype(BF16)
        lhs = jnp.concatenate([vt_ref[0, h * hw:(h + 1) * hw, pl.ds(start, blk)], ones], axis=0)
        pv = _dot(lhs, p)
        l_ref[h] = alpha * l_ref[h] + pv[hw:hw + 1]
        acc_ref[h] = alpha * acc_ref[h] + pv[:hw]
        m_ref[h] = m_new

    def block(j, carry):
        scores(1, j)
        consume(0, j, False)
        scores(0, j + 1)
        consume(1, j, False)
        return carry

    scores(0, 0)
    lax.fori_loop(0, i, block, 0)
    scores(1, i)
    consume(0, i, True)
    consume(1, i, True)

    lam = _lambda(lq1_ref, lk1_ref, lq2_ref, lk2_ref, lam_init)
    for h in heads:
        o = acc_ref[h] / l_ref[h]
        o = o[:, :blk] - lam * o[:, blk:]
        o = o * lax.rsqrt(jnp.mean(o * o, axis=0, keepdims=True) + EPS) * g_ref[...] * (1.0 - lam_init)
        o_ref[0, :, h * hw:(h + 1) * hw] = o.T.astype(o_ref.dtype)


def _attn_prompt_call(qt, k, vt, lq1, lk1, lq2, lk2, subln_g, lam_init, blk):
    b, t, _ = k.shape
    hw = 2 * ATT_DH
    gw = ATT_HPS * hw
    qtspec = pl.BlockSpec((1, gw, blk), lambda bi, hi, i: (bi, hi, i))
    kspec = pl.BlockSpec((1, t, gw), lambda bi, hi, i: (bi, 0, hi))
    vtspec = pl.BlockSpec((1, gw, t), lambda bi, hi, i: (bi, hi, 0))
    return pl.pallas_call(
        functools.partial(_attn_prompt_body, lam_init, blk),
        grid=(b, ATT_HEADS // ATT_HPS, t // blk),
        in_specs=[qtspec, kspec, vtspec] + [_const_spec((1, ATT_DH))] * 4 + [_const_spec((hw, 1))],
        out_specs=pl.BlockSpec((1, blk, gw), lambda bi, hi, i: (bi, i, hi)),
        out_shape=jax.ShapeDtypeStruct((b, t, ATT_W), BF16),
        scratch_shapes=[pltpu.VMEM((ATT_HPS, hw, 2 * blk), BF16), pltpu.VMEM((ATT_HPS, 1, 2 * blk), F32),
                        pltpu.VMEM((ATT_HPS, 1, 2 * blk), F32), pltpu.VMEM((ATT_HPS, hw, 2 * blk), F32),
                        pltpu.VMEM((ATT_HPS, blk, 2 * blk), F32)],
        compiler_params=_params(("parallel", "parallel", "arbitrary")),
        name="attn_prompt",
    )(qt, k, vt, lq1, lk1, lq2, lk2, subln_g.reshape(hw, 1))


def _attn_sample_body(lam_init, ts, q_ref, ckt_ref, cv_ref, kn_ref, vn_ref, lq1_ref, lk1_ref, lq2_ref, lk2_ref,
                      g_ref, o_ref, qq_ref, m_ref, l_ref, acc_ref):
    j = pl.program_id(1)
    hw = 2 * ATT_DH
    heads = range(ATT_HEADS)

    @pl.when(j == 0)
    def _():
        for h in heads:
            q = q_ref[0, :, h * hw:(h + 1) * hw]
            lane = lax.broadcasted_iota(jnp.int32, q.shape, 1)
            zero = jnp.zeros_like(q)
            qq_ref[h, :ts, :] = jnp.where(lane < ATT_DH, q, zero)
            qq_ref[h, ts:, :] = jnp.where(lane >= ATT_DH, q, zero)
        m_ref[...] = jnp.full(m_ref.shape, -jnp.inf, F32)
        l_ref[...] = jnp.zeros(l_ref.shape, F32)
        acc_ref[...] = jnp.zeros(acc_ref.shape, F32)

    def softmax_step(ss, vs):
        m_prev = [m_ref[h] for h in heads]
        m_new = [jnp.maximum(m, jnp.max(s, axis=-1, keepdims=True)) for m, s in zip(m_prev, ss)]
        alpha = [jnp.exp2(m - mn) for m, mn in zip(m_prev, m_new)]
        p = [jnp.exp2(s - mn) for s, mn in zip(ss, m_new)]
        pv = [_dot(x.astype(BF16), v) for x, v in zip(p, vs)]
        for h in heads:
            l_ref[h] = alpha[h] * l_ref[h] + jnp.sum(p[h], axis=-1, keepdims=True)
            acc_ref[h] = alpha[h] * acc_ref[h] + pv[h]
            m_ref[h] = m_new[h]

    tk = ckt_ref.shape[2]
    softmax_step([_dot(qq_ref[h], ckt_ref[0, h * hw:(h + 1) * hw, :].astype(BF16)) for h in heads],
                 [cv_ref[0, pl.ds(h, tk, stride=ATT_HEADS), :].astype(BF16) for h in heads])

    @pl.when(j == pl.num_programs(1) - 1)
    def _():
        lam = _lambda(lq1_ref, lk1_ref, lq2_ref, lk2_ref, lam_init)
        cols = [slice(h * hw, (h + 1) * hw) for h in heads]
        softmax_step([_dot_nt(qq_ref[h], kn_ref[0, :, cols[h]]) for h in heads],
                     [vn_ref[0, :, cols[h]] for h in heads])
        for h in heads:
            o = acc_ref[h] / l_ref[h]
            o_ref[0, :, cols[h]] = _subln(o[:ts] - lam * o[ts:], g_ref[...], lam_init).astype(o_ref.dtype)


def _attn_sample_call(q, cache_kt, cache_v, k_new, v_new, lq1, lk1, lq2, lk2, subln_g, lam_init, tk):
    b, ts, _ = q.shape
    past = cache_kt.shape[2]
    hw = 2 * ATT_DH
    new = pl.BlockSpec((1, ts, ATT_W), lambda bi, j: (bi, 0, 0))
    return pl.pallas_call(
        functools.partial(_attn_sample_body, lam_init, ts),
        grid=(b, past // tk),
        in_specs=[new, pl.BlockSpec((1, ATT_W, tk), lambda bi, j: (bi, 0, j)),
                  pl.BlockSpec((1, tk * ATT_HEADS, hw), lambda bi, j: (bi, j, 0)), new, new]
        + [_const_spec((1, ATT_DH))] * 4 + [_const_spec((1, hw))],
        out_specs=new,
        out_shape=jax.ShapeDtypeStruct((b, ts, ATT_W), BF16),
        scratch_shapes=[pltpu.VMEM((ATT_HEADS, 2 * ts, hw), BF16), pltpu.VMEM((ATT_HEADS, 2 * ts, 1), F32),
                        pltpu.VMEM((ATT_HEADS, 2 * ts, 1), F32), pltpu.VMEM((ATT_HEADS, 2 * ts, hw), F32)],
        compiler_params=_params(("parallel", "arbitrary")),
        name="attn_sample",
    )(q, cache_kt, cache_v, k_new, v_new, lq1, lk1, lq2, lk2, subln_g)


def _split2(x):
    hi = x.astype(BF16)
    return hi, (x - hi.astype(F32)).astype(BF16)


def _head_sum(x, ones_bd, passes=2):
    parts = []
    for g in range(x.shape[1] // MXU_W):
        xg = x[:, g * MXU_W:(g + 1) * MXU_W]
        if passes == 1:
            parts.append(_dot(xg.astype(BF16), ones_bd))
        else:
            hi, lo = _split2(xg)
            parts.append(_dot(hi, ones_bd) + _dot(lo, ones_bd))
    return jnp.concatenate(parts, axis=1)


def _bd_expand(x):
    c = x.shape[0]
    xt = jnp.concatenate([x] * WKV_GROUP, axis=0)
    row = lax.broadcasted_iota(jnp.int32, xt.shape, 0)
    lane = lax.broadcasted_iota(jnp.int32, xt.shape, 1)
    return jnp.where(row // c == lane // RW_HEAD, xt, jnp.zeros_like(xt))


def _wkv_chunk_terms(units):
    c = units[0][0].shape[0]
    slab = (c, WKV_GROUP * c)
    t_idx = lax.broadcasted_iota(jnp.int32, slab, 0)
    s_idx = lax.broadcasted_iota(jnp.int32, slab, 1) % c
    strict = s_idx < t_idx
    incl = s_idx <= t_idx
    eye = jnp.where(s_idx == t_idx, 1.0, 0.0)
    expand = lambda x: _bd_expand(x).astype(BF16)

    ar = [jnp.concatenate([u[0], u[1]], axis=0).astype(BF16) for u in units]
    sb = [_dot_nt(a, expand(u[2])) for a, u in zip(ar, units)]
    yield
    sk = [_dot_nt(a, expand(u[3])) for a, u in zip(ar, units)]
    yield
    l_ab = [jnp.where(strict, x[:c], 0.0) for x in sb]
    m_rb = [jnp.where(incl, x[c:], 0.0).astype(BF16) for x in sb]
    l_ak = [jnp.where(strict, x[:c], 0.0).astype(BF16) for x in sk]
    m_rk = [jnp.where(incl, x[c:], 0.0).astype(BF16) for x in sk]

    lp = l_ab
    tinv = [eye + x for x in l_ab]
    lp = [_dot(x.astype(BF16), expand(x)) for x in lp]
    yield
    power = 2
    while power < c:
        last = 2 * power >= c
        nxt = []
        for i, (x, t) in enumerate(zip(lp, tinv)):
            lhs = t if last else jnp.concatenate([x, t], axis=0)
            prod = _dot(lhs.astype(BF16), expand(x))
            if last:
                tinv[i] = t + prod
            else:
                nxt.append(prod[:c])
                tinv[i] = t + prod[c:]
        lp = nxt
        power *= 2
        yield

    v_exp = [expand(u[4]) for u in units]
    lakv = [_dot(x, ve) for x, ve in zip(l_ak, v_exp)]
    yield
    tinv_bf = [t.astype(BF16) for t in tinv]
    ta = [_dot(t, expand(u[0])) for t, u in zip(tinv_bf, units)]
    yield
    uv = [_dot(t, expand(x)) for t, x in zip(tinv_bf, lakv)]
    yield

    tr = [jnp.concatenate([x, u[1]], axis=0).astype(BF16) for x, u in zip(ta, units)]
    m_rbk = [jnp.concatenate([m, mk], axis=1) for m, mk in zip(m_rb, m_rk)]
    return list(zip(tr, uv, m_rbk, v_exp))


def _wkv_chunk_apply(states, terms, vs, bkhs, decays):
    c = vs[0].shape[0]
    gw = states[0].shape[0]
    bd = (lax.broadcasted_iota(jnp.int32, (gw, gw), 0) // RW_HEAD
          == lax.broadcasted_iota(jnp.int32, (gw, gw), 1) // RW_HEAD)
    ur = [_dot_nt(t[0], s.astype(BF16)) for t, s in zip(terms, states)]
    yield
    u = [x[:c] + t[1] for x, t in zip(ur, terms)]
    upd = [_dot_tn(jnp.concatenate([x, v], axis=0).astype(BF16), bkh) for x, v, bkh in zip(u, vs, bkhs)]
    yield
    ys = [x[c:] + _dot(t[2], jnp.concatenate([_bd_expand(uu).astype(BF16), t[3]], axis=0))
          for x, uu, t in zip(ur, u, terms)]
    yield
    return ys, [s * d + jnp.where(bd, x, 0.0) for s, d, x in zip(states, decays, upd)]


def _interleave(*gens):
    results = [None] * len(gens)
    live = list(range(len(gens)))
    while live:
        for i in list(live):
            try:
                next(gens[i])
            except StopIteration as stop:
                results[i] = stop.value
                live.remove(i)
    return results


def _rwkv_body(bb, tb, t_valid, z_ref, s0_ref, sh0_ref, mu_ref, w0_ref, wa_ref, a0_ref, g2_ref, kk_ref, ka_ref,
               rk_ref, lng_ref, lnb_ref, o_ref, s_ref, h_ref, carry_ref):
    ib = pl.program_id(1)
    c = WKV_C
    nc = tb // c
    lanes_of = lambda g: slice(g * WKV_GW, (g + 1) * WKV_GW)

    @pl.when(ib == 0)
    def _():
        for bi in range(bb):
            carry_ref[bi] = sh0_ref[bi]
            for g in range(WKV_NG):
                blocks = [s0_ref[bi, g * WKV_GROUP + h] for h in range(WKV_GROUP)]
                h_ref[bi, g] = _bd_expand(jnp.concatenate(blocks, axis=1))

    ones_bd = (lax.broadcasted_iota(jnp.int32, (MXU_W, MXU_W), 0) // RW_HEAD
               == lax.broadcasted_iota(jnp.int32, (MXU_W, MXU_W), 1) // RW_HEAD).astype(BF16)

    def prepare(seqs):
        n = len(seqs) * tb
        z = jnp.concatenate([z_ref[bi] for bi in seqs], axis=0)
        mu = mu_ref[...]
        zs = z + (pltpu.roll(z, 1, 0) - z) * mu
        first = lax.broadcasted_iota(jnp.int32, (F32_ROWS, SHIFT_W), 0) == 0
        pieces = []
        for i, bi in enumerate(seqs):
            head = z[i * tb:i * tb + F32_ROWS]
            pieces += [jnp.where(first, head + (carry_ref[bi] - head) * mu, zs[i * tb:i * tb + F32_ROWS]),
                       zs[i * tb + F32_ROWS:(i + 1) * tb]]
            carry_ref[bi] = z[(i + 1) * tb - 1:(i + 1) * tb, :]
        zs = jnp.concatenate(pieces, axis=0)
        yield

        r = zs[:, :RW_W]
        k = zs[:, RW_W:2 * RW_W]
        v = zs[:, 2 * RW_W:3 * RW_W]
        zwa = zs[:, 3 * RW_W:3 * RW_W + W_LORA + A_LORA]
        zg = zs[:, 3 * RW_W + W_LORA + A_LORA:]
        lane = lax.broadcasted_iota(jnp.int32, zwa.shape, 1)
        lora = _dot(jnp.where(lane < W_LORA, jnp.tanh(zwa), zwa).astype(BF16), wa_ref[...])
        ew = math.exp(-0.5) * jax.nn.sigmoid(w0_ref[...] + lora[:, :RW_W])
        a = jax.nn.sigmoid(a0_ref[...] + lora[:, RW_W:])
        yield
        gate = _dot(jax.nn.sigmoid(zg).astype(BF16), g2_ref[...])
        yield

        kk = k * kk_ref[...]
        kk = kk * lax.rsqrt(jnp.maximum(_head_sum(kk * kk, ones_bd, passes=1), 1e-24))
        yield
        k = k * (1.0 + (a - 1.0) * ka_ref[...])
        bonus = _head_sum(r * k * rk_ref[...], ones_bd) * v
        yield

        if t_valid is not None:
            live = lax.broadcasted_iota(jnp.int32, ew.shape, 0) % tb + ib * tb < t_valid
            ew = jnp.where(live, ew, 0.0)
            k = jnp.where(live, k, 0.0)
            kk = jnp.where(live, kk, 0.0)
            v = jnp.where(live, v, 0.0)

        rows = min(n, MXU_W)
        ti = lax.broadcasted_iota(jnp.int32, (rows, rows), 0)
        si = lax.broadcasted_iota(jnp.int32, (rows, rows), 1)
        tri = jnp.logical_and(si <= ti, si // c == ti // c).astype(BF16)
        e_hi, e_lo = _split2(ew)
        cum = -jnp.concatenate([_dot(tri, e_hi[i0:i0 + rows]) + _dot(tri, e_lo[i0:i0 + rows])
                                for i0 in range(0, n, rows)], axis=0)
        gamma_end = jnp.concatenate([jnp.broadcast_to(jnp.exp(cum[i0 + c - 1:i0 + c, :]), (c, RW_W))
                                     for i0 in range(0, n, c)], axis=0)
        yield
        grow = jnp.exp(-cum)
        b = kk * a
        at = -kk * jnp.exp(cum + ew)
        rt = r / grow
        yield
        bt, kt = b * grow, k * grow
        pre = dict(at=at, rt=rt, bt=bt, kt=kt, v=v, gate=gate, bonus=bonus)
        yield
        bh, kh = bt * gamma_end, kt * gamma_end
        for ci in range(nc):
            for i in range(len(seqs)):
                sl = rows_of(ci, i)
                for g in range(WKV_NG):
                    gl = lanes_of(g)
                    pre[ci, i, g] = (v[sl, gl], jnp.concatenate([bh[sl, gl], kh[sl, gl]], axis=0).astype(BF16),
                                     gamma_end[sl, gl][:1])
        return pre

    rows_of = lambda ci, i: slice(i * tb + ci * c, i * tb + (ci + 1) * c)

    def chunk_terms(pre, nseq):
        order = [(ci, i, g) for ci in range(nc) for i in range(nseq) for g in range(WKV_NG)]
        units = [tuple(pre[name][rows_of(ci, i), lanes_of(g)] for name in ("at", "rt", "bt", "kt", "v"))
                 for ci, i, g in order]
        return dict(zip(order, (yield from _wkv_chunk_terms(units))))

    def apply_chunks(seqs, pre, terms):
        chains = [(i, g) for i in range(len(seqs)) for g in range(WKV_NG)]
        states = [h_ref[seqs[i], g] for i, g in chains]
        y_parts = {}
        for ci in range(nc):
            extra = [pre[ci, i, g] for i, g in chains]
            ys, states = yield from _wkv_chunk_apply(states, [terms[ci, i, g] for i, g in chains],
                                                     [e[0] for e in extra], [e[1] for e in extra],
                                                     [e[2] for e in extra])
            for (i, g), y_unit in zip(chains, ys):
                y_parts[ci, i, g] = y_unit
        for (i, g), s_out in zip(chains, states):
            h_ref[seqs[i], g] = s_out
        return jnp.concatenate([jnp.concatenate([y_parts[ci, i, g] for g in range(WKV_NG)], axis=1)
                                for i in range(len(seqs)) for ci in range(nc)], axis=0)

    def finish(seqs, pre, y):
        mean = _head_sum(y, ones_bd) * (1.0 / RW_HEAD)
        yield
        d = y - mean
        var = _head_sum(d * d, ones_bd, passes=1) * (1.0 / RW_HEAD)
        yield
        yn = d * lax.rsqrt(var + LNX_EPS) * lng_ref[...] + lnb_ref[...]
        out = ((yn + pre["bonus"]) * pre["gate"]).astype(o_ref.dtype)
        for i, bi in enumerate(seqs):
            o_ref[bi] = out[i * tb:(i + 1) * tb]

    halves = [list(range(bb))] if bb == 1 else [list(range(bb // 2)), list(range(bb // 2, bb))]
    first, second = halves[0], halves[-1]
    pre_a, = _interleave(prepare(first))
    if len(halves) == 1:
        terms_a, = _interleave(chunk_terms(pre_a, len(first)))
        y_a, = _interleave(apply_chunks(first, pre_a, terms_a))
        _interleave(finish(first, pre_a, y_a))
    else:
        terms_a, pre_b = _interleave(chunk_terms(pre_a, len(first)), prepare(second))
        y_a, terms_b = _interleave(apply_chunks(first, pre_a, terms_a), chunk_terms(pre_b, len(second)))
        y_b, _ = _interleave(apply_chunks(second, pre_b, terms_b), finish(first, pre_a, y_a))
        _interleave(finish(second, pre_b, y_b))

    @pl.when(ib == pl.num_programs(1) - 1)
    def _():
        for bi in range(bb):
            for g in range(WKV_NG):
                h = h_ref[bi, g]
                for hh in range(WKV_GROUP):
                    blk = h[hh * RW_HEAD:(hh + 1) * RW_HEAD, hh * RW_HEAD:(hh + 1) * RW_HEAD]
                    s_ref[bi, g * WKV_GROUP + hh] = blk


def _rwkv_call(z_rw, s0, shift0, p, bb, tb, t_valid):
    b, t, _ = z_rw.shape
    vec = lambda w: _const_spec((1, w))
    return pl.pallas_call(
        functools.partial(_rwkv_body, bb, tb, t_valid),
        grid=(b // bb, t // tb),
        in_specs=[pl.BlockSpec((bb, tb, SHIFT_W), lambda bi, i: (bi, i, 0)),
                  pl.BlockSpec((bb, RW_HEADS, RW_HEAD, RW_HEAD), lambda bi, i: (bi, 0, 0, 0)),
                  pl.BlockSpec((bb, 1, SHIFT_W), lambda bi, i: (bi, 0, 0)),
                  vec(SHIFT_W), vec(RW_W), _const_spec((W_LORA + A_LORA, 2 * RW_W)), vec(RW_W),
                  _const_spec((G_LORA, RW_W)), vec(RW_W), vec(RW_W), vec(RW_W), vec(RW_W), vec(RW_W)],
        out_specs=[pl.BlockSpec((bb, tb, RW_W), lambda bi, i: (bi, i, 0)),
                   pl.BlockSpec((bb, RW_HEADS, RW_HEAD, RW_HEAD), lambda bi, i: (bi, 0, 0, 0))],
        out_shape=[jax.ShapeDtypeStruct((b, t, RW_W), BF16),
                   jax.ShapeDtypeStruct((b, RW_HEADS, RW_HEAD, RW_HEAD), F32)],
        scratch_shapes=[pltpu.VMEM((bb, WKV_NG, WKV_GW, WKV_GW), F32), pltpu.VMEM((bb, 1, SHIFT_W), F32)],
        compiler_params=_params(("parallel", "arbitrary")),
        name="rwkv",
    )(z_rw, s0, shift0, p["mu"], p["w0"], p["wa"], p["a0"], p["g2"], p["k_k"], p["k_a"], p["r_k"],
      p["lnx_g"], p["lnx_b"])


def _merge_body(h_ref, oa_ref, orw_ref, gate_ref, woa_ref, worw_ref, wout_ref, postg_ref, pre_ref, win_ref,
                wo2_ref, post2_ref, y_ref):
    gates = gate_ref[...]
    merged = (gates[:, :D_MODEL] * _dot(oa_ref[...], woa_ref[...])
              + gates[:, D_MODEL:] * _dot(orw_ref[...], worw_ref[...]))
    h2 = h_ref[...] + _rms(_dot(merged.astype(BF16), wout_ref[...]), postg_ref[...])
    f = _swiglu_ffn(_rms(h2, pre_ref[...]).astype(BF16), win_ref, wo2_ref)
    y_ref[...] = h2 + MACARON * _rms(f, post2_ref[...])


def _merge_call(h, o_att, o_rw, gates, w_o_att, w_o_rwkv, w_out, mix_post_g, pre_g, w_in, w_out2, post_g, tm):
    m = h.shape[0]
    row = lambda w: pl.BlockSpec((tm, w), lambda i: (i, 0))
    vec = _const_spec((1, D_MODEL))
    return pl.pallas_call(
        _merge_body,
        grid=(m // tm,),
        in_specs=[row(D_MODEL), row(ATT_W), row(RW_W), row(GATE_W),
                  _const_spec((ATT_W, D_MODEL)), _const_spec((RW_W, D_MODEL)), _const_spec((D_MODEL, D_MODEL)), vec,
                  vec, _const_spec((D_MODEL, 2 * D_FF)), _const_spec((D_FF, D_MODEL)), vec],
        out_specs=row(D_MODEL),
        out_shape=jax.ShapeDtypeStruct((m, D_MODEL), F32),
        compiler_params=_params(("parallel",)),
        name="merge_ffn",
    )(h, o_att, o_rw, gates, w_o_att, w_o_rwkv, w_out, mix_post_g, pre_g, w_in, w_out2, post_g)


def _pick_tile(m, want):
    t = min(m, want)
    assert m % t == 0
    return t


def _layer(x, pos, l, p, cache_k, cache_v, s0, shift0):
    b, t, _ = x.shape
    m = b * t
    tm = _pick_tile(m, 512)
    lam_init = 0.8 - 0.6 * math.exp(-0.3 * l)

    late = p.get("late_f32")
    h = _ffn_call(x.reshape(m, D_MODEL), p["ffn1_pre_g"], p["ffn1_w_in"], p["ffn1_w_out"], p["ffn1_post_g"],
                  _pick_tile(m, 512), cast=tuple(late.values()) if late else ())
    if late:
        h, casted = h
        p = {**p, **dict(zip(late, casted)), "late_f32": None}

    cos_t, sin_t = _rope_tables(pos)
    tmp = _pick_tile(m, 512)
    if t < tmp:
        cos_t, sin_t = jnp.tile(cos_t, (tmp // t, 1)), jnp.tile(sin_t, (tmp // t, 1))
    q, k, k_bf, v, v_bf, z_rw, gates = _mixproj_call(h, p["mix_pre_g"], p["w_in"], cos_t, sin_t, tmp,
                                                     t if cache_k is None else None)

    lam_args = (p["att_lambda_q1"], p["att_lambda_k1"], p["att_lambda_q2"], p["att_lambda_k2"], p["att_subln_g"])
    r3 = lambda a: a.reshape(b, t, a.shape[-1])
    if cache_k is None:
        o_att = _attn_prompt_call(q, r3(k_bf), v_bf, *lam_args, lam_init, _pick_tile(t, 512))
    else:
        past = cache_k.shape[1]
        cache_kt = cache_k.reshape(b, past, ATT_W).transpose(0, 2, 1)
        cache_vr = cache_v.reshape(b, past * ATT_HEADS, 2 * ATT_DH)
        o_att = _attn_sample_call(r3(q), cache_kt, cache_vr, r3(k_bf), r3(v_bf), *lam_args, lam_init,
                                  _pick_tile(past, 4096))

    z_rw3 = r3(z_rw)
    t_pad = -(-t // WKV_C) * WKV_C
    z_in = z_rw3 if t_pad == t else jnp.pad(z_rw3, ((0, 0), (0, t_pad - t), (0, 0)))
    o_rw, s_new = _rwkv_call(z_in, s0, shift0, p["rwkv"], _pick_tile(b, WKV_SEQS), _pick_tile(t_pad, WKV_TB),
                             None if t_pad == t else t)
    o_rw = o_rw[:, :t]

    y = _merge_call(h, o_att.reshape(m, ATT_W), o_rw.reshape(m, RW_W), gates, p["w_o_att"], p["w_o_rwkv"],
                    p["w_out"], p["mix_post_g"], p["ffn2_pre_g"], p["ffn2_w_in"], p["ffn2_w_out"],
                    p["ffn2_post_g"], tm)
    if cache_k is None:
        k_rows = k.reshape(b, ATT_HEADS, 2, ATT_DH, t).transpose(0, 4, 1, 2, 3)
    else:
        k_rows = k.reshape(b, t, ATT_HEADS, 2, ATT_DH)
    return (y.reshape(b, t, D_MODEL), k_rows, v.reshape(b, t, ATT_HEADS, 2 * ATT_DH), s_new, z_rw3[:, -1:]), p


def kernel(x_prompt, x_sample, cache_att_k, cache_att_v, state_rwkv, state_shift, ffn1_pre_g, ffn1_w_in, ffn1_w_out, ffn1_post_g, mix_pre_g, w_in, att_lambda_q1, att_lambda_k1, att_lambda_q2, att_lambda_k2, att_subln_g, rwkv_mu, rwkv_w0, rwkv_w2, rwkv_a0, rwkv_a2, rwkv_g2, rwkv_k_k, rwkv_k_a, rwkv_r_k, rwkv_lnx_g, rwkv_lnx_b, w_o_att, w_o_rwkv, w_out, mix_post_g, ffn2_pre_g, ffn2_w_in, ffn2_w_out, ffn2_post_g):
    depth = w_in.shape[0]
    bp, tp, _ = x_prompt.shape
    bs, ts, _ = x_sample.shape
    past = cache_att_k.shape[2]
    pos_p = jnp.arange(tp)
    pos_s = past + jnp.arange(ts)
    xp, xs = x_prompt, x_sample
    outs_p, outs_s = [], []
    vec = lambda a: a.reshape(1, -1)
    for l in range(depth):
        zeros = jnp.zeros((W_LORA, RW_W), F32)
        wa = jnp.concatenate([jnp.concatenate([rwkv_w2[l], zeros], axis=1),
                              jnp.concatenate([zeros, rwkv_a2[l]], axis=1)], axis=0)
        p = dict(
            ffn1_pre_g=vec(ffn1_pre_g[l]), ffn1_w_in=ffn1_w_in[l].astype(BF16), ffn1_w_out=ffn1_w_out[l].astype(BF16),
            ffn1_post_g=vec(ffn1_post_g[l]), mix_pre_g=vec(mix_pre_g[l]),
            late_f32=dict(w_in=w_in[l], w_o_att=w_o_att[l], w_o_rwkv=w_o_rwkv[l], w_out=w_out[l],
                          ffn2_w_in=ffn2_w_in[l], ffn2_w_out=ffn2_w_out[l]),
            att_lambda_q1=vec(att_lambda_q1[l]), att_lambda_k1=vec(att_lambda_k1[l]),
            att_lambda_q2=vec(att_lambda_q2[l]), att_lambda_k2=vec(att_lambda_k2[l]),
            att_subln_g=vec(att_subln_g[l]),
            rwkv=dict(mu=vec(rwkv_mu[l]), w0=vec(rwkv_w0[l]), wa=wa.astype(BF16), a0=vec(rwkv_a0[l]),
                      g2=rwkv_g2[l].astype(BF16), k_k=vec(rwkv_k_k[l]), k_a=vec(rwkv_k_a[l]), r_k=vec(rwkv_r_k[l]),
                      lnx_g=vec(rwkv_lnx_g[l]), lnx_b=vec(rwkv_lnx_b[l])),
            mix_post_g=vec(mix_post_g[l]), ffn2_pre_g=vec(ffn2_pre_g[l]), ffn2_post_g=vec(ffn2_post_g[l]),
        )
        s0p = jnp.zeros((bp, RW_HEADS, RW_HEAD, RW_HEAD), F32)
        sh0p = jnp.zeros((bp, 1, SHIFT_W), F32)
        (xp, *rest_p), p = _layer(xp, pos_p, l, p, None, None, s0p, sh0p)
        (xs, *rest_s), p = _layer(xs, pos_s, l, p, cache_att_k[l], cache_att_v[l], state_rwkv[l], state_shift[l])
        outs_p.append(rest_p)
        outs_s.append(rest_s)
    stack = lambda outs, i: jnp.stack([o[i] for o in outs], 0)
    return (xp, xs, stack(outs_p, 0), stack(outs_p, 1), stack(outs_p, 2), stack(outs_p, 3),
            stack(outs_s, 0), stack(outs_s, 1), stack(outs_s, 2), stack(outs_s, 3))
```

```python
import functools
import math

import jax
import jax.numpy as jnp
from jax import lax
from jax.experimental import pallas as pl
from jax.experimental.pallas import tpu as pltpu

F32 = jnp.float32
BF16 = jnp.bfloat16

D_MODEL = 1024
D_FF = 2816
CHUNK = 64
ROPE_THETA = 10000.0
EPS = 1e-6
MACARON = 0.5
ATT_HEADS = 4
ATT_DH = 64
ATT_W = ATT_HEADS * 2 * ATT_DH
ATT_HPS = 2
Q_SCALE = ATT_DH ** -0.5 * math.log2(math.e)
RW_HEAD = 64
RW_W = D_MODEL // 2
RW_HEADS = RW_W // RW_HEAD
W_LORA = 64
A_LORA = 64
G_LORA = 128
SHIFT_W = 3 * RW_W + W_LORA + A_LORA + G_LORA
LNX_EPS = 64e-5
GATE_W = 2 * D_MODEL
IN_W = 3 * ATT_W + SHIFT_W + GATE_W

LANES = 128
F32_ROWS = 8
BF16_ROWS = 16
MXU_W = 256
VMEM_LIMIT = 56 * 1024 * 1024

ROW_TILE = 512
ATT_BLK = 512
CACHE_BLK = 4096

WKV_C = 64
WKV_GROUP = MXU_W // RW_HEAD
WKV_GW = WKV_GROUP * RW_HEAD
WKV_NG = RW_W // WKV_GW
WKV_SEQS = 8
WKV_TB = 256
WKV_PIPE_GROUPS = 2


def _params(sem):
    return pltpu.CompilerParams(dimension_semantics=sem, vmem_limit_bytes=VMEM_LIMIT)


def _const_spec(shape):
    nd = len(shape)
    return pl.BlockSpec(shape, lambda *_: (0,) * nd, pipeline_mode=pl.Buffered(1))


def _rms(x, g):
    return x * lax.rsqrt(jnp.mean(x * x, axis=-1, keepdims=True) + EPS) * g


def _dot(a, b):
    return jnp.dot(a, b, preferred_element_type=F32)


def _dot_nt(a, b):
    return lax.dot_general(a, b, (((1,), (1,)), ((), ())), preferred_element_type=F32)


def _dot_tn(a, b):
    return lax.dot_general(a, b, (((0,), (0,)), ((), ())), preferred_element_type=F32)


def _swiglu_ffn(xn_bf, w_in_ref, w_out_ref):
    hh = _dot(xn_bf, w_in_ref[...])
    gate = hh[:, :D_FF]
    up = hh[:, D_FF:]
    act = (gate * jax.nn.sigmoid(gate) * up).astype(BF16)
    return _dot(act, w_out_ref[...])


def _ffn_body(cast_steps, x_ref, pre_ref, win_ref, wout_ref, post_ref, *refs):
    ncast = len(cast_steps)
    cast_in, o_ref, cast_out = refs[:ncast], refs[ncast], refs[ncast + 1:]
    x = x_ref[...]
    f = _swiglu_ffn(_rms(x, pre_ref[...]).astype(BF16), win_ref, wout_ref)
    o_ref[...] = x + MACARON * _rms(f, post_ref[...])
    for w_ref, c_ref, steps in zip(cast_in, cast_out, cast_steps):
        @pl.when(pl.program_id(0) < steps)
        def _():
            c_ref[...] = w_ref[...].astype(BF16)


def _cast_rows(rows, nsteps):
    return next(r for r in range(BF16_ROWS, rows + 1, BF16_ROWS) if rows % r == 0 and rows // r <= nsteps)


def _ffn_call(x, pre_g, w_in, w_out, post_g, tm, cast=()):
    m = x.shape[0]
    nsteps = m // tm
    row = lambda w: pl.BlockSpec((tm, w), lambda i: (i, 0))
    slabs = [_cast_rows(w.shape[0], nsteps) for w in cast]
    steps = tuple(w.shape[0] // r for w, r in zip(cast, slabs))
    slab_index = lambda i, last: (jnp.minimum(i, last), 0)
    cast_specs = [pl.BlockSpec((r, w.shape[1]), functools.partial(slab_index, last=s - 1))
                  for w, r, s in zip(cast, slabs, steps)]
    out = pl.pallas_call(
        functools.partial(_ffn_body, steps),
        grid=(nsteps,),
        in_specs=[row(D_MODEL), _const_spec((1, D_MODEL)), _const_spec((D_MODEL, 2 * D_FF)),
                  _const_spec((D_FF, D_MODEL)), _const_spec((1, D_MODEL))] + cast_specs,
        out_specs=[row(D_MODEL)] + cast_specs,
        out_shape=[jax.ShapeDtypeStruct((m, D_MODEL), F32)] + [jax.ShapeDtypeStruct(w.shape, BF16) for w in cast],
        compiler_params=_params(("arbitrary",) if cast else ("parallel",)),
        name="ffn",
    )(x, pre_g, w_in, w_out, post_g, *cast)
    return (out[0], out[1:]) if cast else out[0]


def _rope_tables(pos):
    half = ATT_DH // 2
    inv = ROPE_THETA ** (-jnp.arange(half, dtype=F32) / half)
    ang = pos.astype(F32)[:, None] * inv[None, :]
    cos, sin = jnp.cos(ang), jnp.sin(ang)
    cos_t = jnp.tile(jnp.concatenate([cos, cos], axis=-1), (1, LANES // ATT_DH))
    sin_t = jnp.tile(jnp.concatenate([-sin, sin], axis=-1), (1, LANES // ATT_DH))
    return cos_t, sin_t


def _rope(x, cos_t, sin_t):
    n = x.shape[-1]
    half = ATT_DH // 2
    lane = lax.broadcasted_iota(jnp.int32, x.shape, 1)
    swapped = jnp.where(lane % ATT_DH < half, pltpu.roll(x, n - half, 1), pltpu.roll(x, half, 1))
    return x * cos_t + swapped * sin_t


def _mixproj_body(v_cols, h_ref, g_ref, w_ref, cos_ref, sin_ref, q_ref, k_ref, kb_ref, v_ref, vb_ref, rw_ref,
                  gate_ref):
    u = _rms(h_ref[...], g_ref[...]).astype(BF16)
    z = _dot(u, w_ref[...])
    cos_t = jnp.tile(cos_ref[...], (1, ATT_W // LANES))
    sin_t = jnp.tile(sin_ref[...], (1, ATT_W // LANES))
    q = _rope(z[:, :ATT_W], cos_t, sin_t) * Q_SCALE
    if v_cols:
        q_ref[0] = q.T.astype(BF16)
    else:
        q_ref[...] = q.astype(BF16)
    k = _rope(z[:, ATT_W:2 * ATT_W], cos_t, sin_t)
    kb_ref[...] = k.astype(BF16)
    v = z[:, 2 * ATT_W:3 * ATT_W]
    if v_cols:
        k_ref[0] = k.T
        hw = 2 * ATT_DH
        for h in range(ATT_HEADS):
            v_ref[pl.ds(h, v.shape[0], stride=ATT_HEADS), :] = v[:, h * hw:(h + 1) * hw]
        vb_ref[0] = v.T.astype(BF16)
    else:
        k_ref[...] = k
        v_ref[...] = v
        vb_ref[...] = v.astype(BF16)
    rw_ref[...] = z[:, 3 * ATT_W:3 * ATT_W + SHIFT_W]
    gate_ref[...] = jax.nn.sigmoid(z[:, 3 * ATT_W + SHIFT_W:]).astype(BF16)


def _mixproj_call(h, mix_pre_g, w_in, cos_t, sin_t, tm, v_cols_t):
    m = h.shape[0]
    ntab = cos_t.shape[0] // tm
    row = lambda w: pl.BlockSpec((tm, w), lambda i: (i, 0))
    tab = pl.BlockSpec((tm, LANES), lambda i: (i % ntab, 0))
    sds = lambda w, dt: jax.ShapeDtypeStruct((m, w), dt)
    if v_cols_t is None:
        q_spec, q_shape = row(ATT_W), sds(ATT_W, BF16)
        k_spec, k_shape = row(ATT_W), sds(ATT_W, F32)
        v_spec, v_shape = row(ATT_W), sds(ATT_W, F32)
        vb_spec, vb_shape = row(ATT_W), sds(ATT_W, BF16)
    else:
        nt = v_cols_t // tm
        q_spec = k_spec = vb_spec = pl.BlockSpec((1, ATT_W, tm), lambda i: (i // nt, 0, i % nt))
        k_shape = jax.ShapeDtypeStruct((m // v_cols_t, ATT_W, v_cols_t), F32)
        q_shape = vb_shape = jax.ShapeDtypeStruct((m // v_cols_t, ATT_W, v_cols_t), BF16)
        v_spec = pl.BlockSpec((tm * ATT_HEADS, 2 * ATT_DH), lambda i: (i, 0))
        v_shape = jax.ShapeDtypeStruct((m * ATT_HEADS, 2 * ATT_DH), F32)
    return pl.pallas_call(
        functools.partial(_mixproj_body, v_cols_t is not None),
        grid=(m // tm,),
        in_specs=[row(D_MODEL), _const_spec((1, D_MODEL)), _const_spec((D_MODEL, IN_W)), tab, tab],
        out_specs=[q_spec, k_spec, row(ATT_W), v_spec, vb_spec, row(SHIFT_W), row(GATE_W)],
        out_shape=[q_shape, k_shape, sds(ATT_W, BF16), v_shape, vb_shape,
                   sds(SHIFT_W, F32), sds(GATE_W, BF16)],
        compiler_params=_params(("parallel",)),
        name="mixproj",
    )(h, mix_pre_g, w_in, cos_t, sin_t)


def _lambda(lq1_ref, lk1_ref, lq2_ref, lk2_ref, lam_init):
    s1 = jnp.sum(lq1_ref[...] * lk1_ref[...], axis=-1, keepdims=True)
    s2 = jnp.sum(lq2_ref[...] * lk2_ref[...], axis=-1, keepdims=True)
    return jnp.exp(s1) - jnp.exp(s2) + lam_init


def _subln(o, g, lam_init):
    return o * lax.rsqrt(jnp.mean(o * o, axis=-1, keepdims=True) + EPS) * g * (1.0 - lam_init)


def _attn_prompt_body(lam_init, blk, qt_ref, k_ref, vt_ref, lq1_ref, lk1_ref, lq2_ref, lk2_ref, g_ref, o_ref,
                      qq_ref, m_ref, l_ref, acc_ref, st_ref):
    i = pl.program_id(2)
    hw = 2 * ATT_DH
    heads = range(ATT_HPS)
    for h in heads:
        qt = qt_ref[0, h * hw:(h + 1) * hw, :]
        feat = lax.broadcasted_iota(jnp.int32, qt.shape, 0)
        zero = jnp.zeros_like(qt)
        qq_ref[h, :, :blk] = jnp.where(feat < ATT_DH, qt, zero)
        qq_ref[h, :, blk:] = jnp.where(feat >= ATT_DH, qt, zero)
    m_ref[...] = jnp.full(m_ref.shape, -jnp.inf, F32)
    l_ref[...] = jnp.zeros(l_ref.shape, F32)
    acc_ref[...] = jnp.zeros(acc_ref.shape, F32)

    def scores(h, j):
        start = pl.multiple_of(j * blk, blk)
        st_ref[h] = _dot(k_ref[0, pl.ds(start, blk), h * hw:(h + 1) * hw], qq_ref[h])

    ones = jnp.ones((BF16_ROWS, blk), BF16)

    def consume(h, j, diagonal):
        start = pl.multiple_of(j * blk, blk)
        st = st_ref[h]
        if diagonal:
            key = lax.broadcasted_iota(jnp.int32, st.shape, 0)
            qry = lax.broadcasted_iota(jnp.int32, st.shape, 1) % blk
            st = jnp.where(key // CHUNK <= qry // CHUNK, st, -jnp.inf)
        m_prev = m_ref[h]
        m_new = jnp.maximum(m_prev, jnp.max(st, axis=0, keepdims=True))
        alpha = jnp.exp2(m_prev - m_new)
        p = jnp.exp2(st - m_new).astype(BF16)
        lhs = jnp.concatenate([vt_ref[0, h * hw:(h + 1) * hw, pl.ds(start, blk)], ones], axis=0)
        pv = _dot(lhs, p)
        l_ref[h] = alpha * l_ref[h] + pv[hw:hw + 1]
        acc_ref[h] = alpha * acc_ref[h] + pv[:hw]
        m_ref[h] = m_new

    def block(j, carry):
        scores(1, j)
        consume(0, j, False)
        scores(0, j + 1)
        consume(1, j, False)
        return carry

    scores(0, 0)
    lax.fori_loop(0, i, block, 0)
    scores(1, i)
    consume(0, i, True)
    consume(1, i, True)

    lam = _lambda(lq1_ref, lk1_ref, lq2_ref, lk2_ref, lam_init)
    for h in heads:
        o = acc_ref[h] / l_ref[h]
        o = o[:, :blk] - lam * o[:, blk:]
        o = o * lax.rsqrt(jnp.mean(o * o, axis=0, keepdims=True) + EPS) * g_ref[...] * (1.0 - lam_init)
        o_ref[0, :, h * hw:(h + 1) * hw] = o.T.astype(o_ref.dtype)


def _attn_prompt_call(qt, k, vt, lq1, lk1, lq2, lk2, subln_g, lam_init, blk):
    b, t, _ = k.shape
    hw = 2 * ATT_DH
    gw = ATT_HPS * hw
    qtspec = pl.BlockSpec((1, gw, blk), lambda bi, hi, i: (bi, hi, i))
    kspec = pl.BlockSpec((1, t, gw), lambda bi, hi, i: (bi, 0, hi))
    vtspec = pl.BlockSpec((1, gw, t), lambda bi, hi, i: (bi, hi, 0))
    return pl.pallas_call(
        functools.partial(_attn_prompt_body, lam_init, blk),
        grid=(b, ATT_HEADS // ATT_HPS, t // blk),
        in_specs=[qtspec, kspec, vtspec] + [_const_spec((1, ATT_DH))] * 4 + [_const_spec((hw, 1))],
        out_specs=pl.BlockSpec((1, blk, gw), lambda bi, hi, i: (bi, i, hi)),
        out_shape=jax.ShapeDtypeStruct((b, t, ATT_W), BF16),
        scratch_shapes=[pltpu.VMEM((ATT_HPS, hw, 2 * blk), BF16), pltpu.VMEM((ATT_HPS, 1, 2 * blk), F32),
                        pltpu.VMEM((ATT_HPS, 1, 2 * blk), F32), pltpu.VMEM((ATT_HPS, hw, 2 * blk), F32),
                        pltpu.VMEM((ATT_HPS, blk, 2 * blk), F32)],
        compiler_params=_params(("parallel", "parallel", "arbitrary")),
        name="attn_prompt",
    )(qt, k, vt, lq1, lk1, lq2, lk2, subln_g.reshape(hw, 1))


def _attn_sample_body(lam_init, ts, q_ref, ckt_ref, cv_ref, kn_ref, vn_ref, lq1_ref, lk1_ref, lq2_ref, lk2_ref,
                      g_ref, o_ref, qq_ref, m_ref, l_ref, acc_ref):
    j = pl.program_id(1)
    hw = 2 * ATT_DH
    heads = range(ATT_HEADS)

    @pl.when(j == 0)
    def _():
        for h in heads:
            q = q_ref[0, :, h * hw:(h + 1) * hw]
            lane = lax.broadcasted_iota(jnp.int32, q.shape, 1)
            zero = jnp.zeros_like(q)
            qq_ref[h, :ts, :] = jnp.where(lane < ATT_DH, q, zero)
            qq_ref[h, ts:, :] = jnp.where(lane >= ATT_DH, q, zero)
        m_ref[...] = jnp.full(m_ref.shape, -jnp.inf, F32)
        l_ref[...] = jnp.zeros(l_ref.shape, F32)
        acc_ref[...] = jnp.zeros(acc_ref.shape, F32)

    def softmax_step(ss, vs):
        m_prev = [m_ref[h] for h in heads]
        m_new = [jnp.maximum(m, jnp.max(s, axis=-1, keepdims=True)) for m, s in zip(m_prev, ss)]
        alpha = [jnp.exp2(m - mn) for m, mn in zip(m_prev, m_new)]
        p = [jnp.exp2(s - mn) for s, mn in zip(ss, m_new)]
        pv = [_dot(x.astype(BF16), v) for x, v in zip(p, vs)]
        for h in heads:
            l_ref[h] = alpha[h] * l_ref[h] + jnp.sum(p[h], axis=-1, keepdims=True)
            acc_ref[h] = alpha[h] * acc_ref[h] + pv[h]
            m_ref[h] = m_new[h]

    tk = ckt_ref.shape[2]
    softmax_step([_dot(qq_ref[h], ckt_ref[0, h * hw:(h + 1) * hw, :].astype(BF16)) for h in heads],
                 [cv_ref[0, pl.ds(h, tk, stride=ATT_HEADS), :].astype(BF16) for h in heads])

    @pl.when(j == pl.num_programs(1) - 1)
    def _():
        lam = _lambda(lq1_ref, lk1_ref, lq2_ref, lk2_ref, lam_init)
        cols = [slice(h * hw, (h + 1) * hw) for h in heads]
        softmax_step([_dot_nt(qq_ref[h], kn_ref[0, :, cols[h]]) for h in heads],
                     [vn_ref[0, :, cols[h]] for h in heads])
        for h in heads:
            o = acc_ref[h] / l_ref[h]
            o_ref[0, :, cols[h]] = _subln(o[:ts] - lam * o[ts:], g_ref[...], lam_init).astype(o_ref.dtype)


def _attn_sample_call(q, cache_kt, cache_v, k_new, v_new, lq1, lk1, lq2, lk2, subln_g, lam_init, tk):
    b, ts, _ = q.shape
    past = cache_kt.shape[2]
    hw = 2 * ATT_DH
    new = pl.BlockSpec((1, ts, ATT_W), lambda bi, j: (bi, 0, 0))
    return pl.pallas_call(
        functools.partial(_attn_sample_body, lam_init, ts),
        grid=(b, past // tk),
        in_specs=[new, pl.BlockSpec((1, ATT_W, tk), lambda bi, j: (bi, 0, j)),
                  pl.BlockSpec((1, tk * ATT_HEADS, hw), lambda bi, j: (bi, j, 0)), new, new]
        + [_const_spec((1, ATT_DH))] * 4 + [_const_spec((1, hw))],
        out_specs=new,
        out_shape=jax.ShapeDtypeStruct((b, ts, ATT_W), BF16),
        scratch_shapes=[pltpu.VMEM((ATT_HEADS, 2 * ts, hw), BF16), pltpu.VMEM((ATT_HEADS, 2 * ts, 1), F32),
                        pltpu.VMEM((ATT_HEADS, 2 * ts, 1), F32), pltpu.VMEM((ATT_HEADS, 2 * ts, hw), F32)],
        compiler_params=_params(("parallel", "arbitrary")),
        name="attn_sample",
    )(q, cache_kt, cache_v, k_new, v_new, lq1, lk1, lq2, lk2, subln_g)


def _split2(x):
    hi = x.astype(BF16)
    return hi, (x - hi.astype(F32)).astype(BF16)


def _head_sum(x, ones_bd, passes=2):
    parts = []
    for g in range(x.shape[1] // MXU_W):
        xg = x[:, g * MXU_W:(g + 1) * MXU_W]
        if passes == 1:
            parts.append(_dot(xg.astype(BF16), ones_bd))
        else:
            hi, lo = _split2(xg)
            parts.append(_dot(hi, ones_bd) + _dot(lo, ones_bd))
    return jnp.concatenate(parts, axis=1)


def _bd_expand(x):
    c = x.shape[0]
    xt = jnp.concatenate([x] * WKV_GROUP, axis=0)
    row = lax.broadcasted_iota(jnp.int32, xt.shape, 0)
    lane = lax.broadcasted_iota(jnp.int32, xt.shape, 1)
    return jnp.where(row // c == lane // RW_HEAD, xt, jnp.zeros_like(xt))


def _wkv_chunk_terms(units):
    c = units[0][0].shape[0]
    slab = (c, WKV_GROUP * c)
    t_idx = lax.broadcasted_iota(jnp.int32, slab, 0)
    s_idx = lax.broadcasted_iota(jnp.int32, slab, 1) % c
    strict = s_idx < t_idx
    incl = s_idx <= t_idx
    eye = jnp.where(s_idx == t_idx, 1.0, 0.0)
    expand = lambda x: _bd_expand(x).astype(BF16)

    ar = [jnp.concatenate([u[0], u[1]], axis=0).astype(BF16) for u in units]
    sb = [_dot_nt(a, expand(u[2])) for a, u in zip(ar, units)]
    yield
    sk = [_dot_nt(a, expand(u[3])) for a, u in zip(ar, units)]
    yield
    l_ab = [jnp.where(strict, x[:c], 0.0) for x in sb]
    m_rb = [jnp.where(incl, x[c:], 0.0).astype(BF16) for x in sb]
    l_ak = [jnp.where(strict, x[:c], 0.0).astype(BF16) for x in sk]
    m_rk = [jnp.where(incl, x[c:], 0.0).astype(BF16) for x in sk]

    lp = l_ab
    tinv = [eye + x for x in l_ab]
    lp = [_dot(x.astype(BF16), expand(x)) for x in lp]
    yield
    power = 2
    while power < c:
        last = 2 * power >= c
        nxt = []
        for i, (x, t) in enumerate(zip(lp, tinv)):
            lhs = t if last else jnp.concatenate([x, t], axis=0)
            prod = _dot(lhs.astype(BF16), expand(x))
            if last:
                tinv[i] = t + prod
            else:
                nxt.append(prod[:c])
                tinv[i] = t + prod[c:]
        lp = nxt
        power *= 2
        yield

    v_exp = [expand(u[4]) for u in units]
    lakv = [_dot(x, ve) for x, ve in zip(l_ak, v_exp)]
    yield
    tinv_bf = [t.astype(BF16) for t in tinv]
    ta = [_dot(t, expand(u[0])) for t, u in zip(tinv_bf, units)]
    yield
    uv = [_dot(t, expand(x)) for t, x in zip(tinv_bf, lakv)]
    yield

    tr = [jnp.concatenate([x, u[1]], axis=0).astype(BF16) for x, u in zip(ta, units)]
    m_rbk = [jnp.concatenate([m, mk], axis=1) for m, mk in zip(m_rb, m_rk)]
    return list(zip(tr, uv, m_rbk, v_exp))


def _wkv_chunk_apply(states, terms, vs, bkhs, decays):
    c = vs[0].shape[0]
    gw = states[0].shape[0]
    bd = (lax.broadcasted_iota(jnp.int32, (gw, gw), 0) // RW_HEAD
          == lax.broadcasted_iota(jnp.int32, (gw, gw), 1) // RW_HEAD)
    ur = [_dot_nt(t[0], s.astype(BF16)) for t, s in zip(terms, states)]
    yield
    u = [x[:c] + t[1] for x, t in zip(ur, terms)]
    upd = [_dot_tn(jnp.concatenate([x, v], axis=0).astype(BF16), bkh) for x, v, bkh in zip(u, vs, bkhs)]
    yield
    ys = [x[c:] + _dot(t[2], jnp.concatenate([_bd_expand(uu).astype(BF16), t[3]], axis=0))
          for x, uu, t in zip(ur, u, terms)]
    yield
    return ys, [s * d + jnp.where(bd, x, 0.0) for s, d, x in zip(states, decays, upd)]


def _interleave(*gens):
    results = [None] * len(gens)
    live = list(range(len(gens)))
    while live:
        for i in list(live):
            try:
                next(gens[i])
            except StopIteration as stop:
                results[i] = stop.value
                live.remove(i)
    return results


def _rwkv_body(bb, tb, t_valid, z_ref, s0_ref, sh0_ref, mu_ref, w0_ref, wa_ref, a0_ref, g2_ref, kk_ref, ka_ref,
               rk_ref, lng_ref, lnb_ref, o_ref, s_ref, h_ref, carry_ref):
    ib = pl.program_id(1)
    c = WKV_C
    nc = tb // c
    lanes_of = lambda g: slice(g * WKV_GW, (g + 1) * WKV_GW)

    @pl.when(ib == 0)
    def _():
        for bi in range(bb):
            carry_ref[bi] = sh0_ref[bi]
            for g in range(WKV_NG):
                blocks = [s0_ref[bi, g * WKV_GROUP + h] for h in range(WKV_GROUP)]
                h_ref[bi, g] = _bd_expand(jnp.concatenate(blocks, axis=1))

    ones_bd = (lax.broadcasted_iota(jnp.int32, (MXU_W, MXU_W), 0) // RW_HEAD
               == lax.broadcasted_iota(jnp.int32, (MXU_W, MXU_W), 1) // RW_HEAD).astype(BF16)

    def prepare(seqs):
        n = len(seqs) * tb
        z = jnp.concatenate([z_ref[bi] for bi in seqs], axis=0)
        mu = mu_ref[...]
        zs = z + (pltpu.roll(z, 1, 0) - z) * mu
        first = lax.broadcasted_iota(jnp.int32, (F32_ROWS, SHIFT_W), 0) == 0
        pieces = []
        for i, bi in enumerate(seqs):
            head = z[i * tb:i * tb + F32_ROWS]
            pieces += [jnp.where(first, head + (carry_ref[bi] - head) * mu, zs[i * tb:i * tb + F32_ROWS]),
                       zs[i * tb + F32_ROWS:(i + 1) * tb]]
            carry_ref[bi] = z[(i + 1) * tb - 1:(i + 1) * tb, :]
        zs = jnp.concatenate(pieces, axis=0)
        yield

        r = zs[:, :RW_W]
        k = zs[:, RW_W:2 * RW_W]
        v = zs[:, 2 * RW_W:3 * RW_W]
        zwa = zs[:, 3 * RW_W:3 * RW_W + W_LORA + A_LORA]
        zg = zs[:, 3 * RW_W + W_LORA + A_LORA:]
        lane = lax.broadcasted_iota(jnp.int32, zwa.shape, 1)
        lora = _dot(jnp.where(lane < W_LORA, jnp.tanh(zwa), zwa).astype(BF16), wa_ref[...])
        ew = math.exp(-0.5) * jax.nn.sigmoid(w0_ref[...] + lora[:, :RW_W])
        a = jax.nn.sigmoid(a0_ref[...] + lora[:, RW_W:])
        yield
        gate = _dot(jax.nn.sigmoid(zg).astype(BF16), g2_ref[...])
        yield

        kk = k * kk_ref[...]
        kk = kk * lax.rsqrt(jnp.maximum(_head_sum(kk * kk, ones_bd, passes=1), 1e-24))
        yield
        k = k * (1.0 + (a - 1.0) * ka_ref[...])
        bonus = _head_sum(r * k * rk_ref[...], ones_bd) * v
        yield

        if t_valid is not None:
            live = lax.broadcasted_iota(jnp.int32, ew.shape, 0) % tb + ib * tb < t_valid
            ew = jnp.where(live, ew, 0.0)
            k = jnp.where(live, k, 0.0)
            kk = jnp.where(live, kk, 0.0)
            v = jnp.where(live, v, 0.0)

        rows = min(n, MXU_W)
        ti = lax.broadcasted_iota(jnp.int32, (rows, rows), 0)
        si = lax.broadcasted_iota(jnp.int32, (rows, rows), 1)
        tri = jnp.logical_and(si <= ti, si // c == ti // c).astype(BF16)
        e_hi, e_lo = _split2(ew)
        cum = -jnp.concatenate([_dot(tri, e_hi[i0:i0 + rows]) + _dot(tri, e_lo[i0:i0 + rows])
                                for i0 in range(0, n, rows)], axis=0)
        gamma_end = jnp.concatenate([jnp.broadcast_to(jnp.exp(cum[i0 + c - 1:i0 + c, :]), (c, RW_W))
                                     for i0 in range(0, n, c)], axis=0)
        yield
        grow = jnp.exp(-cum)
        b = kk * a
        at = -kk * jnp.exp(cum + ew)
        rt = r / grow
        yield
        bt, kt = b * grow, k * grow
        pre = dict(at=at, rt=rt, bt=bt, kt=kt, v=v, gate=gate, bonus=bonus)
        yield
        bh, kh = bt * gamma_end, kt * gamma_end
        for ci in range(nc):
            for i in range(len(seqs)):
                sl = rows_of(ci, i)
                for g in range(WKV_NG):
                    gl = lanes_of(g)
                    pre[ci, i, g] = (v[sl, gl], jnp.concatenate([bh[sl, gl], kh[sl, gl]], axis=0).astype(BF16),
                                     gamma_end[sl, gl][:1])
        return pre

    rows_of = lambda ci, i: slice(i * tb + ci * c, i * tb + (ci + 1) * c)

    def chunk_terms(pre, nseq):
        order = [(ci, i, g) for ci in range(nc) for i in range(nseq) for g in range(WKV_NG)]
        units = [tuple(pre[name][rows_of(ci, i), lanes_of(g)] for name in ("at", "rt", "bt", "kt", "v"))
                 for ci, i, g in order]
        return dict(zip(order, (yield from _wkv_chunk_terms(units))))

    def apply_chunks(seqs, pre, terms):
        chains = [(i, g) for i in range(len(seqs)) for g in range(WKV_NG)]
        states = [h_ref[seqs[i], g] for i, g in chains]
        y_parts = {}
        for ci in range(nc):
            extra = [pre[ci, i, g] for i, g in chains]
            ys, states = yield from _wkv_chunk_apply(states, [terms[ci, i, g] for i, g in chains],
                                                     [e[0] for e in extra], [e[1] for e in extra],
                                                     [e[2] for e in extra])
            for (i, g), y_unit in zip(chains, ys):
                y_parts[ci, i, g] = y_unit
        for (i, g), s_out in zip(chains, states):
            h_ref[seqs[i], g] = s_out
        return jnp.concatenate([jnp.concatenate([y_parts[ci, i, g] for g in range(WKV_NG)], axis=1)
                                for i in range(len(seqs)) for ci in range(nc)], axis=0)

    def finish(seqs, pre, y):
        mean = _head_sum(y, ones_bd) * (1.0 / RW_HEAD)
        yield
        d = y - mean
        var = _head_sum(d * d, ones_bd, passes=1) * (1.0 / RW_HEAD)
        yield
        yn = d * lax.rsqrt(var + LNX_EPS) * lng_ref[...] + lnb_ref[...]
        out = ((yn + pre["bonus"]) * pre["gate"]).astype(o_ref.dtype)
        for i, bi in enumerate(seqs):
            o_ref[bi] = out[i * tb:(i + 1) * tb]

    gsz = max(1, bb // WKV_PIPE_GROUPS)
    groups = [list(range(g0, g0 + gsz)) for g0 in range(0, bb, gsz)]
    stages = (lambda g, _: prepare(groups[g]),
              lambda g, pre: chunk_terms(pre[g], len(groups[g])),
              lambda g, pre, terms: apply_chunks(groups[g], pre[g], terms[g]),
              lambda g, pre, terms, ys: finish(groups[g], pre[g], ys[g]))
    done = [{} for _ in stages]
    for step in range(len(groups) + len(stages) - 1):
        live = [(s, step - s) for s in range(len(stages)) if 0 <= step - s < len(groups)]
        results = _interleave(*[stages[s](g, *done[:s]) if s else stages[0](g, None) for s, g in live])
        for (s, g), res in zip(live, results):
            done[s][g] = res

    @pl.when(ib == pl.num_programs(1) - 1)
    def _():
        for bi in range(bb):
            for g in range(WKV_NG):
                h = h_ref[bi, g]
                for hh in range(WKV_GROUP):
                    blk = h[hh * RW_HEAD:(hh + 1) * RW_HEAD, hh * RW_HEAD:(hh + 1) * RW_HEAD]
                    s_ref[bi, g * WKV_GROUP + hh] = blk


def _rwkv_call(z_rw, s0, shift0, p, bb, tb, t_valid):
    b, t, _ = z_rw.shape
    vec = lambda w: _const_spec((1, w))
    return pl.pallas_call(
        functools.partial(_rwkv_body, bb, tb, t_valid),
        grid=(b // bb, t // tb),
        in_specs=[pl.BlockSpec((bb, tb, SHIFT_W), lambda bi, i: (bi, i, 0)),
                  pl.BlockSpec((bb, RW_HEADS, RW_HEAD, RW_HEAD), lambda bi, i: (bi, 0, 0, 0)),
                  pl.BlockSpec((bb, 1, SHIFT_W), lambda bi, i: (bi, 0, 0)),
                  vec(SHIFT_W), vec(RW_W), _const_spec((W_LORA + A_LORA, 2 * RW_W)), vec(RW_W),
                  _const_spec((G_LORA, RW_W)), vec(RW_W), vec(RW_W), vec(RW_W), vec(RW_W), vec(RW_W)],
        out_specs=[pl.BlockSpec((bb, tb, RW_W), lambda bi, i: (bi, i, 0)),
                   pl.BlockSpec((bb, RW_HEADS, RW_HEAD, RW_HEAD), lambda bi, i: (bi, 0, 0, 0))],
        out_shape=[jax.ShapeDtypeStruct((b, t, RW_W), BF16),
                   jax.ShapeDtypeStruct((b, RW_HEADS, RW_HEAD, RW_HEAD), F32)],
        scratch_shapes=[pltpu.VMEM((bb, WKV_NG, WKV_GW, WKV_GW), F32), pltpu.VMEM((bb, 1, SHIFT_W), F32)],
        compiler_params=_params(("parallel", "arbitrary")),
        name="rwkv",
    )(z_rw, s0, shift0, p["mu"], p["w0"], p["wa"], p["a0"], p["g2"], p["k_k"], p["k_a"], p["r_k"],
      p["lnx_g"], p["lnx_b"])


def _merge_body(h_ref, oa_ref, orw_ref, gate_ref, woa_ref, worw_ref, wout_ref, postg_ref, pre_ref, win_ref,
                wo2_ref, post2_ref, y_ref):
    gates = gate_ref[...]
    merged = (gates[:, :D_MODEL] * _dot(oa_ref[...], woa_ref[...])
              + gates[:, D_MODEL:] * _dot(orw_ref[...], worw_ref[...]))
    h2 = h_ref[...] + _rms(_dot(merged.astype(BF16), wout_ref[...]), postg_ref[...])
    f = _swiglu_ffn(_rms(h2, pre_ref[...]).astype(BF16), win_ref, wo2_ref)
    y_ref[...] = h2 + MACARON * _rms(f, post2_ref[...])


def _merge_call(h, o_att, o_rw, gates, w_o_att, w_o_rwkv, w_out, mix_post_g, pre_g, w_in, w_out2, post_g, tm):
    m = h.shape[0]
    row = lambda w: pl.BlockSpec((tm, w), lambda i: (i, 0))
    vec = _const_spec((1, D_MODEL))
    return pl.pallas_call(
        _merge_body,
        grid=(m // tm,),
        in_specs=[row(D_MODEL), row(ATT_W), row(RW_W), row(GATE_W),
                  _const_spec((ATT_W, D_MODEL)), _const_spec((RW_W, D_MODEL)), _const_spec((D_MODEL, D_MODEL)), vec,
                  vec, _const_spec((D_MODEL, 2 * D_FF)), _const_spec((D_FF, D_MODEL)), vec],
        out_specs=row(D_MODEL),
        out_shape=jax.ShapeDtypeStruct((m, D_MODEL), F32),
        compiler_params=_params(("parallel",)),
        name="merge_ffn",
    )(h, o_att, o_rw, gates, w_o_att, w_o_rwkv, w_out, mix_post_g, pre_g, w_in, w_out2, post_g)


def _pick_tile(m, want):
    t = min(m, want)
    assert m % t == 0
    return t


def _layer(x, pos, l, p, cache_k, cache_v, s0, shift0):
    b, t, _ = x.shape
    m = b * t
    tm = _pick_tile(m, ROW_TILE)
    lam_init = 0.8 - 0.6 * math.exp(-0.3 * l)

    late = p.get("late_f32")
    h = _ffn_call(x.reshape(m, D_MODEL), p["ffn1_pre_g"], p["ffn1_w_in"], p["ffn1_w_out"], p["ffn1_post_g"], tm,
                  cast=tuple(late.values()) if late else ())
    if late:
        h, casted = h
        p = {**p, **dict(zip(late, casted)), "late_f32": None}

    cos_t, sin_t = _rope_tables(pos)
    if t < tm:
        cos_t, sin_t = jnp.tile(cos_t, (tm // t, 1)), jnp.tile(sin_t, (tm // t, 1))
    q, k, k_bf, v, v_bf, z_rw, gates = _mixproj_call(h, p["mix_pre_g"], p["w_in"], cos_t, sin_t, tm,
                                                     t if cache_k is None else None)

    lam_args = (p["att_lambda_q1"], p["att_lambda_k1"], p["att_lambda_q2"], p["att_lambda_k2"], p["att_subln_g"])
    r3 = lambda a: a.reshape(b, t, a.shape[-1])
    if cache_k is None:
        o_att = _attn_prompt_call(q, r3(k_bf), v_bf, *lam_args, lam_init, _pick_tile(t, ATT_BLK))
    else:
        past = cache_k.shape[1]
        cache_kt = cache_k.reshape(b, past, ATT_W).transpose(0, 2, 1)
        cache_vr = cache_v.reshape(b, past * ATT_HEADS, 2 * ATT_DH)
        o_att = _attn_sample_call(r3(q), cache_kt, cache_vr, r3(k_bf), r3(v_bf), *lam_args, lam_init,
                                  _pick_tile(past, CACHE_BLK))

    z_rw3 = r3(z_rw)
    t_pad = -(-t // WKV_C) * WKV_C
    z_in = z_rw3 if t_pad == t else jnp.pad(z_rw3, ((0, 0), (0, t_pad - t), (0, 0)))
    o_rw, s_new = _rwkv_call(z_in, s0, shift0, p["rwkv"], _pick_tile(b, WKV_SEQS), _pick_tile(t_pad, WKV_TB),
                             None if t_pad == t else t)
    o_rw = o_rw[:, :t]

    y = _merge_call(h, o_att.reshape(m, ATT_W), o_rw.reshape(m, RW_W), gates, p["w_o_att"], p["w_o_rwkv"],
                    p["w_out"], p["mix_post_g"], p["ffn2_pre_g"], p["ffn2_w_in"], p["ffn2_w_out"],
                    p["ffn2_post_g"], tm)
    if cache_k is None:
        k_rows = k.reshape(b, ATT_HEADS, 2, ATT_DH, t).transpose(0, 4, 1, 2, 3)
    else:
        k_rows = k.reshape(b, t, ATT_HEADS, 2, ATT_DH)
    return (y.reshape(b, t, D_MODEL), k_rows, v.reshape(b, t, ATT_HEADS, 2 * ATT_DH), s_new, z_rw3[:, -1:]), p


def kernel(x_prompt, x_sample, cache_att_k, cache_att_v, state_rwkv, state_shift, ffn1_pre_g, ffn1_w_in, ffn1_w_out, ffn1_post_g, mix_pre_g, w_in, att_lambda_q1, att_lambda_k1, att_lambda_q2, att_lambda_k2, att_subln_g, rwkv_mu, rwkv_w0, rwkv_w2, rwkv_a0, rwkv_a2, rwkv_g2, rwkv_k_k, rwkv_k_a, rwkv_r_k, rwkv_lnx_g, rwkv_lnx_b, w_o_att, w_o_rwkv, w_out, mix_post_g, ffn2_pre_g, ffn2_w_in, ffn2_w_out, ffn2_post_g):
    depth = w_in.shape[0]
    bp, tp, _ = x_prompt.shape
    bs, ts, _ = x_sample.shape
    past = cache_att_k.shape[2]
    pos_p = jnp.arange(tp)
    pos_s = past + jnp.arange(ts)
    xp, xs = x_prompt, x_sample
    outs_p, outs_s = [], []
    vec = lambda a: a.reshape(1, -1)
    for l in range(depth):
        zeros = jnp.zeros((W_LORA, RW_W), F32)
        wa = jnp.concatenate([jnp.concatenate([rwkv_w2[l], zeros], axis=1),
                              jnp.concatenate([zeros, rwkv_a2[l]], axis=1)], axis=0)
        p = dict(
            ffn1_pre_g=vec(ffn1_pre_g[l]), ffn1_w_in=ffn1_w_in[l].astype(BF16), ffn1_w_out=ffn1_w_out[l].astype(BF16),
            ffn1_post_g=vec(ffn1_post_g[l]), mix_pre_g=vec(mix_pre_g[l]),
            late_f32=dict(w_in=w_in[l], w_o_att=w_o_att[l], w_o_rwkv=w_o_rwkv[l], w_out=w_out[l],
                          ffn2_w_in=ffn2_w_in[l], ffn2_w_out=ffn2_w_out[l]),
            att_lambda_q1=vec(att_lambda_q1[l]), att_lambda_k1=vec(att_lambda_k1[l]),
            att_lambda_q2=vec(att_lambda_q2[l]), att_lambda_k2=vec(att_lambda_k2[l]),
            att_subln_g=vec(att_subln_g[l]),
            rwkv=dict(mu=vec(rwkv_mu[l]), w0=vec(rwkv_w0[l]), wa=wa.astype(BF16), a0=vec(rwkv_a0[l]),
                      g2=rwkv_g2[l].astype(BF16), k_k=vec(rwkv_k_k[l]), k_a=vec(rwkv_k_a[l]), r_k=vec(rwkv_r_k[l]),
                      lnx_g=vec(rwkv_lnx_g[l]), lnx_b=vec(rwkv_lnx_b[l])),
            mix_post_g=vec(mix_post_g[l]), ffn2_pre_g=vec(ffn2_pre_g[l]), ffn2_post_g=vec(ffn2_post_g[l]),
        )
        s0p = jnp.zeros((bp, RW_HEADS, RW_HEAD, RW_HEAD), F32)
        sh0p = jnp.zeros((bp, 1, SHIFT_W), F32)
        (xp, *rest_p), p = _layer(xp, pos_p, l, p, None, None, s0p, sh0p)
        (xs, *rest_s), p = _layer(xs, pos_s, l, p, cache_att_k[l], cache_att_v[l], state_rwkv[l], state_shift[l])
        outs_p.append(rest_p)
        outs_s.append(rest_s)
    stack = lambda outs, i: jnp.stack([o[i] for o in outs], 0)
    return (xp, xs, stack(outs_p, 0), stack(outs_p, 1), stack(outs_p, 2), stack(outs_p, 3),
            stack(outs_s, 0), stack(outs_s, 1), stack(outs_s, 2), stack(outs_s, 3))
```

```python
import functools
import math

import jax
import jax.numpy as jnp
from jax import lax
from jax.experimental import pallas as pl
from jax.experimental.pallas import tpu as pltpu

F32 = jnp.float32
BF16 = jnp.bfloat16

D_MODEL = 1024
D_FF = 2816
CHUNK = 64
ROPE_THETA = 10000.0
EPS = 1e-6
MACARON = 0.5
ATT_HEADS = 4
ATT_DH = 64
ATT_W = ATT_HEADS * 2 * ATT_DH
ATT_HPS = 2
Q_SCALE = ATT_DH ** -0.5 * math.log2(math.e)
RW_HEAD = 64
RW_W = D_MODEL // 2
RW_HEADS = RW_W // RW_HEAD
W_LORA = 64
A_LORA = 64
G_LORA = 128
SHIFT_W = 3 * RW_W + W_LORA + A_LORA + G_LORA
LNX_EPS = 64e-5
GATE_W = 2 * D_MODEL
IN_W = 3 * ATT_W + SHIFT_W + GATE_W

LANES = 128
F32_ROWS = 8
BF16_ROWS = 16
MXU_W = 256
VMEM_LIMIT = 56 * 1024 * 1024

ROW_TILE = 512
ATT_BLK = 512
CACHE_BLK = 4096

WKV_C = 64
WKV_GROUP = MXU_W // RW_HEAD
WKV_GW = WKV_GROUP * RW_HEAD
WKV_NG = RW_W // WKV_GW
WKV_SEQS = 8
WKV_TB = 256
WKV_PIPE_GROUPS = 2


def _params(sem):
    return pltpu.CompilerParams(dimension_semantics=sem, vmem_limit_bytes=VMEM_LIMIT)


def _const_spec(shape):
    nd = len(shape)
    return pl.BlockSpec(shape, lambda *_: (0,) * nd, pipeline_mode=pl.Buffered(1))


def _rms(x, g):
    return x * lax.rsqrt(jnp.mean(x * x, axis=-1, keepdims=True) + EPS) * g


def _dot(a, b):
    return jnp.dot(a, b, preferred_element_type=F32)


def _dot_nt(a, b):
    return lax.dot_general(a, b, (((1,), (1,)), ((), ())), preferred_element_type=F32)


def _dot_tn(a, b):
    return lax.dot_general(a, b, (((0,), (0,)), ((), ())), preferred_element_type=F32)


def _swiglu_ffn(xn_bf, w_in_ref, w_out_ref):
    hh = _dot(xn_bf, w_in_ref[...])
    gate = hh[:, :D_FF]
    up = hh[:, D_FF:]
    act = (gate * jax.nn.sigmoid(gate) * up).astype(BF16)
    return _dot(act, w_out_ref[...])


def _ffn_body(cast_steps, x_ref, pre_ref, win_ref, wout_ref, post_ref, *refs):
    ncast = len(cast_steps)
    cast_in, o_ref, cast_out = refs[:ncast], refs[ncast], refs[ncast + 1:]
    x = x_ref[...]
    f = _swiglu_ffn(_rms(x, pre_ref[...]).astype(BF16), win_ref, wout_ref)
    o_ref[...] = x + MACARON * _rms(f, post_ref[...])
    for w_ref, c_ref, steps in zip(cast_in, cast_out, cast_steps):
        @pl.when(pl.program_id(0) < steps)
        def _():
            c_ref[...] = w_ref[...].astype(BF16)


def _cast_rows(rows, nsteps):
    return next(r for r in range(BF16_ROWS, rows + 1, BF16_ROWS) if rows % r == 0 and rows // r <= nsteps)


def _ffn_call(x, pre_g, w_in, w_out, post_g, tm, cast=()):
    m = x.shape[0]
    nsteps = m // tm
    row = lambda w: pl.BlockSpec((tm, w), lambda i: (i, 0))
    slabs = [_cast_rows(w.shape[0], nsteps) for w in cast]
    steps = tuple(w.shape[0] // r for w, r in zip(cast, slabs))
    slab_index = lambda i, last: (jnp.minimum(i, last), 0)
    cast_specs = [pl.BlockSpec((r, w.shape[1]), functools.partial(slab_index, last=s - 1))
                  for w, r, s in zip(cast, slabs, steps)]
    out = pl.pallas_call(
        functools.partial(_ffn_body, steps),
        grid=(nsteps,),
        in_specs=[row(D_MODEL), _const_spec((1, D_MODEL)), _const_spec((D_MODEL, 2 * D_FF)),
                  _const_spec((D_FF, D_MODEL)), _const_spec((1, D_MODEL))] + cast_specs,
        out_specs=[row(D_MODEL)] + cast_specs,
        out_shape=[jax.ShapeDtypeStruct((m, D_MODEL), F32)] + [jax.ShapeDtypeStruct(w.shape, BF16) for w in cast],
        compiler_params=_params(("arbitrary",) if cast else ("parallel",)),
        name="ffn",
    )(x, pre_g, w_in, w_out, post_g, *cast)
    return (out[0], out[1:]) if cast else out[0]


def _rope_tables(pos):
    half = ATT_DH // 2
    inv = ROPE_THETA ** (-jnp.arange(half, dtype=F32) / half)
    ang = pos.astype(F32)[:, None] * inv[None, :]
    cos, sin = jnp.cos(ang), jnp.sin(ang)
    cos_t = jnp.tile(jnp.concatenate([cos, cos], axis=-1), (1, LANES // ATT_DH))
    sin_t = jnp.tile(jnp.concatenate([-sin, sin], axis=-1), (1, LANES // ATT_DH))
    return cos_t, sin_t


def _rope(x, cos_t, sin_t):
    n = x.shape[-1]
    half = ATT_DH // 2
    lane = lax.broadcasted_iota(jnp.int32, x.shape, 1)
    swapped = jnp.where(lane % ATT_DH < half, pltpu.roll(x, n - half, 1), pltpu.roll(x, half, 1))
    return x * cos_t + swapped * sin_t


def _mixproj_body(v_cols, h_ref, g_ref, w_ref, cos_ref, sin_ref, q_ref, k_ref, kb_ref, v_ref, vb_ref, rw_ref,
                  gate_ref):
    u = _rms(h_ref[...], g_ref[...]).astype(BF16)
    z = _dot(u, w_ref[...])
    cos_t = jnp.tile(cos_ref[...], (1, ATT_W // LANES))
    sin_t = jnp.tile(sin_ref[...], (1, ATT_W // LANES))
    q = _rope(z[:, :ATT_W], cos_t, sin_t) * Q_SCALE
    if v_cols:
        q_ref[0] = q.T.astype(BF16)
    else:
        q_ref[...] = q.astype(BF16)
    k = _rope(z[:, ATT_W:2 * ATT_W], cos_t, sin_t)
    kb_ref[...] = k.astype(BF16)
    v = z[:, 2 * ATT_W:3 * ATT_W]
    if v_cols:
        k_ref[0] = k.T
        hw = 2 * ATT_DH
        for h in range(ATT_HEADS):
            v_ref[pl.ds(h, v.shape[0], stride=ATT_HEADS), :] = v[:, h * hw:(h + 1) * hw]
        vb_ref[0] = v.T.astype(BF16)
    else:
        k_ref[...] = k
        v_ref[...] = v
        vb_ref[...] = v.astype(BF16)
    rw_ref[...] = z[:, 3 * ATT_W:3 * ATT_W + SHIFT_W]
    gate_ref[...] = jax.nn.sigmoid(z[:, 3 * ATT_W + SHIFT_W:]).astype(BF16)


def _mixproj_call(h, mix_pre_g, w_in, cos_t, sin_t, tm, v_cols_t):
    m = h.shape[0]
    ntab = cos_t.shape[0] // tm
    row = lambda w: pl.BlockSpec((tm, w), lambda i: (i, 0))
    tab = pl.BlockSpec((tm, LANES), lambda i: (i % ntab, 0))
    sds = lambda w, dt: jax.ShapeDtypeStruct((m, w), dt)
    if v_cols_t is None:
        q_spec, q_shape = row(ATT_W), sds(ATT_W, BF16)
        k_spec, k_shape = row(ATT_W), sds(ATT_W, F32)
        v_spec, v_shape = row(ATT_W), sds(ATT_W, F32)
        vb_spec, vb_shape = row(ATT_W), sds(ATT_W, BF16)
    else:
        nt = v_cols_t // tm
        q_spec = k_spec = vb_spec = pl.BlockSpec((1, ATT_W, tm), lambda i: (i // nt, 0, i % nt))
        k_shape = jax.ShapeDtypeStruct((m // v_cols_t, ATT_W, v_cols_t), F32)
        q_shape = vb_shape = jax.ShapeDtypeStruct((m // v_cols_t, ATT_W, v_cols_t), BF16)
        v_spec = pl.BlockSpec((tm * ATT_HEADS, 2 * ATT_DH), lambda i: (i, 0))
        v_shape = jax.ShapeDtypeStruct((m * ATT_HEADS, 2 * ATT_DH), F32)
    return pl.pallas_call(
        functools.partial(_mixproj_body, v_cols_t is not None),
        grid=(m // tm,),
        in_specs=[row(D_MODEL), _const_spec((1, D_MODEL)), _const_spec((D_MODEL, IN_W)), tab, tab],
        out_specs=[q_spec, k_spec, row(ATT_W), v_spec, vb_spec, row(SHIFT_W), row(GATE_W)],
        out_shape=[q_shape, k_shape, sds(ATT_W, BF16), v_shape, vb_shape,
                   sds(SHIFT_W, F32), sds(GATE_W, BF16)],
        compiler_params=_params(("parallel",)),
        name="mixproj",
    )(h, mix_pre_g, w_in, cos_t, sin_t)


def _lambda(lq1_ref, lk1_ref, lq2_ref, lk2_ref, lam_init):
    s1 = jnp.sum(lq1_ref[...] * lk1_ref[...], axis=-1, keepdims=True)
    s2 = jnp.sum(lq2_ref[...] * lk2_ref[...], axis=-1, keepdims=True)
    return jnp.exp(s1) - jnp.exp(s2) + lam_init


def _subln(o, g, lam_init):
    return o * lax.rsqrt(jnp.mean(o * o, axis=-1, keepdims=True) + EPS) * g * (1.0 - lam_init)


def _attn_prompt_body(lam_init, blk, qt_ref, k_ref, vt_ref, lq1_ref, lk1_ref, lq2_ref, lk2_ref, g_ref, o_ref,
                      qq_ref, m_ref, l_ref, acc_ref, st_ref):
    i = pl.program_id(2)
    hw = 2 * ATT_DH
    heads = range(ATT_HPS)
    for h in heads:
        qt = qt_ref[0, h * hw:(h + 1) * hw, :]
        feat = lax.broadcasted_iota(jnp.int32, qt.shape, 0)
        zero = jnp.zeros_like(qt)
        qq_ref[h, :, :blk] = jnp.where(feat < ATT_DH, qt, zero)
        qq_ref[h, :, blk:] = jnp.where(feat >= ATT_DH, qt, zero)
    m_ref[...] = jnp.full(m_ref.shape, -jnp.inf, F32)
    l_ref[...] = jnp.zeros(l_ref.shape, F32)
    acc_ref[...] = jnp.zeros(acc_ref.shape, F32)

    def scores(h, j):
        start = pl.multiple_of(j * blk, blk)
        st_ref[h] = _dot(k_ref[0, pl.ds(start, blk), h * hw:(h + 1) * hw], qq_ref[h])

    ones = jnp.ones((BF16_ROWS, blk), BF16)

    def consume(h, j, diagonal):
        start = pl.multiple_of(j * blk, blk)
        st = st_ref[h]
        if diagonal:
            key = lax.broadcasted_iota(jnp.int32, st.shape, 0)
            qry = lax.broadcasted_iota(jnp.int32, st.shape, 1) % blk
            st = jnp.where(key // CHUNK <= qry // CHUNK, st, -jnp.inf)
        m_prev = m_ref[h]
        m_new = jnp.maximum(m_prev, jnp.max(st, axis=0, keepdims=True))
        alpha = jnp.exp2(m_prev - m_new)
        p = jnp.exp2(st - m_new).astype(BF16)
        lhs = jnp.concatenate([vt_ref[0, h * hw:(h + 1) * hw, pl.ds(start, blk)], ones], axis=0)
        pv = _dot(lhs, p)
        l_ref[h] = alpha * l_ref[h] + pv[hw:hw + 1]
        acc_ref[h] = alpha * acc_ref[h] + pv[:hw]
        m_ref[h] = m_new

    def block(j, carry):
        scores(1, j)
        consume(0, j, False)
        scores(0, j + 1)
        consume(1, j, False)
        return carry

    scores(0, 0)
    lax.fori_loop(0, i, block, 0)
    scores(1, i)
    consume(0, i, True)
    consume(1, i, True)

    lam = _lambda(lq1_ref, lk1_ref, lq2_ref, lk2_ref, lam_init)
    for h in heads:
        o = acc_ref[h] / l_ref[h]
        o = o[:, :blk] - lam * o[:, blk:]
        o = o * lax.rsqrt(jnp.mean(o * o, axis=0, keepdims=True) + EPS) * g_ref[...] * (1.0 - lam_init)
        o_ref[0, :, h * hw:(h + 1) * hw] = o.T.astype(o_ref.dtype)


def _attn_prompt_call(qt, k, vt, lq1, lk1, lq2, lk2, subln_g, lam_init, blk):
    b, t, _ = k.shape
    hw = 2 * ATT_DH
    gw = ATT_HPS * hw
    qtspec = pl.BlockSpec((1, gw, blk), lambda bi, hi, i: (bi, hi, i))
    kspec = pl.BlockSpec((1, t, gw), lambda bi, hi, i: (bi, 0, hi))
    vtspec = pl.BlockSpec((1, gw, t), lambda bi, hi, i: (bi, hi, 0))
    return pl.pallas_call(
        functools.partial(_attn_prompt_body, lam_init, blk),
        grid=(b, ATT_HEADS // ATT_HPS, t // blk),
        in_specs=[qtspec, kspec, vtspec] + [_const_spec((1, ATT_DH))] * 4 + [_const_spec((hw, 1))],
        out_specs=pl.BlockSpec((1, blk, gw), lambda bi, hi, i: (bi, i, hi)),
        out_shape=jax.ShapeDtypeStruct((b, t, ATT_W), BF16),
        scratch_shapes=[pltpu.VMEM((ATT_HPS, hw, 2 * blk), BF16), pltpu.VMEM((ATT_HPS, 1, 2 * blk), F32),
                        pltpu.VMEM((ATT_HPS, 1, 2 * blk), F32), pltpu.VMEM((ATT_HPS, hw, 2 * blk), F32),
                        pltpu.VMEM((ATT_HPS, blk, 2 * blk), F32)],
        compiler_params=_params(("parallel", "parallel", "arbitrary")),
        name="attn_prompt",
    )(qt, k, vt, lq1, lk1, lq2, lk2, subln_g.reshape(hw, 1))


def _attn_sample_body(lam_init, ts, q_ref, ckt_ref, cv_ref, kn_ref, vn_ref, lq1_ref, lk1_ref, lq2_ref, lk2_ref,
                      g_ref, o_ref, qq_ref, m_ref, l_ref, acc_ref):
    j = pl.program_id(1)
    hw = 2 * ATT_DH
    heads = range(ATT_HEADS)

    @pl.when(j == 0)
    def _():
        for h in heads:
            q = q_ref[0, :, h * hw:(h + 1) * hw]
            lane = lax.broadcasted_iota(jnp.int32, q.shape, 1)
            zero = jnp.zeros_like(q)
            qq_ref[h, :ts, :] = jnp.where(lane < ATT_DH, q, zero)
            qq_ref[h, ts:, :] = jnp.where(lane >= ATT_DH, q, zero)
        m_ref[...] = jnp.full(m_ref.shape, -jnp.inf, F32)
        l_ref[...] = jnp.zeros(l_ref.shape, F32)
        acc_ref[...] = jnp.zeros(acc_ref.shape, F32)

    def softmax_step(ss, vs):
        m_prev = [m_ref[h] for h in heads]
        m_new = [jnp.maximum(m, jnp.max(s, axis=-1, keepdims=True)) for m, s in zip(m_prev, ss)]
        alpha = [jnp.exp2(m - mn) for m, mn in zip(m_prev, m_new)]
        p = [jnp.exp2(s - mn) for s, mn in zip(ss, m_new)]
        pv = [_dot(x.astype(BF16), v) for x, v in zip(p, vs)]
        for h in heads:
            l_ref[h] = alpha[h] * l_ref[h] + jnp.sum(p[h], axis=-1, keepdims=True)
            acc_ref[h] = alpha[h] * acc_ref[h] + pv[h]
            m_ref[h] = m_new[h]

    tk = ckt_ref.shape[2]
    softmax_step([_dot(qq_ref[h], ckt_ref[0, h * hw:(h + 1) * hw, :].astype(BF16)) for h in heads],
                 [cv_ref[0, pl.ds(h, tk, stride=ATT_HEADS), :].astype(BF16) for h in heads])

    @pl.when(j == pl.num_programs(1) - 1)
    def _():
        lam = _lambda(lq1_ref, lk1_ref, lq2_ref, lk2_ref, lam_init)
        cols = [slice(h * hw, (h + 1) * hw) for h in heads]
        softmax_step([_dot_nt(qq_ref[h], kn_ref[0, :, cols[h]]) for h in heads],
                     [vn_ref[0, :, cols[h]] for h in heads])
        for h in heads:
            o = acc_ref[h] / l_ref[h]
            o_ref[0, :, cols[h]] = _subln(o[:ts] - lam * o[ts:], g_ref[...], lam_init).astype(o_ref.dtype)


def _attn_sample_call(q, cache_kt, cache_v, k_new, v_new, lq1, lk1, lq2, lk2, subln_g, lam_init, tk):
    b, ts, _ = q.shape
    past = cache_kt.shape[2]
    hw = 2 * ATT_DH
    new = pl.BlockSpec((1, ts, ATT_W), lambda bi, j: (bi, 0, 0))
    return pl.pallas_call(
        functools.partial(_attn_sample_body, lam_init, ts),
        grid=(b, past // tk),
        in_specs=[new, pl.BlockSpec((1, ATT_W, tk), lambda bi, j: (bi, 0, j)),
                  pl.BlockSpec((1, tk * ATT_HEADS, hw), lambda bi, j: (bi, j, 0)), new, new]
        + [_const_spec((1, ATT_DH))] * 4 + [_const_spec((1, hw))],
        out_specs=new,
        out_shape=jax.ShapeDtypeStruct((b, ts, ATT_W), BF16),
        scratch_shapes=[pltpu.VMEM((ATT_HEADS, 2 * ts, hw), BF16), pltpu.VMEM((ATT_HEADS, 2 * ts, 1), F32),
                        pltpu.VMEM((ATT_HEADS, 2 * ts, 1), F32), pltpu.VMEM((ATT_HEADS, 2 * ts, hw), F32)],
        compiler_params=_params(("parallel", "arbitrary")),
        name="attn_sample",
    )(q, cache_kt, cache_v, k_new, v_new, lq1, lk1, lq2, lk2, subln_g)


def _split2(x):
    hi = x.astype(BF16)
    return hi, (x - hi.astype(F32)).astype(BF16)


def _head_sum(x, ones_bd, passes=2):
    parts = []
    for g in range(x.shape[1] // MXU_W):
        xg = x[:, g * MXU_W:(g + 1) * MXU_W]
        if passes == 1:
            parts.append(_dot(xg.astype(BF16), ones_bd))
        else:
            hi, lo = _split2(xg)
            parts.append(_dot(hi, ones_bd) + _dot(lo, ones_bd))
    return jnp.concatenate(parts, axis=1)


def _bd_expand(x):
    c = x.shape[0]
    xt = jnp.concatenate([x] * WKV_GROUP, axis=0)
    row = lax.broadcasted_iota(jnp.int32, xt.shape, 0)
    lane = lax.broadcasted_iota(jnp.int32, xt.shape, 1)
    return jnp.where(row // c == lane // RW_HEAD, xt, jnp.zeros_like(xt))


def _wkv_chunk_terms(units):
    c = units[0][0].shape[0]
    slab = (c, WKV_GROUP * c)
    t_idx = lax.broadcasted_iota(jnp.int32, slab, 0)
    s_idx = lax.broadcasted_iota(jnp.int32, slab, 1) % c
    strict = s_idx < t_idx
    incl = s_idx <= t_idx
    eye = jnp.where(s_idx == t_idx, 1.0, 0.0)
    expand = lambda x: _bd_expand(x).astype(BF16)

    ar = [jnp.concatenate([u[0], u[1]], axis=0).astype(BF16) for u in units]
    def expand_t(x):
        xt = jnp.concatenate([x] * WKV_GROUP, axis=0).T
        row = lax.broadcasted_iota(jnp.int32, xt.shape, 0)
        col = lax.broadcasted_iota(jnp.int32, xt.shape, 1)
        return jnp.where(row // RW_HEAD == col // c, xt, jnp.zeros_like(xt)).astype(BF16)

    sb = [_dot(a, expand_t(u[2])) for a, u in zip(ar, units)]
    yield
    sk = [_dot(a, expand_t(u[3])) for a, u in zip(ar, units)]
    yield
    l_ab = [jnp.where(strict, x[:c], 0.0) for x in sb]
    m_rb = [jnp.where(incl, x[c:], 0.0).astype(BF16) for x in sb]
    l_ak = [jnp.where(strict, x[:c], 0.0).astype(BF16) for x in sk]
    m_rk = [jnp.where(incl, x[c:], 0.0).astype(BF16) for x in sk]

    lp = l_ab
    tinv = [eye + x for x in l_ab]
    lp = [_dot(x.astype(BF16), expand(x)) for x in lp]
    yield
    power = 2
    while power < c:
        last = 2 * power >= c
        nxt = []
        for i, (x, t) in enumerate(zip(lp, tinv)):
            lhs = t if last else jnp.concatenate([x, t], axis=0)
            prod = _dot(lhs.astype(BF16), expand(x))
            if last:
                tinv[i] = t + prod
            else:
                nxt.append(prod[:c])
                tinv[i] = t + prod[c:]
        lp = nxt
        power *= 2
        yield

    v_exp = [expand(u[4]) for u in units]
    lakv = [_dot(x, ve) for x, ve in zip(l_ak, v_exp)]
    yield
    tinv_bf = [t.astype(BF16) for t in tinv]
    ta = [_dot(t, expand(u[0])) for t, u in zip(tinv_bf, units)]
    yield
    uv = [_dot(t, expand(x)) for t, x in zip(tinv_bf, lakv)]
    yield

    tr = [jnp.concatenate([x, u[1]], axis=0).astype(BF16) for x, u in zip(ta, units)]
    m_rbk = [jnp.concatenate([m, mk], axis=1) for m, mk in zip(m_rb, m_rk)]
    return list(zip(tr, uv, m_rbk, v_exp))


def _wkv_chunk_apply(states, terms, vs, bkhs, decays):
    c = vs[0].shape[0]
    gw = states[0].shape[0]
    bd = (lax.broadcasted_iota(jnp.int32, (gw, gw), 0) // RW_HEAD
          == lax.broadcasted_iota(jnp.int32, (gw, gw), 1) // RW_HEAD)
    ur = [_dot_nt(t[0], s.astype(BF16)) for t, s in zip(terms, states)]
    yield
    u = [x[:c] + t[1] for x, t in zip(ur, terms)]
    upd = [_dot_tn(jnp.concatenate([x, v], axis=0).astype(BF16), bkh) for x, v, bkh in zip(u, vs, bkhs)]
    yield
    ys = [x[c:] + _dot(t[2], jnp.concatenate([_bd_expand(uu).astype(BF16), t[3]], axis=0))
          for x, uu, t in zip(ur, u, terms)]
    yield
    return ys, [s * d + jnp.where(bd, x, 0.0) for s, d, x in zip(states, decays, upd)]


def _interleave(*gens):
    results = [None] * len(gens)
    live = list(range(len(gens)))
    while live:
        for i in list(live):
            try:
                next(gens[i])
            except StopIteration as stop:
                results[i] = stop.value
                live.remove(i)
    return results


def _rwkv_body(bb, tb, t_valid, z_ref, s0_ref, sh0_ref, mu_ref, w0_ref, wa_ref, a0_ref, g2_ref, kk_ref, ka_ref,
               rk_ref, lng_ref, lnb_ref, o_ref, s_ref, h_ref, carry_ref):
    ib = pl.program_id(1)
    c = WKV_C
    nc = tb // c
    lanes_of = lambda g: slice(g * WKV_GW, (g + 1) * WKV_GW)

    @pl.when(ib == 0)
    def _():
        for bi in range(bb):
            carry_ref[bi] = sh0_ref[bi]
            for g in range(WKV_NG):
                blocks = [s0_ref[bi, g * WKV_GROUP + h] for h in range(WKV_GROUP)]
                h_ref[bi, g] = _bd_expand(jnp.concatenate(blocks, axis=1))

    ones_bd = (lax.broadcasted_iota(jnp.int32, (MXU_W, MXU_W), 0) // RW_HEAD
               == lax.broadcasted_iota(jnp.int32, (MXU_W, MXU_W), 1) // RW_HEAD).astype(BF16)

    def prepare(seqs):
        n = len(seqs) * tb
        z = jnp.concatenate([z_ref[bi] for bi in seqs], axis=0)
        mu = mu_ref[...]
        zs = z + (pltpu.roll(z, 1, 0) - z) * mu
        first = lax.broadcasted_iota(jnp.int32, (F32_ROWS, SHIFT_W), 0) == 0
        pieces = []
        for i, bi in enumerate(seqs):
            head = z[i * tb:i * tb + F32_ROWS]
            pieces += [jnp.where(first, head + (carry_ref[bi] - head) * mu, zs[i * tb:i * tb + F32_ROWS]),
                       zs[i * tb + F32_ROWS:(i + 1) * tb]]
            carry_ref[bi] = z[(i + 1) * tb - 1:(i + 1) * tb, :]
        zs = jnp.concatenate(pieces, axis=0)
        yield

        r = zs[:, :RW_W]
        k = zs[:, RW_W:2 * RW_W]
        v = zs[:, 2 * RW_W:3 * RW_W]
        zwa = zs[:, 3 * RW_W:3 * RW_W + W_LORA + A_LORA]
        zg = zs[:, 3 * RW_W + W_LORA + A_LORA:]
        lane = lax.broadcasted_iota(jnp.int32, zwa.shape, 1)
        lora = _dot(jnp.where(lane < W_LORA, jnp.tanh(zwa), zwa).astype(BF16), wa_ref[...])
        ew = math.exp(-0.5) * jax.nn.sigmoid(w0_ref[...] + lora[:, :RW_W])
        a = jax.nn.sigmoid(a0_ref[...] + lora[:, RW_W:])
        yield
        gate = _dot(jax.nn.sigmoid(zg).astype(BF16), g2_ref[...])
        yield

        kk = k * kk_ref[...]
        kk = kk * lax.rsqrt(jnp.maximum(_head_sum(kk * kk, ones_bd, passes=1), 1e-24))
        yield
        k = k * (1.0 + (a - 1.0) * ka_ref[...])
        bonus = _head_sum(r * k * rk_ref[...], ones_bd) * v
        yield

        if t_valid is not None:
            live = lax.broadcasted_iota(jnp.int32, ew.shape, 0) % tb + ib * tb < t_valid
            ew = jnp.where(live, ew, 0.0)
            k = jnp.where(live, k, 0.0)
            kk = jnp.where(live, kk, 0.0)
            v = jnp.where(live, v, 0.0)

        rows = min(n, MXU_W)
        ti = lax.broadcasted_iota(jnp.int32, (rows, rows), 0)
        si = lax.broadcasted_iota(jnp.int32, (rows, rows), 1)
        tri = jnp.logical_and(si <= ti, si // c == ti // c).astype(BF16)
        e_hi, e_lo = _split2(ew)
        cum = -jnp.concatenate([_dot(tri, e_hi[i0:i0 + rows]) + _dot(tri, e_lo[i0:i0 + rows])
                                for i0 in range(0, n, rows)], axis=0)
        gamma_end = jnp.concatenate([jnp.broadcast_to(jnp.exp(cum[i0 + c - 1:i0 + c, :]), (c, RW_W))
                                     for i0 in range(0, n, c)], axis=0)
        yield
        grow = jnp.exp(-cum)
        b = kk * a
        at = -kk * jnp.exp(cum + ew)
        rt = r / grow
        yield
        bt, kt = b * grow, k * grow
        pre = dict(at=at, rt=rt, bt=bt, kt=kt, v=v, gate=gate, bonus=bonus)
        yield
        bh, kh = bt * gamma_end, kt * gamma_end
        for ci in range(nc):
            for i in range(len(seqs)):
                sl = rows_of(ci, i)
                for g in range(WKV_NG):
                    gl = lanes_of(g)
                    pre[ci, i, g] = (v[sl, gl], jnp.concatenate([bh[sl, gl], kh[sl, gl]], axis=0).astype(BF16),
                                     gamma_end[sl, gl][:1])
        return pre

    rows_of = lambda ci, i: slice(i * tb + ci * c, i * tb + (ci + 1) * c)

    def chunk_terms(pre, nseq):
        order = [(ci, i, g) for ci in range(nc) for i in range(nseq) for g in range(WKV_NG)]
        units = [tuple(pre[name][rows_of(ci, i), lanes_of(g)] for name in ("at", "rt", "bt", "kt", "v"))
                 for ci, i, g in order]
        return dict(zip(order, (yield from _wkv_chunk_terms(units))))

    def apply_chunks(seqs, pre, terms):
        chains = [(i, g) for i in range(len(seqs)) for g in range(WKV_NG)]
        states = [h_ref[seqs[i], g] for i, g in chains]
        y_parts = {}
        for ci in range(nc):
            extra = [pre[ci, i, g] for i, g in chains]
            ys, states = yield from _wkv_chunk_apply(states, [terms[ci, i, g] for i, g in chains],
                                                     [e[0] for e in extra], [e[1] for e in extra],
                                                     [e[2] for e in extra])
            for (i, g), y_unit in zip(chains, ys):
                y_parts[ci, i, g] = y_unit
        for (i, g), s_out in zip(chains, states):
            h_ref[seqs[i], g] = s_out
        return jnp.concatenate([jnp.concatenate([y_parts[ci, i, g] for g in range(WKV_NG)], axis=1)
                                for i in range(len(seqs)) for ci in range(nc)], axis=0)

    def finish(seqs, pre, y):
        mean = _head_sum(y, ones_bd) * (1.0 / RW_HEAD)
        yield
        d = y - mean
        var = _head_sum(d * d, ones_bd, passes=1) * (1.0 / RW_HEAD)
        yield
        yn = d * lax.rsqrt(var + LNX_EPS) * lng_ref[...] + lnb_ref[...]
        out = ((yn + pre["bonus"]) * pre["gate"]).astype(o_ref.dtype)
        for i, bi in enumerate(seqs):
            o_ref[bi] = out[i * tb:(i + 1) * tb]

    gsz = max(1, bb // WKV_PIPE_GROUPS)
    groups = [list(range(g0, g0 + gsz)) for g0 in range(0, bb, gsz)]
    stages = (lambda g, _: prepare(groups[g]),
              lambda g, pre: chunk_terms(pre[g], len(groups[g])),
              lambda g, pre, terms: apply_chunks(groups[g], pre[g], terms[g]),
              lambda g, pre, terms, ys: finish(groups[g], pre[g], ys[g]))
    done = [{} for _ in stages]
    for step in range(len(groups) + len(stages) - 1):
        live = [(s, step - s) for s in range(len(stages)) if 0 <= step - s < len(groups)]
        results = _interleave(*[stages[s](g, *done[:s]) if s else stages[0](g, None) for s, g in live])
        for (s, g), res in zip(live, results):
            done[s][g] = res

    @pl.when(ib == pl.num_programs(1) - 1)
    def _():
        for bi in range(bb):
            for g in range(WKV_NG):
                h = h_ref[bi, g]
                for hh in range(WKV_GROUP):
                    blk = h[hh * RW_HEAD:(hh + 1) * RW_HEAD, hh * RW_HEAD:(hh + 1) * RW_HEAD]
                    s_ref[bi, g * WKV_GROUP + hh] = blk


def _rwkv_call(z_rw, s0, shift0, p, bb, tb, t_valid):
    b, t, _ = z_rw.shape
    vec = lambda w: _const_spec((1, w))
    return pl.pallas_call(
        functools.partial(_rwkv_body, bb, tb, t_valid),
        grid=(b // bb, t // tb),
        in_specs=[pl.BlockSpec((bb, tb, SHIFT_W), lambda bi, i: (bi, i, 0)),
                  pl.BlockSpec((bb, RW_HEADS, RW_HEAD, RW_HEAD), lambda bi, i: (bi, 0, 0, 0)),
                  pl.BlockSpec((bb, 1, SHIFT_W), lambda bi, i: (bi, 0, 0)),
                  vec(SHIFT_W), vec(RW_W), _const_spec((W_LORA + A_LORA, 2 * RW_W)), vec(RW_W),
                  _const_spec((G_LORA, RW_W)), vec(RW_W), vec(RW_W), vec(RW_W), vec(RW_W), vec(RW_W)],
        out_specs=[pl.BlockSpec((bb, tb, RW_W), lambda bi, i: (bi, i, 0)),
                   pl.BlockSpec((bb, RW_HEADS, RW_HEAD, RW_HEAD), lambda bi, i: (bi, 0, 0, 0))],
        out_shape=[jax.ShapeDtypeStruct((b, t, RW_W), BF16),
                   jax.ShapeDtypeStruct((b, RW_HEADS, RW_HEAD, RW_HEAD), F32)],
        scratch_shapes=[pltpu.VMEM((bb, WKV_NG, WKV_GW, WKV_GW), F32), pltpu.VMEM((bb, 1, SHIFT_W), F32)],
        compiler_params=_params(("parallel", "arbitrary")),
        name="rwkv",
    )(z_rw, s0, shift0, p["mu"], p["w0"], p["wa"], p["a0"], p["g2"], p["k_k"], p["k_a"], p["r_k"],
      p["lnx_g"], p["lnx_b"])


def _merge_body(h_ref, oa_ref, orw_ref, gate_ref, woa_ref, worw_ref, wout_ref, postg_ref, pre_ref, win_ref,
                wo2_ref, post2_ref, y_ref):
    gates = gate_ref[...]
    merged = (gates[:, :D_MODEL] * _dot(oa_ref[...], woa_ref[...])
              + gates[:, D_MODEL:] * _dot(orw_ref[...], worw_ref[...]))
    h2 = h_ref[...] + _rms(_dot(merged.astype(BF16), wout_ref[...]), postg_ref[...])
    f = _swiglu_ffn(_rms(h2, pre_ref[...]).astype(BF16), win_ref, wo2_ref)
    y_ref[...] = h2 + MACARON * _rms(f, post2_ref[...])


def _merge_call(h, o_att, o_rw, gates, w_o_att, w_o_rwkv, w_out, mix_post_g, pre_g, w_in, w_out2, post_g, tm):
    m = h.shape[0]
    row = lambda w: pl.BlockSpec((tm, w), lambda i: (i, 0))
    vec = _const_spec((1, D_MODEL))
    return pl.pallas_call(
        _merge_body,
        grid=(m // tm,),
        in_specs=[row(D_MODEL), row(ATT_W), row(RW_W), row(GATE_W),
                  _const_spec((ATT_W, D_MODEL)), _const_spec((RW_W, D_MODEL)), _const_spec((D_MODEL, D_MODEL)), vec,
                  vec, _const_spec((D_MODEL, 2 * D_FF)), _const_spec((D_FF, D_MODEL)), vec],
        out_specs=row(D_MODEL),
        out_shape=jax.ShapeDtypeStruct((m, D_MODEL), F32),
        compiler_params=_params(("parallel",)),
        name="merge_ffn",
    )(h, o_att, o_rw, gates, w_o_att, w_o_rwkv, w_out, mix_post_g, pre_g, w_in, w_out2, post_g)


def _pick_tile(m, want):
    t = min(m, want)
    assert m % t == 0
    return t


def _layer(x, pos, l, p, cache_k, cache_v, s0, shift0):
    b, t, _ = x.shape
    m = b * t
    tm = _pick_tile(m, ROW_TILE)
    lam_init = 0.8 - 0.6 * math.exp(-0.3 * l)

    late = p.get("late_f32")
    h = _ffn_call(x.reshape(m, D_MODEL), p["ffn1_pre_g"], p["ffn1_w_in"], p["ffn1_w_out"], p["ffn1_post_g"], tm,
                  cast=tuple(late.values()) if late else ())
    if late:
        h, casted = h
        p = {**p, **dict(zip(late, casted)), "late_f32": None}

    cos_t, sin_t = _rope_tables(pos)
    if t < tm:
        cos_t, sin_t = jnp.tile(cos_t, (tm // t, 1)), jnp.tile(sin_t, (tm // t, 1))
    q, k, k_bf, v, v_bf, z_rw, gates = _mixproj_call(h, p["mix_pre_g"], p["w_in"], cos_t, sin_t, tm,
                                                     t if cache_k is None else None)

    lam_args = (p["att_lambda_q1"], p["att_lambda_k1"], p["att_lambda_q2"], p["att_lambda_k2"], p["att_subln_g"])
    r3 = lambda a: a.reshape(b, t, a.shape[-1])
    if cache_k is None:
        o_att = _attn_prompt_call(q, r3(k_bf), v_bf, *lam_args, lam_init, _pick_tile(t, ATT_BLK))
    else:
        past = cache_k.shape[1]
        cache_kt = cache_k.reshape(b, past, ATT_W).transpose(0, 2, 1)
        cache_vr = cache_v.reshape(b, past * ATT_HEADS, 2 * ATT_DH)
        o_att = _attn_sample_call(r3(q), cache_kt, cache_vr, r3(k_bf), r3(v_bf), *lam_args, lam_init,
                                  _pick_tile(past, CACHE_BLK))

    z_rw3 = r3(z_rw)
    t_pad = -(-t // WKV_C) * WKV_C
    z_in = z_rw3 if t_pad == t else jnp.pad(z_rw3, ((0, 0), (0, t_pad - t), (0, 0)))
    o_rw, s_new = _rwkv_call(z_in, s0, shift0, p["rwkv"], _pick_tile(b, WKV_SEQS), _pick_tile(t_pad, WKV_TB),
                             None if t_pad == t else t)
    o_rw = o_rw[:, :t]

    y = _merge_call(h, o_att.reshape(m, ATT_W), o_rw.reshape(m, RW_W), gates, p["w_o_att"], p["w_o_rwkv"],
                    p["w_out"], p["mix_post_g"], p["ffn2_pre_g"], p["ffn2_w_in"], p["ffn2_w_out"],
                    p["ffn2_post_g"], tm)
    if cache_k is None:
        k_rows = k.reshape(b, ATT_HEADS, 2, ATT_DH, t).transpose(0, 4, 1, 2, 3)
    else:
        k_rows = k.reshape(b, t, ATT_HEADS, 2, ATT_DH)
    return (y.reshape(b, t, D_MODEL), k_rows, v.reshape(b, t, ATT_HEADS, 2 * ATT_DH), s_new, z_rw3[:, -1:]), p


def kernel(x_prompt, x_sample, cache_att_k, cache_att_v, state_rwkv, state_shift, ffn1_pre_g, ffn1_w_in, ffn1_w_out, ffn1_post_g, mix_pre_g, w_in, att_lambda_q1, att_lambda_k1, att_lambda_q2, att_lambda_k2, att_subln_g, rwkv_mu, rwkv_w0, rwkv_w2, rwkv_a0, rwkv_a2, rwkv_g2, rwkv_k_k, rwkv_k_a, rwkv_r_k, rwkv_lnx_g, rwkv_lnx_b, w_o_att, w_o_rwkv, w_out, mix_post_g, ffn2_pre_g, ffn2_w_in, ffn2_w_out, ffn2_post_g):
    depth = w_in.shape[0]
    bp, tp, _ = x_prompt.shape
    bs, ts, _ = x_sample.shape
    past = cache_att_k.shape[2]
    pos_p = jnp.arange(tp)
    pos_s = past + jnp.arange(ts)
    xp, xs = x_prompt, x_sample
    outs_p, outs_s = [], []
    vec = lambda a: a.reshape(1, -1)
    for l in range(depth):
        zeros = jnp.zeros((W_LORA, RW_W), F32)
        wa = jnp.concatenate([jnp.concatenate([rwkv_w2[l], zeros], axis=1),
                              jnp.concatenate([zeros, rwkv_a2[l]], axis=1)], axis=0)
        p = dict(
            ffn1_pre_g=vec(ffn1_pre_g[l]), ffn1_w_in=ffn1_w_in[l].astype(BF16), ffn1_w_out=ffn1_w_out[l].astype(BF16),
            ffn1_post_g=vec(ffn1_post_g[l]), mix_pre_g=vec(mix_pre_g[l]),
            late_f32=dict(w_in=w_in[l], w_o_att=w_o_att[l], w_o_rwkv=w_o_rwkv[l], w_out=w_out[l],
                          ffn2_w_in=ffn2_w_in[l], ffn2_w_out=ffn2_w_out[l]),
            att_lambda_q1=vec(att_lambda_q1[l]), att_lambda_k1=vec(att_lambda_k1[l]),
            att_lambda_q2=vec(att_lambda_q2[l]), att_lambda_k2=vec(att_lambda_k2[l]),
            att_subln_g=vec(att_subln_g[l]),
            rwkv=dict(mu=vec(rwkv_mu[l]), w0=vec(rwkv_w0[l]), wa=wa.astype(BF16), a0=vec(rwkv_a0[l]),
                      g2=rwkv_g2[l].astype(BF16), k_k=vec(rwkv_k_k[l]), k_a=vec(rwkv_k_a[l]), r_k=vec(rwkv_r_k[l]),
                      lnx_g=vec(rwkv_lnx_g[l]), lnx_b=vec(rwkv_lnx_b[l])),
            mix_post_g=vec(mix_post_g[l]), ffn2_pre_g=vec(ffn2_pre_g[l]), ffn2_post_g=vec(ffn2_post_g[l]),
        )
        s0p = jnp.zeros((bp, RW_HEADS, RW_HEAD, RW_HEAD), F32)
        sh0p = jnp.zeros((bp, 1, SHIFT_W), F32)
        (xp, *rest_p), p = _layer(xp, pos_p, l, p, None, None, s0p, sh0p)
        (xs, *rest_s), p = _layer(xs, pos_s, l, p, cache_att_k[l], cache_att_v[l], state_rwkv[l], state_shift[l])
        outs_p.append(rest_p)
        outs_s.append(rest_s)
    stack = lambda outs, i: jnp.stack([o[i] for o in outs], 0)
    return (xp, xs, stack(outs_p, 0), stack(outs_p, 1), stack(outs_p, 2), stack(outs_p, 3),
            stack(outs_s, 0), stack(outs_s, 1), stack(outs_s, 2), stack(outs_s, 3))
```

```python
import functools
import math

import jax
import jax.numpy as jnp
from jax import lax
from jax.experimental import pallas as pl
from jax.experimental.pallas import tpu as pltpu

F32 = jnp.float32
BF16 = jnp.bfloat16

D_MODEL = 1024
D_FF = 2816
CHUNK = 64
ROPE_THETA = 10000.0
EPS = 1e-6
MACARON = 0.5
ATT_HEADS = 4
ATT_DH = 64
ATT_W = ATT_HEADS * 2 * ATT_DH
ATT_HPS = 2
Q_SCALE = ATT_DH ** -0.5 * math.log2(math.e)
RW_HEAD = 64
RW_W = D_MODEL // 2
RW_HEADS = RW_W // RW_HEAD
W_LORA = 64
A_LORA = 64
G_LORA = 128
SHIFT_W = 3 * RW_W + W_LORA + A_LORA + G_LORA
LNX_EPS = 64e-5
GATE_W = 2 * D_MODEL
IN_W = 3 * ATT_W + SHIFT_W + GATE_W

LANES = 128
F32_ROWS = 8
BF16_ROWS = 16
MXU_W = 256
VMEM_LIMIT = 56 * 1024 * 1024

ROW_TILE = 512
ATT_BLK = 512
CACHE_BLK = 1024
CACHE_BUFS = 3

WKV_C = 64
WKV_GROUP = MXU_W // RW_HEAD
WKV_GW = WKV_GROUP * RW_HEAD
WKV_NG = RW_W // WKV_GW
WKV_SEQS = 8
WKV_TB = 256
WKV_PIPE_GROUPS = 2


def _params(sem):
    return pltpu.CompilerParams(dimension_semantics=sem, vmem_limit_bytes=VMEM_LIMIT)


def _const_spec(shape):
    nd = len(shape)
    return pl.BlockSpec(shape, lambda *_: (0,) * nd, pipeline_mode=pl.Buffered(1))


def _rms(x, g):
    return x * lax.rsqrt(jnp.mean(x * x, axis=-1, keepdims=True) + EPS) * g


def _dot(a, b):
    return jnp.dot(a, b, preferred_element_type=F32)


def _dot_nt(a, b):
    return lax.dot_general(a, b, (((1,), (1,)), ((), ())), preferred_element_type=F32)


def _dot_tn(a, b):
    return lax.dot_general(a, b, (((0,), (0,)), ((), ())), preferred_element_type=F32)


def _swiglu_ffn(xn_bf, w_in_ref, w_out_ref):
    hh = _dot(xn_bf, w_in_ref[...])
    gate = hh[:, :D_FF]
    up = hh[:, D_FF:]
    act = (gate * jax.nn.sigmoid(gate) * up).astype(BF16)
    return _dot(act, w_out_ref[...])


def _ffn_body(cast_steps, x_ref, pre_ref, win_ref, wout_ref, post_ref, *refs):
    ncast = len(cast_steps)
    cast_in, o_ref, cast_out = refs[:ncast], refs[ncast], refs[ncast + 1:]
    x = x_ref[...]
    f = _swiglu_ffn(_rms(x, pre_ref[...]).astype(BF16), win_ref, wout_ref)
    o_ref[...] = x + MACARON * _rms(f, post_ref[...])
    for w_ref, c_ref, steps in zip(cast_in, cast_out, cast_steps):
        @pl.when(pl.program_id(0) < steps)
        def _():
            c_ref[...] = w_ref[...].astype(BF16)


def _cast_rows(rows, nsteps):
    return next(r for r in range(BF16_ROWS, rows + 1, BF16_ROWS) if rows % r == 0 and rows // r <= nsteps)


def _ffn_call(x, pre_g, w_in, w_out, post_g, tm, cast=()):
    m = x.shape[0]
    nsteps = m // tm
    row = lambda w: pl.BlockSpec((tm, w), lambda i: (i, 0))
    slabs = [_cast_rows(w.shape[0], nsteps) for w in cast]
    steps = tuple(w.shape[0] // r for w, r in zip(cast, slabs))
    slab_index = lambda i, last: (jnp.minimum(i, last), 0)
    cast_specs = [pl.BlockSpec((r, w.shape[1]), functools.partial(slab_index, last=s - 1))
                  for w, r, s in zip(cast, slabs, steps)]
    out = pl.pallas_call(
        functools.partial(_ffn_body, steps),
        grid=(nsteps,),
        in_specs=[row(D_MODEL), _const_spec((1, D_MODEL)), _const_spec((D_MODEL, 2 * D_FF)),
                  _const_spec((D_FF, D_MODEL)), _const_spec((1, D_MODEL))] + cast_specs,
        out_specs=[row(D_MODEL)] + cast_specs,
        out_shape=[jax.ShapeDtypeStruct((m, D_MODEL), F32)] + [jax.ShapeDtypeStruct(w.shape, BF16) for w in cast],
        compiler_params=_params(("arbitrary",) if cast else ("parallel",)),
        name="ffn",
    )(x, pre_g, w_in, w_out, post_g, *cast)
    return (out[0], out[1:]) if cast else out[0]


def _rope_tables(pos):
    half = ATT_DH // 2
    inv = ROPE_THETA ** (-jnp.arange(half, dtype=F32) / half)
    ang = pos.astype(F32)[:, None] * inv[None, :]
    cos, sin = jnp.cos(ang), jnp.sin(ang)
    cos_t = jnp.tile(jnp.concatenate([cos, cos], axis=-1), (1, LANES // ATT_DH))
    sin_t = jnp.tile(jnp.concatenate([-sin, sin], axis=-1), (1, LANES // ATT_DH))
    return cos_t, sin_t


def _rope(x, cos_t, sin_t):
    n = x.shape[-1]
    half = ATT_DH // 2
    lane = lax.broadcasted_iota(jnp.int32, x.shape, 1)
    swapped = jnp.where(lane % ATT_DH < half, pltpu.roll(x, n - half, 1), pltpu.roll(x, half, 1))
    return x * cos_t + swapped * sin_t


def _mixproj_body(v_cols, h_ref, g_ref, w_ref, cos_ref, sin_ref, q_ref, k_ref, kb_ref, v_ref, vb_ref, rw_ref,
                  gate_ref):
    u = _rms(h_ref[...], g_ref[...]).astype(BF16)
    z = _dot(u, w_ref[...])
    cos_t = jnp.tile(cos_ref[...], (1, ATT_W // LANES))
    sin_t = jnp.tile(sin_ref[...], (1, ATT_W // LANES))
    q = _rope(z[:, :ATT_W], cos_t, sin_t) * Q_SCALE
    if v_cols:
        q_ref[0] = q.T.astype(BF16)
    else:
        q_ref[...] = q.astype(BF16)
    k = _rope(z[:, ATT_W:2 * ATT_W], cos_t, sin_t)
    kb_ref[...] = k.astype(BF16)
    v = z[:, 2 * ATT_W:3 * ATT_W]
    if v_cols:
        k_ref[0] = k.T
        hw = 2 * ATT_DH
        for h in range(ATT_HEADS):
            v_ref[pl.ds(h, v.shape[0], stride=ATT_HEADS), :] = v[:, h * hw:(h + 1) * hw]
        vb_ref[0] = v.T.astype(BF16)
    else:
        k_ref[...] = k
        v_ref[...] = v
        vb_ref[...] = v.astype(BF16)
    rw_ref[...] = z[:, 3 * ATT_W:3 * ATT_W + SHIFT_W]
    gate_ref[...] = jax.nn.sigmoid(z[:, 3 * ATT_W + SHIFT_W:]).astype(BF16)


def _mixproj_call(h, mix_pre_g, w_in, cos_t, sin_t, tm, v_cols_t):
    m = h.shape[0]
    ntab = cos_t.shape[0] // tm
    row = lambda w: pl.BlockSpec((tm, w), lambda i: (i, 0))
    tab = pl.BlockSpec((tm, LANES), lambda i: (i % ntab, 0))
    sds = lambda w, dt: jax.ShapeDtypeStruct((m, w), dt)
    if v_cols_t is None:
        q_spec, q_shape = row(ATT_W), sds(ATT_W, BF16)
        k_spec, k_shape = row(ATT_W), sds(ATT_W, F32)
        v_spec, v_shape = row(ATT_W), sds(ATT_W, F32)
        vb_spec, vb_shape = row(ATT_W), sds(ATT_W, BF16)
    else:
        nt = v_cols_t // tm
        q_spec = k_spec = vb_spec = pl.BlockSpec((1, ATT_W, tm), lambda i: (i // nt, 0, i % nt))
        k_shape = jax.ShapeDtypeStruct((m // v_cols_t, ATT_W, v_cols_t), F32)
        q_shape = vb_shape = jax.ShapeDtypeStruct((m // v_cols_t, ATT_W, v_cols_t), BF16)
        v_spec = pl.BlockSpec((tm * ATT_HEADS, 2 * ATT_DH), lambda i: (i, 0))
        v_shape = jax.ShapeDtypeStruct((m * ATT_HEADS, 2 * ATT_DH), F32)
    return pl.pallas_call(
        functools.partial(_mixproj_body, v_cols_t is not None),
        grid=(m // tm,),
        in_specs=[row(D_MODEL), _const_spec((1, D_MODEL)), _const_spec((D_MODEL, IN_W)), tab, tab],
        out_specs=[q_spec, k_spec, row(ATT_W), v_spec, vb_spec, row(SHIFT_W), row(GATE_W)],
        out_shape=[q_shape, k_shape, sds(ATT_W, BF16), v_shape, vb_shape,
                   sds(SHIFT_W, F32), sds(GATE_W, BF16)],
        compiler_params=_params(("parallel",)),
        name="mixproj",
    )(h, mix_pre_g, w_in, cos_t, sin_t)


def _lambda(lq1_ref, lk1_ref, lq2_ref, lk2_ref, lam_init):
    s1 = jnp.sum(lq1_ref[...] * lk1_ref[...], axis=-1, keepdims=True)
    s2 = jnp.sum(lq2_ref[...] * lk2_ref[...], axis=-1, keepdims=True)
    return jnp.exp(s1) - jnp.exp(s2) + lam_init


def _subln(o, g, lam_init):
    return o * lax.rsqrt(jnp.mean(o * o, axis=-1, keepdims=True) + EPS) * g * (1.0 - lam_init)


def _attn_prompt_body(lam_init, blk, qt_ref, k_ref, vt_ref, lq1_ref, lk1_ref, lq2_ref, lk2_ref, g_ref, o_ref,
                      qq_ref, m_ref, l_ref, acc_ref, st_ref):
    i = pl.program_id(2)
    hw = 2 * ATT_DH
    heads = range(ATT_HPS)
    for h in heads:
        qt = qt_ref[0, h * hw:(h + 1) * hw, :]
        feat = lax.broadcasted_iota(jnp.int32, qt.shape, 0)
        zero = jnp.zeros_like(qt)
        qq_ref[h, :, :blk] = jnp.where(feat < ATT_DH, qt, zero)
        qq_ref[h, :, blk:] = jnp.where(feat >= ATT_DH, qt, zero)
    m_ref[...] = jnp.full(m_ref.shape, -jnp.inf, F32)
    l_ref[...] = jnp.zeros(l_ref.shape, F32)
    acc_ref[...] = jnp.zeros(acc_ref.shape, F32)

    def scores(h, j):
        start = pl.multiple_of(j * blk, blk)
        st_ref[h] = _dot(k_ref[0, pl.ds(start, blk), h * hw:(h + 1) * hw], qq_ref[h])

    ones = jnp.ones((BF16_ROWS, blk), BF16)

    def consume(h, j, diagonal):
        start = pl.multiple_of(j * blk, blk)
        st = st_ref[h]
        if diagonal:
            key = lax.broadcasted_iota(jnp.int32, st.shape, 0)
            qry = lax.broadcasted_iota(jnp.int32, st.shape, 1) % blk
            st = jnp.where(key // CHUNK <= qry // CHUNK, st, -jnp.inf)
        m_prev = m_ref[h]
        m_new = jnp.maximum(m_prev, jnp.max(st, axis=0, keepdims=True))
        alpha = jnp.exp2(m_prev - m_new)
        p = jnp.exp2(st - m_new).astype(BF16)
        lhs = jnp.concatenate([vt_ref[0, h * hw:(h + 1) * hw, pl.ds(start, blk)], ones], axis=0)
        pv = _dot(lhs, p)
        l_ref[h] = alpha * l_ref[h] + pv[hw:hw + 1]
        acc_ref[h] = alpha * acc_ref[h] + pv[:hw]
        m_ref[h] = m_new

    def block(j, carry):
        scores(1, j)
        consume(0, j, False)
        scores(0, j + 1)
        consume(1, j, False)
        return carry

    scores(0, 0)
    lax.fori_loop(0, i, block, 0)
    scores(1, i)
    consume(0, i, True)
    consume(1, i, True)

    lam = _lambda(lq1_ref, lk1_ref, lq2_ref, lk2_ref, lam_init)
    for h in heads:
        o = acc_ref[h] / l_ref[h]
        o = o[:, :blk] - lam * o[:, blk:]
        o = o * lax.rsqrt(jnp.mean(o * o, axis=0, keepdims=True) + EPS) * g_ref[...] * (1.0 - lam_init)
        o_ref[0, :, h * hw:(h + 1) * hw] = o.T.astype(o_ref.dtype)


def _attn_prompt_call(qt, k, vt, lq1, lk1, lq2, lk2, subln_g, lam_init, blk):
    b, t, _ = k.shape
    hw = 2 * ATT_DH
    gw = ATT_HPS * hw
    qtspec = pl.BlockSpec((1, gw, blk), lambda bi, hi, i: (bi, hi, i))
    kspec = pl.BlockSpec((1, t, gw), lambda bi, hi, i: (bi, 0, hi))
    vtspec = pl.BlockSpec((1, gw, t), lambda bi, hi, i: (bi, hi, 0))
    return pl.pallas_call(
        functools.partial(_attn_prompt_body, lam_init, blk),
        grid=(b, ATT_HEADS // ATT_HPS, t // blk),
        in_specs=[qtspec, kspec, vtspec] + [_const_spec((1, ATT_DH))] * 4 + [_const_spec((hw, 1))],
        out_specs=pl.BlockSpec((1, blk, gw), lambda bi, hi, i: (bi, i, hi)),
        out_shape=jax.ShapeDtypeStruct((b, t, ATT_W), BF16),
        scratch_shapes=[pltpu.VMEM((ATT_HPS, hw, 2 * blk), BF16), pltpu.VMEM((ATT_HPS, 1, 2 * blk), F32),
                        pltpu.VMEM((ATT_HPS, 1, 2 * blk), F32), pltpu.VMEM((ATT_HPS, hw, 2 * blk), F32),
                        pltpu.VMEM((ATT_HPS, blk, 2 * blk), F32)],
        compiler_params=_params(("parallel", "parallel", "arbitrary")),
        name="attn_prompt",
    )(qt, k, vt, lq1, lk1, lq2, lk2, subln_g.reshape(hw, 1))


def _attn_sample_body(lam_init, ts, nb, nblk, q_ref, ckt_hbm, cv_hbm, kn_ref, vn_ref, lq1_ref, lk1_ref, lq2_ref,
                      lk2_ref, g_ref, o_ref, qq_ref, m_ref, l_ref, acc_ref, kbuf, vbuf, sem):
    j = pl.program_id(1)
    hw = 2 * ATT_DH
    heads = range(ATT_HEADS)
    nbuf = kbuf.shape[0]
    tk = kbuf.shape[2]
    g = pl.program_id(0) * nblk + j

    def block_copies(gg):
        slot, seq = gg % nbuf, gg // nblk
        k0 = pl.multiple_of((gg % nblk) * tk, tk)
        v0 = pl.multiple_of((gg % nblk) * tk * ATT_HEADS, tk * ATT_HEADS)
        return (pltpu.make_async_copy(ckt_hbm.at[seq, :, pl.ds(k0, tk)], kbuf.at[slot], sem.at[0, slot]),
                pltpu.make_async_copy(cv_hbm.at[seq, pl.ds(v0, tk * ATT_HEADS), :], vbuf.at[slot], sem.at[1, slot]))

    @pl.when(g == 0)
    def _():
        for gg in range(min(nbuf - 1, nb * nblk)):
            for copy in block_copies(jnp.int32(gg)):
                copy.start()

    @pl.when(g + nbuf - 1 < nb * nblk)
    def _():
        for copy in block_copies(g + nbuf - 1):
            copy.start()

    for copy in block_copies(g):
        copy.wait()
    slot = g % nbuf

    @pl.when(j == 0)
    def _():
        for h in heads:
            q = q_ref[0, :, h * hw:(h + 1) * hw]
            lane = lax.broadcasted_iota(jnp.int32, q.shape, 1)
            zero = jnp.zeros_like(q)
            qq_ref[h, :ts, :] = jnp.where(lane < ATT_DH, q, zero)
            qq_ref[h, ts:, :] = jnp.where(lane >= ATT_DH, q, zero)
        m_ref[...] = jnp.full(m_ref.shape, -jnp.inf, F32)
        l_ref[...] = jnp.zeros(l_ref.shape, F32)
        acc_ref[...] = jnp.zeros(acc_ref.shape, F32)

    def softmax_step(ss, vs):
        m_prev = [m_ref[h] for h in heads]
        m_new = [jnp.maximum(m, jnp.max(s, axis=-1, keepdims=True)) for m, s in zip(m_prev, ss)]
        alpha = [jnp.exp2(m - mn) for m, mn in zip(m_prev, m_new)]
        p = [jnp.exp2(s - mn) for s, mn in zip(ss, m_new)]
        pv = [_dot(x.astype(BF16), v) for x, v in zip(p, vs)]
        for h in heads:
            l_ref[h] = alpha[h] * l_ref[h] + jnp.sum(p[h], axis=-1, keepdims=True)
            acc_ref[h] = alpha[h] * acc_ref[h] + pv[h]
            m_ref[h] = m_new[h]

    softmax_step([_dot(qq_ref[h], kbuf[slot, h * hw:(h + 1) * hw, :].astype(BF16)) for h in heads],
                 [vbuf[slot, pl.ds(h, tk, stride=ATT_HEADS), :].astype(BF16) for h in heads])

    @pl.when(j == pl.num_programs(1) - 1)
    def _():
        lam = _lambda(lq1_ref, lk1_ref, lq2_ref, lk2_ref, lam_init)
        cols = [slice(h * hw, (h + 1) * hw) for h in heads]
        softmax_step([_dot_nt(qq_ref[h], kn_ref[0, :, cols[h]]) for h in heads],
                     [vn_ref[0, :, cols[h]] for h in heads])
        for h in heads:
            o = acc_ref[h] / l_ref[h]
            o_ref[0, :, cols[h]] = _subln(o[:ts] - lam * o[ts:], g_ref[...], lam_init).astype(o_ref.dtype)


def _attn_sample_call(q, cache_kt, cache_v, k_new, v_new, lq1, lk1, lq2, lk2, subln_g, lam_init, tk):
    b, ts, _ = q.shape
    past = cache_kt.shape[2]
    hw = 2 * ATT_DH
    new = pl.BlockSpec((1, ts, ATT_W), lambda bi, j: (bi, 0, 0))
    hbm = pl.BlockSpec(memory_space=pl.ANY)
    return pl.pallas_call(
        functools.partial(_attn_sample_body, lam_init, ts, b, past // tk),
        grid=(b, past // tk),
        in_specs=[new, hbm, hbm, new, new] + [_const_spec((1, ATT_DH))] * 4 + [_const_spec((1, hw))],
        out_specs=new,
        out_shape=jax.ShapeDtypeStruct((b, ts, ATT_W), BF16),
        scratch_shapes=[pltpu.VMEM((ATT_HEADS, 2 * ts, hw), BF16), pltpu.VMEM((ATT_HEADS, 2 * ts, 1), F32),
                        pltpu.VMEM((ATT_HEADS, 2 * ts, 1), F32), pltpu.VMEM((ATT_HEADS, 2 * ts, hw), F32),
                        pltpu.VMEM((CACHE_BUFS, ATT_W, tk), F32), pltpu.VMEM((CACHE_BUFS, tk * ATT_HEADS, hw), F32),
                        pltpu.SemaphoreType.DMA((2, CACHE_BUFS))],
        compiler_params=_params(("arbitrary", "arbitrary")),
        name="attn_sample",
    )(q, cache_kt, cache_v, k_new, v_new, lq1, lk1, lq2, lk2, subln_g)


def _split2(x):
    hi = x.astype(BF16)
    return hi, (x - hi.astype(F32)).astype(BF16)


def _head_sum(x, ones_bd, passes=2):
    parts = []
    for g in range(x.shape[1] // MXU_W):
        xg = x[:, g * MXU_W:(g + 1) * MXU_W]
        if passes == 1:
            parts.append(_dot(xg.astype(BF16), ones_bd))
        else:
            hi, lo = _split2(xg)
            parts.append(_dot(hi, ones_bd) + _dot(lo, ones_bd))
    return jnp.concatenate(parts, axis=1)


def _bd_expand(x):
    c = x.shape[0]
    xt = jnp.concatenate([x] * WKV_GROUP, axis=0)
    row = lax.broadcasted_iota(jnp.int32, xt.shape, 0)
    lane = lax.broadcasted_iota(jnp.int32, xt.shape, 1)
    return jnp.where(row // c == lane // RW_HEAD, xt, jnp.zeros_like(xt))


def _wkv_chunk_terms(units):
    c = units[0][0].shape[0]
    slab = (c, WKV_GROUP * c)
    t_idx = lax.broadcasted_iota(jnp.int32, slab, 0)
    s_idx = lax.broadcasted_iota(jnp.int32, slab, 1) % c
    strict = s_idx < t_idx
    incl = s_idx <= t_idx
    eye = jnp.where(s_idx == t_idx, 1.0, 0.0)
    expand = lambda x: _bd_expand(x).astype(BF16)

    ar = [jnp.concatenate([u[0], u[1]], axis=0).astype(BF16) for u in units]
    def expand_t(x):
        xt = jnp.concatenate([x] * WKV_GROUP, axis=0).T
        row = lax.broadcasted_iota(jnp.int32, xt.shape, 0)
        col = lax.broadcasted_iota(jnp.int32, xt.shape, 1)
        return jnp.where(row // RW_HEAD == col // c, xt, jnp.zeros_like(xt)).astype(BF16)

    sb = [_dot(a, expand_t(u[2])) for a, u in zip(ar, units)]
    yield
    sk = [_dot(a, expand_t(u[3])) for a, u in zip(ar, units)]
    yield
    l_ab = [jnp.where(strict, x[:c], 0.0) for x in sb]
    m_rb = [jnp.where(incl, x[c:], 0.0).astype(BF16) for x in sb]
    l_ak = [jnp.where(strict, x[:c], 0.0).astype(BF16) for x in sk]
    m_rk = [jnp.where(incl, x[c:], 0.0).astype(BF16) for x in sk]

    lp = l_ab
    tinv = [eye + x for x in l_ab]
    lp = [_dot(x.astype(BF16), expand(x)) for x in lp]
    yield
    power = 2
    while power < c:
        last = 2 * power >= c
        nxt = []
        for i, (x, t) in enumerate(zip(lp, tinv)):
            lhs = t if last else jnp.concatenate([x, t], axis=0)
            prod = _dot(lhs.astype(BF16), expand(x))
            if last:
                tinv[i] = t + prod
            else:
                nxt.append(prod[:c])
                tinv[i] = t + prod[c:]
        lp = nxt
        power *= 2
        yield

    v_exp = [expand(u[4]) for u in units]
    lakv = [_dot(x, ve) for x, ve in zip(l_ak, v_exp)]
    yield
    tinv_bf = [t.astype(BF16) for t in tinv]
    ta = [_dot(t, expand(u[0])) for t, u in zip(tinv_bf, units)]
    yield
    uv = [_dot(t, expand(x)) for t, x in zip(tinv_bf, lakv)]
    yield

    tr = [jnp.concatenate([x, u[1]], axis=0).astype(BF16) for x, u in zip(ta, units)]
    m_rbk = [jnp.concatenate([m, mk], axis=1) for m, mk in zip(m_rb, m_rk)]
    return list(zip(tr, uv, m_rbk, v_exp))


def _wkv_chunk_apply(states, terms, vs, bkhs, decays):
    c = vs[0].shape[0]
    gw = states[0].shape[0]
    bd = (lax.broadcasted_iota(jnp.int32, (gw, gw), 0) // RW_HEAD
          == lax.broadcasted_iota(jnp.int32, (gw, gw), 1) // RW_HEAD)
    ur = [_dot_nt(t[0], s.astype(BF16)) for t, s in zip(terms, states)]
    yield
    u = [x[:c] + t[1] for x, t in zip(ur, terms)]
    upd = [_dot_tn(jnp.concatenate([x, v], axis=0).astype(BF16), bkh) for x, v, bkh in zip(u, vs, bkhs)]
    yield
    ys = [x[c:] + _dot(t[2], jnp.concatenate([_bd_expand(uu).astype(BF16), t[3]], axis=0))
          for x, uu, t in zip(ur, u, terms)]
    yield
    return ys, [s * d + jnp.where(bd, x, 0.0) for s, d, x in zip(states, decays, upd)]


def _interleave(*gens):
    results = [None] * len(gens)
    live = list(range(len(gens)))
    while live:
        for i in list(live):
            try:
                next(gens[i])
            except StopIteration as stop:
                results[i] = stop.value
                live.remove(i)
    return results


def _rwkv_body(bb, tb, t_valid, z_ref, s0_ref, sh0_ref, mu_ref, w0_ref, wa_ref, a0_ref, g2_ref, kk_ref, ka_ref,
               rk_ref, lng_ref, lnb_ref, o_ref, s_ref, h_ref, carry_ref):
    ib = pl.program_id(1)
    c = WKV_C
    nc = tb // c
    lanes_of = lambda g: slice(g * WKV_GW, (g + 1) * WKV_GW)

    @pl.when(ib == 0)
    def _():
        for bi in range(bb):
            carry_ref[bi] = sh0_ref[bi]
            for g in range(WKV_NG):
                blocks = [s0_ref[bi, g * WKV_GROUP + h] for h in range(WKV_GROUP)]
                h_ref[bi, g] = _bd_expand(jnp.concatenate(blocks, axis=1))

    ones_bd = (lax.broadcasted_iota(jnp.int32, (MXU_W, MXU_W), 0) // RW_HEAD
               == lax.broadcasted_iota(jnp.int32, (MXU_W, MXU_W), 1) // RW_HEAD).astype(BF16)

    def prepare(seqs):
        n = len(seqs) * tb
        z = jnp.concatenate([z_ref[bi] for bi in seqs], axis=0)
        mu = mu_ref[...]
        zs = z + (pltpu.roll(z, 1, 0) - z) * mu
        first = lax.broadcasted_iota(jnp.int32, (F32_ROWS, SHIFT_W), 0) == 0
        pieces = []
        for i, bi in enumerate(seqs):
            head = z[i * tb:i * tb + F32_ROWS]
            pieces += [jnp.where(first, head + (carry_ref[bi] - head) * mu, zs[i * tb:i * tb + F32_ROWS]),
                       zs[i * tb + F32_ROWS:(i + 1) * tb]]
            carry_ref[bi] = z[(i + 1) * tb - 1:(i + 1) * tb, :]
        zs = jnp.concatenate(pieces, axis=0)
        yield

        r = zs[:, :RW_W]
        k = zs[:, RW_W:2 * RW_W]
        v = zs[:, 2 * RW_W:3 * RW_W]
        zwa = zs[:, 3 * RW_W:3 * RW_W + W_LORA + A_LORA]
        zg = zs[:, 3 * RW_W + W_LORA + A_LORA:]
        lane = lax.broadcasted_iota(jnp.int32, zwa.shape, 1)
        lora = _dot(jnp.where(lane < W_LORA, jnp.tanh(zwa), zwa).astype(BF16), wa_ref[...])
        ew = math.exp(-0.5) * jax.nn.sigmoid(w0_ref[...] + lora[:, :RW_W])
        a = jax.nn.sigmoid(a0_ref[...] + lora[:, RW_W:])
        yield
        gate = _dot(jax.nn.sigmoid(zg).astype(BF16), g2_ref[...])
        yield

        kk = k * kk_ref[...]
        kk = kk * lax.rsqrt(jnp.maximum(_head_sum(kk * kk, ones_bd, passes=1), 1e-24))
        yield
        k = k * (1.0 + (a - 1.0) * ka_ref[...])
        bonus = _head_sum(r * k * rk_ref[...], ones_bd) * v
        yield

        if t_valid is not None:
            live = lax.broadcasted_iota(jnp.int32, ew.shape, 0) % tb + ib * tb < t_valid
            ew = jnp.where(live, ew, 0.0)
            k = jnp.where(live, k, 0.0)
            kk = jnp.where(live, kk, 0.0)
            v = jnp.where(live, v, 0.0)

        rows = min(n, MXU_W)
        ti = lax.broadcasted_iota(jnp.int32, (rows, rows), 0)
        si = lax.broadcasted_iota(jnp.int32, (rows, rows), 1)
        tri = jnp.logical_and(si <= ti, si // c == ti // c).astype(BF16)
        e_hi, e_lo = _split2(ew)
        cum = -jnp.concatenate([_dot(tri, e_hi[i0:i0 + rows]) + _dot(tri, e_lo[i0:i0 + rows])
                                for i0 in range(0, n, rows)], axis=0)
        gamma_end = jnp.concatenate([jnp.broadcast_to(jnp.exp(cum[i0 + c - 1:i0 + c, :]), (c, RW_W))
                                     for i0 in range(0, n, c)], axis=0)
        yield
        grow = jnp.exp(-cum)
        b = kk * a
        at = -kk * jnp.exp(cum + ew)
        rt = r / grow
        yield
        bt, kt = b * grow, k * grow
        pre = dict(at=at, rt=rt, bt=bt, kt=kt, v=v, gate=gate, bonus=bonus)
        yield
        bh, kh = bt * gamma_end, kt * gamma_end
        for ci in range(nc):
            for i in range(len(seqs)):
                sl = rows_of(ci, i)
                for g in range(WKV_NG):
                    gl = lanes_of(g)
                    pre[ci, i, g] = (v[sl, gl], jnp.concatenate([bh[sl, gl], kh[sl, gl]], axis=0).astype(BF16),
                                     gamma_end[sl, gl][:1])
        return pre

    rows_of = lambda ci, i: slice(i * tb + ci * c, i * tb + (ci + 1) * c)

    def chunk_terms(pre, nseq):
        order = [(ci, i, g) for ci in range(nc) for i in range(nseq) for g in range(WKV_NG)]
        units = [tuple(pre[name][rows_of(ci, i), lanes_of(g)] for name in ("at", "rt", "bt", "kt", "v"))
                 for ci, i, g in order]
        return dict(zip(order, (yield from _wkv_chunk_terms(units))))

    def apply_chunks(seqs, pre, terms):
        chains = [(i, g) for i in range(len(seqs)) for g in range(WKV_NG)]
        states = [h_ref[seqs[i], g] for i, g in chains]
        y_parts = {}
        for ci in range(nc):
            extra = [pre[ci, i, g] for i, g in chains]
            ys, states = yield from _wkv_chunk_apply(states, [terms[ci, i, g] for i, g in chains],
                                                     [e[0] for e in extra], [e[1] for e in extra],
                                                     [e[2] for e in extra])
            for (i, g), y_unit in zip(chains, ys):
                y_parts[ci, i, g] = y_unit
        for (i, g), s_out in zip(chains, states):
            h_ref[seqs[i], g] = s_out
        return jnp.concatenate([jnp.concatenate([y_parts[ci, i, g] for g in range(WKV_NG)], axis=1)
                                for i in range(len(seqs)) for ci in range(nc)], axis=0)

    def finish(seqs, pre, y):
        mean = _head_sum(y, ones_bd) * (1.0 / RW_HEAD)
        yield
        d = y - mean
        var = _head_sum(d * d, ones_bd, passes=1) * (1.0 / RW_HEAD)
        yield
        yn = d * lax.rsqrt(var + LNX_EPS) * lng_ref[...] + lnb_ref[...]
        out = ((yn + pre["bonus"]) * pre["gate"]).astype(o_ref.dtype)
        for i, bi in enumerate(seqs):
            o_ref[bi] = out[i * tb:(i + 1) * tb]

    gsz = max(1, bb // WKV_PIPE_GROUPS)
    groups = [list(range(g0, g0 + gsz)) for g0 in range(0, bb, gsz)]
    stages = (lambda g, _: prepare(groups[g]),
              lambda g, pre: chunk_terms(pre[g], len(groups[g])),
              lambda g, pre, terms: apply_chunks(groups[g], pre[g], terms[g]),
              lambda g, pre, terms, ys: finish(groups[g], pre[g], ys[g]))
    done = [{} for _ in stages]
    for step in range(len(groups) + len(stages) - 1):
        live = [(s, step - s) for s in range(len(stages)) if 0 <= step - s < len(groups)]
        results = _interleave(*[stages[s](g, *done[:s]) if s else stages[0](g, None) for s, g in live])
        for (s, g), res in zip(live, results):
            done[s][g] = res

    @pl.when(ib == pl.num_programs(1) - 1)
    def _():
        for bi in range(bb):
            for g in range(WKV_NG):
                h = h_ref[bi, g]
                for hh in range(WKV_GROUP):
                    blk = h[hh * RW_HEAD:(hh + 1) * RW_HEAD, hh * RW_HEAD:(hh + 1) * RW_HEAD]
                    s_ref[bi, g * WKV_GROUP + hh] = blk


def _rwkv_call(z_rw, s0, shift0, p, bb, tb, t_valid):
    b, t, _ = z_rw.shape
    vec = lambda w: _const_spec((1, w))
    return pl.pallas_call(
        functools.partial(_rwkv_body, bb, tb, t_valid),
        grid=(b // bb, t // tb),
        in_specs=[pl.BlockSpec((bb, tb, SHIFT_W), lambda bi, i: (bi, i, 0)),
                  pl.BlockSpec((bb, RW_HEADS, RW_HEAD, RW_HEAD), lambda bi, i: (bi, 0, 0, 0)),
                  pl.BlockSpec((bb, 1, SHIFT_W), lambda bi, i: (bi, 0, 0)),
                  vec(SHIFT_W), vec(RW_W), _const_spec((W_LORA + A_LORA, 2 * RW_W)), vec(RW_W),
                  _const_spec((G_LORA, RW_W)), vec(RW_W), vec(RW_W), vec(RW_W), vec(RW_W), vec(RW_W)],
        out_specs=[pl.BlockSpec((bb, tb, RW_W), lambda bi, i: (bi, i, 0)),
                   pl.BlockSpec((bb, RW_HEADS, RW_HEAD, RW_HEAD), lambda bi, i: (bi, 0, 0, 0))],
        out_shape=[jax.ShapeDtypeStruct((b, t, RW_W), BF16),
                   jax.ShapeDtypeStruct((b, RW_HEADS, RW_HEAD, RW_HEAD), F32)],
        scratch_shapes=[pltpu.VMEM((bb, WKV_NG, WKV_GW, WKV_GW), F32), pltpu.VMEM((bb, 1, SHIFT_W), F32)],
        compiler_params=_params(("parallel", "arbitrary")),
        name="rwkv",
    )(z_rw, s0, shift0, p["mu"], p["w0"], p["wa"], p["a0"], p["g2"], p["k_k"], p["k_a"], p["r_k"],
      p["lnx_g"], p["lnx_b"])


def _merge_body(h_ref, oa_ref, orw_ref, gate_ref, woa_ref, worw_ref, wout_ref, postg_ref, pre_ref, win_ref,
                wo2_ref, post2_ref, y_ref):
    gates = gate_ref[...]
    merged = (gates[:, :D_MODEL] * _dot(oa_ref[...], woa_ref[...])
              + gates[:, D_MODEL:] * _dot(orw_ref[...], worw_ref[...]))
    h2 = h_ref[...] + _rms(_dot(merged.astype(BF16), wout_ref[...]), postg_ref[...])
    f = _swiglu_ffn(_rms(h2, pre_ref[...]).astype(BF16), win_ref, wo2_ref)
    y_ref[...] = h2 + MACARON * _rms(f, post2_ref[...])


def _merge_call(h, o_att, o_rw, gates, w_o_att, w_o_rwkv, w_out, mix_post_g, pre_g, w_in, w_out2, post_g, tm):
    m = h.shape[0]
    row = lambda w: pl.BlockSpec((tm, w), lambda i: (i, 0))
    vec = _const_spec((1, D_MODEL))
    return pl.pallas_call(
        _merge_body,
        grid=(m // tm,),
        in_specs=[row(D_MODEL), row(ATT_W), row(RW_W), row(GATE_W),
                  _const_spec((ATT_W, D_MODEL)), _const_spec((RW_W, D_MODEL)), _const_spec((D_MODEL, D_MODEL)), vec,
                  vec, _const_spec((D_MODEL, 2 * D_FF)), _const_spec((D_FF, D_MODEL)), vec],
        out_specs=row(D_MODEL),
        out_shape=jax.ShapeDtypeStruct((m, D_MODEL), F32),
        compiler_params=_params(("parallel",)),
        name="merge_ffn",
    )(h, o_att, o_rw, gates, w_o_att, w_o_rwkv, w_out, mix_post_g, pre_g, w_in, w_out2, post_g)


def _pick_tile(m, want):
    t = min(m, want)
    assert m % t == 0
    return t


def _layer(x, pos, l, p, cache_k, cache_v, s0, shift0):
    b, t, _ = x.shape
    m = b * t
    tm = _pick_tile(m, ROW_TILE)
    lam_init = 0.8 - 0.6 * math.exp(-0.3 * l)

    late = p.get("late_f32")
    h = _ffn_call(x.reshape(m, D_MODEL), p["ffn1_pre_g"], p["ffn1_w_in"], p["ffn1_w_out"], p["ffn1_post_g"], tm,
                  cast=tuple(late.values()) if late else ())
    if late:
        h, casted = h
        p = {**p, **dict(zip(late, casted)), "late_f32": None}

    cos_t, sin_t = _rope_tables(pos)
    if t < tm:
        cos_t, sin_t = jnp.tile(cos_t, (tm // t, 1)), jnp.tile(sin_t, (tm // t, 1))
    q, k, k_bf, v, v_bf, z_rw, gates = _mixproj_call(h, p["mix_pre_g"], p["w_in"], cos_t, sin_t, tm,
                                                     t if cache_k is None else None)

    lam_args = (p["att_lambda_q1"], p["att_lambda_k1"], p["att_lambda_q2"], p["att_lambda_k2"], p["att_subln_g"])
    r3 = lambda a: a.reshape(b, t, a.shape[-1])
    if cache_k is None:
        o_att = _attn_prompt_call(q, r3(k_bf), v_bf, *lam_args, lam_init, _pick_tile(t, ATT_BLK))
    else:
        past = cache_k.shape[1]
        cache_kt = cache_k.reshape(b, past, ATT_W).transpose(0, 2, 1)
        cache_vr = cache_v.reshape(b, past * ATT_HEADS, 2 * ATT_DH)
        o_att = _attn_sample_call(r3(q), cache_kt, cache_vr, r3(k_bf), r3(v_bf), *lam_args, lam_init,
                                  _pick_tile(past, CACHE_BLK))

    z_rw3 = r3(z_rw)
    t_pad = -(-t // WKV_C) * WKV_C
    z_in = z_rw3 if t_pad == t else jnp.pad(z_rw3, ((0, 0), (0, t_pad - t), (0, 0)))
    o_rw, s_new = _rwkv_call(z_in, s0, shift0, p["rwkv"], _pick_tile(b, WKV_SEQS), _pick_tile(t_pad, WKV_TB),
                             None if t_pad == t else t)
    o_rw = o_rw[:, :t]

    y = _merge_call(h, o_att.reshape(m, ATT_W), o_rw.reshape(m, RW_W), gates, p["w_o_att"], p["w_o_rwkv"],
                    p["w_out"], p["mix_post_g"], p["ffn2_pre_g"], p["ffn2_w_in"], p["ffn2_w_out"],
                    p["ffn2_post_g"], tm)
    if cache_k is None:
        k_rows = k.reshape(b, ATT_HEADS, 2, ATT_DH, t).transpose(0, 4, 1, 2, 3)
    else:
        k_rows = k.reshape(b, t, ATT_HEADS, 2, ATT_DH)
    return (y.reshape(b, t, D_MODEL), k_rows, v.reshape(b, t, ATT_HEADS, 2 * ATT_DH), s_new, z_rw3[:, -1:]), p


def kernel(x_prompt, x_sample, cache_att_k, cache_att_v, state_rwkv, state_shift, ffn1_pre_g, ffn1_w_in, ffn1_w_out, ffn1_post_g, mix_pre_g, w_in, att_lambda_q1, att_lambda_k1, att_lambda_q2, att_lambda_k2, att_subln_g, rwkv_mu, rwkv_w0, rwkv_w2, rwkv_a0, rwkv_a2, rwkv_g2, rwkv_k_k, rwkv_k_a, rwkv_r_k, rwkv_lnx_g, rwkv_lnx_b, w_o_att, w_o_rwkv, w_out, mix_post_g, ffn2_pre_g, ffn2_w_in, ffn2_w_out, ffn2_post_g):
    depth = w_in.shape[0]
    bp, tp, _ = x_prompt.shape
    bs, ts, _ = x_sample.shape
    past = cache_att_k.shape[2]
    pos_p = jnp.arange(tp)
    pos_s = past + jnp.arange(ts)
    xp, xs = x_prompt, x_sample
    outs_p, outs_s = [], []
    vec = lambda a: a.reshape(1, -1)
    for l in range(depth):
        zeros = jnp.zeros((W_LORA, RW_W), F32)
        wa = jnp.concatenate([jnp.concatenate([rwkv_w2[l], zeros], axis=1),
                              jnp.concatenate([zeros, rwkv_a2[l]], axis=1)], axis=0)
        p = dict(
            ffn1_pre_g=vec(ffn1_pre_g[l]), ffn1_w_in=ffn1_w_in[l].astype(BF16), ffn1_w_out=ffn1_w_out[l].astype(BF16),
            ffn1_post_g=vec(ffn1_post_g[l]), mix_pre_g=vec(mix_pre_g[l]),
            late_f32=dict(w_in=w_in[l], w_o_att=w_o_att[l], w_o_rwkv=w_o_rwkv[l], w_out=w_out[l],
                          ffn2_w_in=ffn2_w_in[l], ffn2_w_out=ffn2_w_out[l]),
            att_lambda_q1=vec(att_lambda_q1[l]), att_lambda_k1=vec(att_lambda_k1[l]),
            att_lambda_q2=vec(att_lambda_q2[l]), att_lambda_k2=vec(att_lambda_k2[l]),
            att_subln_g=vec(att_subln_g[l]),
            rwkv=dict(mu=vec(rwkv_mu[l]), w0=vec(rwkv_w0[l]), wa=wa.astype(BF16), a0=vec(rwkv_a0[l]),
                      g2=rwkv_g2[l].astype(BF16), k_k=vec(rwkv_k_k[l]), k_a=vec(rwkv_k_a[l]), r_k=vec(rwkv_r_k[l]),
                      lnx_g=vec(rwkv_lnx_g[l]), lnx_b=vec(rwkv_lnx_b[l])),
            mix_post_g=vec(mix_post_g[l]), ffn2_pre_g=vec(ffn2_pre_g[l]), ffn2_post_g=vec(ffn2_post_g[l]),
        )
        s0p = jnp.zeros((bp, RW_HEADS, RW_HEAD, RW_HEAD), F32)
        sh0p = jnp.zeros((bp, 1, SHIFT_W), F32)
        (xp, *rest_p), p = _layer(xp, pos_p, l, p, None, None, s0p, sh0p)
        (xs, *rest_s), p = _layer(xs, pos_s, l, p, cache_att_k[l], cache_att_v[l], state_rwkv[l], state_shift[l])
        outs_p.append(rest_p)
        outs_s.append(rest_s)
    stack = lambda outs, i: jnp.stack([o[i] for o in outs], 0)
    return (xp, xs, stack(outs_p, 0), stack(outs_p, 1), stack(outs_p, 2), stack(outs_p, 3),
            stack(outs_s, 0), stack(outs_s, 1), stack(outs_s, 2), stack(outs_s, 3))
```
